```python
import math
import jax
import jax.numpy as jnp
from jax import lax
import numpy as np

D_MODEL = 1024
BATCH = 8
SEQ = 8192
DEPTH = 2

MEM_LEN = 256
NORM_EPS = 1e-6
MASK_VALUE = -1e30
TINY = 1e-30

HG_HEADS = 4
HG_DK = 128
HG_DV = 128
HG_WIDTH = HG_HEADS * HG_DK
HG_VWIDTH = HG_HEADS * HG_DV
HG_CHUNK = 64

DA_HEADS = 4
DA_DK = 64
DA_DV = 2 * DA_DK
DA_WIDTH = DA_HEADS * 2 * DA_DK
DA_VWIDTH = DA_HEADS * DA_DV
Q_BLOCK = 128
ALIBI_MAX_BIAS = 8.0

RW_HEADS = 8
RW_HEAD = 64
RW_WIDTH = RW_HEADS * RW_HEAD
RW_DECAY_RANK = 64
RW_A_RANK = 64
RW_GATE_RANK = 128
RW_COLS = 3 * RW_WIDTH + RW_DECAY_RANK + RW_A_RANK + RW_GATE_RANK
RW_GN_EPS = 64e-5
RW_SPLITS = (RW_WIDTH, 2 * RW_WIDTH, 3 * RW_WIDTH, 3 * RW_WIDTH + RW_DECAY_RANK,
             3 * RW_WIDTH + RW_DECAY_RANK + RW_A_RANK)

N_BRANCH = 3
BRANCH_WIDTH = 512

OFF_HG_F = HG_WIDTH
OFF_HG_I = 2 * HG_WIDTH
OFF_HG_G = OFF_HG_I + HG_VWIDTH
OFF_DA_Q = OFF_HG_G + HG_VWIDTH
OFF_DA_K = OFF_DA_Q + DA_WIDTH
OFF_DA_V = OFF_DA_K + DA_WIDTH
OFF_RW = OFF_DA_V + DA_VWIDTH
OFF_GATE = OFF_RW + RW_COLS
N_IN = OFF_GATE + N_BRANCH * D_MODEL
IN_SPLITS = (OFF_HG_F, OFF_HG_I, OFF_HG_G, OFF_DA_Q, OFF_DA_K, OFF_DA_V, OFF_RW, OFF_GATE)

XA_HEADS = 4
XA_HEAD = D_MODEL // XA_HEADS

D_FF = 2816
CONV_W = 3

kernel_name = "hybrid_hgrn2_diffattn_rwkv7_gated_block"


def rms_norm(x, g, eps=NORM_EPS):
    xf = x.astype(jnp.float32)
    y = xf * lax.rsqrt(jnp.mean(xf * xf, axis=-1, keepdims=True) + eps)
    return (y * g.astype(jnp.float32)).astype(x.dtype)


def group_norm(y, g, b, eps):
    mu = jnp.mean(y, axis=-1, keepdims=True)
    var = jnp.mean(jnp.square(y - mu), axis=-1, keepdims=True)
    return (y - mu) * lax.rsqrt(var + eps) * g.astype(jnp.float32) + b.astype(jnp.float32)


def alibi_slopes(n_heads):
    return 2.0 ** (-ALIBI_MAX_BIAS * jnp.arange(1, n_heads + 1, dtype=jnp.float32) / n_heads)


def hgrn2_branch(q_p, f_p, i_p, g_p, lb, norm_g):
    B, S, _ = q_p.shape
    dt = q_p.dtype
    q = jax.nn.silu(q_p.astype(jnp.float32))
    fp = f_p.astype(jnp.float32)
    lbf = lb.astype(jnp.float32)
    f = lbf + (1.0 - lbf) * jax.nn.sigmoid(fp)
    log_f = jnp.log(jnp.maximum(f, TINY))
    k = (1.0 - lbf) * jax.nn.sigmoid(-fp)
    v = i_p.astype(jnp.float32)
    n_chunks = S // HG_CHUNK

    def to_chunks(t, d):
        return t.reshape(B, n_chunks, HG_CHUNK, HG_HEADS, d).transpose(1, 0, 3, 2, 4)

    xs = (to_chunks(q, HG_DK), to_chunks(k, HG_DK), to_chunks(v, HG_DV), to_chunks(log_f, HG_DK))
    causal = jnp.tril(jnp.ones((HG_CHUNK, HG_CHUNK), dtype=bool))[:, :, None]

    def step(state, chunk):
        qb, kb, vb, gb = chunk
        b = jnp.cumsum(gb, axis=2)
        diff = b[:, :, :, None, :] - b[:, :, None, :, :]
        decay = jnp.exp(jnp.where(causal, diff, MASK_VALUE))
        scores = jnp.einsum('bhtk,bhtsk,bhsk->bhts', qb, decay, kb)
        o = (jnp.einsum('bhts,bhsv->bhtv', scores, vb)
             + jnp.einsum('bhtk,bhkv->bhtv', qb * jnp.exp(b), state))
        b_last = b[:, :, -1:, :]
        state = (jnp.exp(b_last[:, :, 0, :])[..., None] * state
                 + jnp.einsum('bhsk,bhsv->bhkv', kb * jnp.exp(b_last - b), vb))
        return state, o

    s0 = jnp.zeros((B, HG_HEADS, HG_DK, HG_DV), jnp.float32)
    _, o = lax.scan(step, s0, xs)
    o = o.transpose(1, 0, 3, 2, 4).reshape(B, S, HG_HEADS, HG_DV)
    o = rms_norm(o, norm_g.reshape(HG_HEADS, HG_DV)).reshape(B, S, HG_VWIDTH)
    return (o * jax.nn.silu(g_p.astype(jnp.float32))).astype(dt)


def diff_attention(q_p, k_p, v_p, lam, subln_g, lambda_init):
    B, S, _ = q_p.shape
    dt = q_p.dtype
    q = q_p.reshape(B, S, DA_HEADS, 2, DA_DK)
    k = k_p.reshape(B, S, DA_HEADS, 2, DA_DK)
    v = v_p.reshape(B, S, DA_HEADS, DA_DV)
    lf = lam.astype(jnp.float32)
    lam_full = jnp.exp(jnp.sum(lf[0] * lf[1])) - jnp.exp(jnp.sum(lf[2] * lf[3])) + lambda_init
    slopes = alibi_slopes(DA_HEADS)
    scale = DA_DK ** -0.5
    n_blk = S // Q_BLOCK
    q_blocks = q.reshape(B, n_blk, Q_BLOCK, DA_HEADS, 2, DA_DK).swapaxes(0, 1)
    k_pos = jnp.arange(S)

    def block(args):
        qb, start = args
        dist = (start + jnp.arange(Q_BLOCK))[:, None] - k_pos[None, :]
        bias = -slopes[:, None, None] * dist.astype(jnp.float32)
        s = jnp.einsum('bqhcd,bshcd->bhcqs', qb, k,
                       preferred_element_type=jnp.float32) * scale + bias[None, :, None]
        s = jnp.where((dist >= 0)[None, None, None], s, MASK_VALUE)
        p = jax.nn.softmax(s, axis=-1)
        a = p[:, :, 0] - lam_full * p[:, :, 1]
        return jnp.einsum('bhqs,bshe->bqhe', a.astype(dt), v)

    starts = jnp.arange(n_blk) * Q_BLOCK
    o = lax.map(block, (q_blocks, starts))
    o = o.swapaxes(0, 1).reshape(B, S, DA_HEADS, DA_DV)
    o = rms_norm(o, subln_g) * (1.0 - lambda_init)
    return o.reshape(B, S, DA_VWIDTH)


def rwkv7_branch(z, mu, w0, w_up, a0, a_up, g_up, k_k, k_a, r_k, ln_g, ln_b):
    B, S, _ = z.shape
    dt = z.dtype
    z_prev = jnp.pad(z[:, :-1], ((0, 0), (1, 0), (0, 0)))
    z = z + mu * (z_prev - z)
    r, k, v, wd, ad, gd = jnp.split(z, RW_SPLITS, axis=-1)
    w = (w0 + jnp.tanh(wd) @ w_up).astype(jnp.float32)
    decay = jnp.exp(-jnp.exp(-jax.nn.softplus(-w) - 0.5))
    a = jax.nn.sigmoid(a0 + ad @ a_up).astype(jnp.float32)
    g = jax.nn.sigmoid(gd) @ g_up
    heads = lambda t: t.astype(jnp.float32).reshape(B, S, RW_HEADS, RW_HEAD)
    kk = heads(k * k_k)
    kk = kk / jnp.maximum(jnp.sqrt(jnp.sum(kk * kk, axis=-1, keepdims=True)), 1e-12)
    k = heads(k) * (1.0 + (heads(a) - 1.0) * k_a.astype(jnp.float32).reshape(RW_HEADS, RW_HEAD))
    r, v, a, decay = heads(r), heads(v), heads(a), heads(decay)

    def step(state, inp):
        r_t, w_t, k_t, v_t, kk_t, a_t = inp
        sa = jnp.einsum('bhvk,bhk->bhv', state, -kk_t)
        state = (state * w_t[:, :, None, :]
                 + sa[..., None] * (kk_t * a_t)[:, :, None, :]
                 + v_t[..., None] * k_t[:, :, None, :])
        return state, jnp.einsum('bhvk,bhk->bhv', state, r_t)

    xs = tuple(t.swapaxes(0, 1) for t in (r, decay, k, v, kk, a))
    s0 = jnp.zeros((B, RW_HEADS, RW_HEAD, RW_HEAD), jnp.float32)
    _, y = lax.scan(step, s0, xs)
    y = y.swapaxes(0, 1)
    y = group_norm(y, ln_g.reshape(RW_HEADS, RW_HEAD), ln_b.reshape(RW_HEADS, RW_HEAD), RW_GN_EPS)
    bonus = jnp.sum(r * k * r_k.astype(jnp.float32), axis=-1, keepdims=True) * v
    y = (y + bonus).reshape(B, S, RW_WIDTH)
    return (y * g.astype(jnp.float32)).astype(dt)


def memory_cross_attention(h, mem_n, wq, wkv, wo):
    B, S, _ = h.shape
    M = mem_n.shape[1]
    q = (h @ wq).reshape(B, S, XA_HEADS, XA_HEAD)
    kv = (mem_n @ wkv).reshape(B, M, 2, XA_HEADS, XA_HEAD)
    s = jnp.einsum('bshd,bmhd->bhsm', q, kv[:, :, 0],
                   preferred_element_type=jnp.float32) * (XA_HEAD ** -0.5)
    p = jax.nn.softmax(s, axis=-1).astype(h.dtype)
    o = jnp.einsum('bhsm,bmhd->bshd', p, kv[:, :, 1]).reshape(B, S, D_MODEL)
    return o @ wo


def conv_gated_ffn(h, w_up, conv_w, conv_b, w_down):
    u, v = jnp.split(h @ w_up, 2, axis=-1)
    u = lax.conv_general_dilated(u, conv_w[:, None, :], window_strides=(1,),
                                 padding=[(CONV_W - 1, 0)],
                                 dimension_numbers=('NWC', 'WIO', 'NWC'),
                                 feature_group_count=D_FF) + conv_b
    return (jax.nn.silu(u) * v) @ w_down


def setup_inputs(seed: int = 0) -> dict:
    key = jax.random.key(seed)
    ks = iter(jax.random.split(key, 40))
    L = DEPTH

    def nrm(shape, scale):
        return scale * jax.random.normal(next(ks), shape, jnp.float32)

    def gain(shape):
        return 1.0 + 0.02 * jax.random.normal(next(ks), shape, jnp.float32)

    return {
        "x": nrm((BATCH, SEQ, D_MODEL), 1.0),
        "mem": nrm((BATCH, MEM_LEN, D_MODEL), 1.0),
        "norm_mix_g": gain((L, D_MODEL)),
        "w_in": nrm((L, D_MODEL, N_IN), D_MODEL ** -0.5),
        "b_gate": nrm((L, N_BRANCH * D_MODEL), 0.02),
        "hgrn_lb_param": nrm((L, HG_WIDTH), 1.0),
        "hgrn_norm_g": gain((L, HG_VWIDTH)),
        "diff_lambda": nrm((L, 4, DA_DK), 0.1),
        "diff_subln_g": gain((L, DA_DV)),
        "rwkv_mu": jax.random.uniform(next(ks), (L, RW_COLS), jnp.float32),
        "rwkv_w0": nrm((L, RW_WIDTH), 0.5),
        "rwkv_w_up": nrm((L, RW_DECAY_RANK, RW_WIDTH), RW_DECAY_RANK ** -0.5),
        "rwkv_a0": nrm((L, RW_WIDTH), 0.1),
        "rwkv_a_up": nrm((L, RW_A_RANK, RW_WIDTH), RW_A_RANK ** -0.5),
        "rwkv_g_up": nrm((L, RW_GATE_RANK, RW_WIDTH), RW_GATE_RANK ** -0.5),
        "rwkv_k_k": 0.85 + nrm((L, RW_WIDTH), 0.02),
        "rwkv_k_a": gain((L, RW_WIDTH)),
        "rwkv_r_k": nrm((L, RW_HEADS, RW_HEAD), 0.1),
        "rwkv_ln_g": gain((L, RW_WIDTH)),
        "rwkv_ln_b": nrm((L, RW_WIDTH), 0.02),
        "w_branch": nrm((L, N_BRANCH, BRANCH_WIDTH, D_MODEL), BRANCH_WIDTH ** -0.5),
        "w_out": nrm((L, D_MODEL, D_MODEL), D_MODEL ** -0.5),
        "norm_xa_g": gain((L, D_MODEL)),
        "norm_mem_g": gain((L, D_MODEL)),
        "xa_wq": nrm((L, D_MODEL, D_MODEL), D_MODEL ** -0.5),
        "xa_wkv": nrm((L, D_MODEL, 2 * D_MODEL), D_MODEL ** -0.5),
        "xa_wo": nrm((L, D_MODEL, D_MODEL), D_MODEL ** -0.5),
        "norm_ffn_g": gain((L, D_MODEL)),
        "ffn_w_up": nrm((L, D_MODEL, 2 * D_FF), D_MODEL ** -0.5),
        "ffn_conv_w": nrm((L, CONV_W, D_FF), CONV_W ** -0.5),
        "ffn_conv_b": nrm((L, D_FF), 0.02),
        "ffn_w_down": nrm((L, D_FF, D_MODEL), D_FF ** -0.5),
        "final_norm_g": gain((D_MODEL,)),
    }


def reference(x, mem, norm_mix_g, w_in, b_gate, hgrn_lb_param, hgrn_norm_g, diff_lambda,
              diff_subln_g, rwkv_mu, rwkv_w0, rwkv_w_up, rwkv_a0, rwkv_a_up, rwkv_g_up,
              rwkv_k_k, rwkv_k_a, rwkv_r_k, rwkv_ln_g, rwkv_ln_b, w_branch, w_out,
              norm_xa_g, norm_mem_g, xa_wq, xa_wkv, xa_wo, norm_ffn_g, ffn_w_up,
              ffn_conv_w, ffn_conv_b, ffn_w_down, final_norm_g):
    B, S, _ = x.shape
    lb_p = jax.nn.softmax(hgrn_lb_param.astype(jnp.float32), axis=0)
    lower_bounds = jnp.cumsum(lb_p, axis=0) - lb_p[0]
    for l in range(DEPTH):
        h = rms_norm(x, norm_mix_g[l])
        proj = h @ w_in[l]
        hq, hf, hi, hg, dq, dk, dv, rw, g_pre = jnp.split(proj, IN_SPLITS, axis=-1)
        o_hg = hgrn2_branch(hq, hf, hi, hg, lower_bounds[l], hgrn_norm_g[l])
        lambda_init = 0.8 - 0.6 * math.exp(-0.3 * l)
        o_da = diff_attention(dq, dk, dv, diff_lambda[l], diff_subln_g[l], lambda_init)
        o_rw = rwkv7_branch(rw, rwkv_mu[l], rwkv_w0[l], rwkv_w_up[l], rwkv_a0[l], rwkv_a_up[l],
                            rwkv_g_up[l], rwkv_k_k[l], rwkv_k_a[l], rwkv_r_k[l],
                            rwkv_ln_g[l], rwkv_ln_b[l])
        branches = jnp.stack([o_hg, o_da, o_rw], axis=2)
        gate = jax.nn.sigmoid((g_pre + b_gate[l]).astype(jnp.float32)).astype(x.dtype)
        gate = gate.reshape(B, S, N_BRANCH, D_MODEL)
        merged = jnp.sum(gate * jnp.einsum('bsnc,ncd->bsnd', branches, w_branch[l]), axis=2)
        x = x + merged @ w_out[l]
        h = rms_norm(x, norm_xa_g[l])
        x = x + memory_cross_attention(h, rms_norm(mem, norm_mem_g[l]), xa_wq[l], xa_wkv[l], xa_wo[l])
        h = rms_norm(x, norm_ffn_g[l])
        x = x + conv_gated_ffn(h, ffn_w_up[l], ffn_conv_w[l], ffn_conv_b[l], ffn_w_down[l])
    return rms_norm(x, final_norm_g)
```

```python
import functools
import math

import jax
import jax.numpy as jnp
from jax import lax
from jax.experimental import pallas as pl
from jax.experimental.pallas import tpu as pltpu

F32 = jnp.float32
BF16 = jnp.bfloat16

D_MODEL = 1024
NORM_EPS = 1e-6
MASK_VALUE = -1e30
TINY = 1e-30

HG_HEADS = 4
HG_D = 128
HG_WIDTH = HG_HEADS * HG_D
HG_SUB = 16

DA_HEADS = 4
DA_DK = 64
DA_DV = 128
DA_WIDTH = DA_HEADS * DA_DV
ALIBI_MAX_BIAS = 8.0

RW_HEADS = 8
RW_HEAD = 64
RW_WIDTH = RW_HEADS * RW_HEAD
RW_DECAY_RANK = 64
RW_A_RANK = 64
RW_GATE_RANK = 128
RW_COLS = 3 * RW_WIDTH + RW_DECAY_RANK + RW_A_RANK + RW_GATE_RANK
RW_GN_EPS = 64e-5

N_BRANCH = 3
BRANCH_WIDTH = 512

XA_HEADS = 4
XA_HEAD = D_MODEL // XA_HEADS

D_FF = 2816
CONV_W = 3

RW_PAD = 2048
P_OFF_RW = 0
P_OFF_HG = RW_PAD
P_OFF_GATE = P_OFF_HG + 4 * HG_WIDTH
P_OFF_DA = P_OFF_GATE + N_BRANCH * D_MODEL
N_PROJ = P_OFF_DA + 3 * DA_WIDTH

R_OFF_DA = 4 * HG_WIDTH
R_OFF_RW = R_OFF_DA + 3 * DA_WIDTH
R_OFF_GATE = R_OFF_RW + RW_COLS

V7X_VMEM_LIMIT = 56 * 1024 * 1024
SUBLANES = 8


def _cparams(sem):
    return pltpu.CompilerParams(dimension_semantics=sem, vmem_limit_bytes=V7X_VMEM_LIMIT)


def _pick(n, pref):
    t = min(n, pref)
    while n % t:
        t //= 2
    return t


def _rms(x, g):
    ms = jnp.mean(x * x, axis=-1, keepdims=True)
    return x * lax.rsqrt(ms + NORM_EPS) * g


def _split3(x):
    hi = x.astype(BF16)
    r1 = x - hi.astype(F32)
    mid = r1.astype(BF16)
    lo = (r1 - mid.astype(F32)).astype(BF16)
    return hi, mid, lo


def _gdot(x, g, passes=3):
    parts = _split3(x)[:passes]
    acc = jnp.dot(parts[0], g, preferred_element_type=F32)
    for p in parts[1:]:
        acc = acc + jnp.dot(p, g, preferred_element_type=F32)
    return acc


def _silu(x):
    return x * jax.nn.sigmoid(x)


def _norm_mm_kernel(x_ref, g_ref, w_ref, o_ref, h_ref):
    @pl.when(pl.program_id(1) == 0)
    def _():
        h_ref[...] = _rms(x_ref[...], g_ref[...]).astype(BF16)

    o_ref[...] = jnp.dot(h_ref[...], w_ref[...], preferred_element_type=F32)


def _norm_matmul(x, g, w, tm, tn):
    t, d = x.shape
    n = w.shape[1]
    tm = _pick(t, tm)
    tn = _pick(n, tn)
    return pl.pallas_call(
        _norm_mm_kernel,
        grid=(t // tm, n // tn),
        in_specs=[
            pl.BlockSpec((tm, d), lambda i, j: (i, 0)),
            pl.BlockSpec((1, d), lambda i, j: (0, 0)),
            pl.BlockSpec((d, tn), lambda i, j: (0, j)),
        ],
        out_specs=pl.BlockSpec((tm, tn), lambda i, j: (i, j)),
        out_shape=jax.ShapeDtypeStruct((t, n), F32),
        scratch_shapes=[pltpu.VMEM((tm, d), BF16)],
        compiler_params=_cparams(("parallel", "arbitrary")),
    )(x, g.reshape(1, d), w)


def _hgrn_kernel(q_ref, f_ref, i_ref, g_ref, lb_ref, ng_ref, o_ref,
                 st_ref, qs_s, b_s, k_s, o_s, *, ts):
    @pl.when(pl.program_id(2) == 0)
    def _():
        st_ref[...] = jnp.zeros_like(st_ref)

    lb = lb_ref[...]
    fp = f_ref[...]
    f = lb + (1.0 - lb) * jax.nn.sigmoid(fp)
    glog = jnp.log(jnp.maximum(f, TINY))
    k_s[...] = (1.0 - lb) * jax.nn.sigmoid(-fp)
    qs_s[...] = _silu(q_ref[...])

    r16 = lax.broadcasted_iota(jnp.int32, (ts, HG_D), 0) % HG_SUB
    b = glog
    sh = 1
    while sh < HG_SUB:
        b = b + jnp.where(r16 >= sh, pltpu.roll(b, sh, axis=0), 0.0)
        sh *= 2
    b_s[...] = b

    rowi = lax.broadcasted_iota(jnp.int32, (HG_SUB, HG_D), 0)

    def body(ci, carry):
        base = pl.multiple_of(ci * HG_SUB, HG_SUB)
        qc = qs_s[pl.ds(base, HG_SUB), :]
        bc = b_s[pl.ds(base, HG_SUB), :]
        kc = k_s[pl.ds(base, HG_SUB), :]
        vc = i_ref[pl.ds(base, HG_SUB), :]
        st = st_ref[...]
        qe = qc * jnp.exp(bc)
        o = lax.dot_general(qe.astype(BF16), st.astype(BF16), (((1,), (1,)), ((), ())),
                            preferred_element_type=F32)
        for j in range(HG_SUB):
            bj = b_s[pl.ds(base + j, 1), :]
            kj = k_s[pl.ds(base + j, 1), :]
            vj = i_ref[pl.ds(base + j, 1), :]
            dec = jnp.exp(jnp.where(rowi >= j, bc - bj, MASK_VALUE))
            sc = jnp.sum(qc * dec * kj, axis=-1, keepdims=True)
            o = o + sc * vj
        o_s[pl.ds(base, HG_SUB), :] = o
        bl = b_s[pl.ds(base + HG_SUB - 1, 1), :]
        kd = kc * jnp.exp(bl - bc)
        upd = lax.dot_general(vc.astype(BF16), kd.astype(BF16), (((0,), (0,)), ((), ())),
                              preferred_element_type=F32)
        st_ref[...] = st * jnp.exp(bl) + upd
        return carry

    lax.fori_loop(0, ts // HG_SUB, body, 0)

    o = _rms(o_s[...], ng_ref[...])
    o_ref[...] = o * _silu(g_ref[...])


def _hgrn2(proj, lb, norm_g, b, s):
    ts = _pick(s, 256)
    nt = s // ts
    c0 = P_OFF_HG // HG_D

    def col(off):
        return pl.BlockSpec((ts, HG_D), lambda bi, h, c, off=off: (bi * nt + c, c0 + off + h))

    par = pl.BlockSpec((1, HG_D), lambda bi, h, c: (0, h))
    return pl.pallas_call(
        functools.partial(_hgrn_kernel, ts=ts),
        grid=(b, HG_HEADS, nt),
        in_specs=[col(0), col(HG_HEADS), col(2 * HG_HEADS), col(3 * HG_HEADS), par, par],
        out_specs=pl.BlockSpec((ts, HG_D), lambda bi, h, c: (bi * nt + c, h)),
        out_shape=jax.ShapeDtypeStruct((b * s, HG_WIDTH), F32),
        scratch_shapes=[pltpu.VMEM((HG_D, HG_D), F32)] + [pltpu.VMEM((ts, HG_D), F32)] * 4,
        compiler_params=_cparams(("parallel", "parallel", "arbitrary")),
    )(proj, proj, proj, proj, lb.reshape(1, HG_WIDTH), norm_g.reshape(1, HG_WIDTH))


def _da_kernel(q_ref, k_ref, v_ref, sl_ref, lam_ref, sg_ref, o_ref,
               q2_s, m_s, l_s, acc_s, *, tq, lambda_init):
    i = pl.program_id(2)
    q = q_ref[...] * (DA_DK ** -0.5)
    lane = lax.broadcasted_iota(jnp.int32, (tq, DA_DV), 1)
    q2_s[pl.ds(0, tq), :] = jnp.where(lane < DA_DK, q, 0.0).astype(BF16)
    q2_s[pl.ds(tq, tq), :] = jnp.where(lane >= DA_DK, q, 0.0).astype(BF16)
    m_s[...] = jnp.full_like(m_s, MASK_VALUE)
    l_s[...] = jnp.zeros_like(l_s)
    acc_s[...] = jnp.zeros_like(acc_s)

    slope = sl_ref[0]
    colf = lax.broadcasted_iota(jnp.int32, (1, tq), 1).astype(F32)

    def step(j, masked):
        start = pl.multiple_of(j * tq, tq)
        kb = k_ref[pl.ds(start, tq), :].astype(BF16)
        vb = v_ref[pl.ds(start, tq), :].astype(BF16)
        s = lax.dot_general(q2_s[...], kb, (((1,), (1,)), ((), ())),
                            preferred_element_type=F32)
        s = s + slope * (colf + ((j - i) * tq).astype(F32))
        if masked:
            r = lax.broadcasted_iota(jnp.int32, (2 * tq, tq), 0) % tq
            c = lax.broadcasted_iota(jnp.int32, (2 * tq, tq), 1)
            s = jnp.where(c <= r, s, MASK_VALUE)
        m_old = m_s[...]
        m_new = jnp.maximum(m_old, jnp.max(s, axis=-1, keepdims=True))
        alpha = jnp.exp(m_old - m_new)
        p = jnp.exp(s - m_new)
        l_s[...] = alpha * l_s[...] + jnp.sum(p, axis=-1, keepdims=True)
        acc_s[...] = alpha * acc_s[...] + jnp.dot(p.astype(BF16), vb,
                                                  preferred_element_type=F32)
        m_s[...] = m_new

    def body(j, carry):
        step(j, False)
        return carry

    lax.fori_loop(0, i, body, 0)
    step(i, True)

    lam = lam_ref[0]
    e1 = jnp.exp(jnp.sum(lam[0:1] * lam[1:2], axis=-1, keepdims=True))
    e2 = jnp.exp(jnp.sum(lam[2:3] * lam[3:4], axis=-1, keepdims=True))
    lam_full = e1 - e2 + lambda_init
    o1 = acc_s[pl.ds(0, tq), :] / l_s[pl.ds(0, tq), :]
    o2 = acc_s[pl.ds(tq, tq), :] / l_s[pl.ds(tq, tq), :]
    o = o1 - lam_full * o2
    o_ref[...] = _rms(o, sg_ref[...]) * (1.0 - lambda_init)


def _diff_attention(proj3, lam, subln_g, lambda_init, b, s):
    tq = _pick(s, 256)
    nq = s // tq
    c0 = P_OFF_DA // DA_DV
    slopes = 2.0 ** (-ALIBI_MAX_BIAS * jnp.arange(1, DA_HEADS + 1, dtype=F32) / DA_HEADS)
    slopes = jnp.broadcast_to(slopes[:, None, None], (DA_HEADS, 1, tq))
    return pl.pallas_call(
        functools.partial(_da_kernel, tq=tq, lambda_init=lambda_init),
        grid=(b, DA_HEADS, nq),
        in_specs=[
            pl.BlockSpec((None, tq, DA_DV), lambda bi, h, i: (bi, i, c0 + h)),
            pl.BlockSpec((None, s, DA_DV), lambda bi, h, i: (bi, 0, c0 + DA_HEADS + h)),
            pl.BlockSpec((None, s, DA_DV), lambda bi, h, i: (bi, 0, c0 + 2 * DA_HEADS + h)),
            pl.BlockSpec((1, 1, tq), lambda bi, h, i: (h, 0, 0)),
            pl.BlockSpec((1, 4, DA_DK), lambda bi, h, i: (0, 0, 0)),
            pl.BlockSpec((1, DA_DV), lambda bi, h, i: (0, 0)),
        ],
        out_specs=pl.BlockSpec((None, tq, DA_DV), lambda bi, h, i: (bi, i, h)),
        out_shape=jax.ShapeDtypeStruct((b, s, DA_WIDTH), F32),
        scratch_shapes=[
            pltpu.VMEM((2 * tq, DA_DV), BF16),
            pltpu.VMEM((2 * tq, 1), F32),
            pltpu.VMEM((2 * tq, 1), F32),
            pltpu.VMEM((2 * tq, DA_DV), F32),
        ],
        compiler_params=_cparams(("parallel", "parallel", "arbitrary")),
    )(proj3, proj3, proj3, slopes, lam.reshape(1, 4, DA_DK), subln_g.reshape(1, DA_DV))


def _rw_prep_kernel(z_ref, zp_ref, mu_ref, wa_ref, w0_ref, a0_ref, gup_ref, kk_ref, ka_ref,
                    rk_ref, hs_ref,
                    nkk_o, w_o, bb_o, k_o, wr_o, v_o, c1_o, c2_o, c3_o, g_o, *, tiles_per_seq):
    tm = z_ref.shape[0]
    z = z_ref[...]
    first = (pl.program_id(0) % tiles_per_seq) == 0
    prev = jnp.where(first, 0.0, zp_ref[SUBLANES - 1:SUBLANES, :])
    row = lax.broadcasted_iota(jnp.int32, z.shape, 0)
    zs = jnp.where(row == 0, prev, pltpu.roll(z, 1, axis=0))
    z = z + mu_ref[...] * (zs - z)

    w3 = RW_WIDTH
    r = z[:, 0:w3]
    k = z[:, w3:2 * w3]
    v = z[:, 2 * w3:3 * w3]
    wd_ad = z[:, 3 * w3:3 * w3 + 128]
    gd = z[:, 3 * w3 + 128:3 * w3 + 256]
    lane = lax.broadcasted_iota(jnp.int32, wd_ad.shape, 1)
    lowrank_in = jnp.where(lane < RW_DECAY_RANK, jnp.tanh(wd_ad), wd_ad)
    wa = jnp.dot(lowrank_in.astype(BF16), wa_ref[...], preferred_element_type=F32)
    w = w0_ref[...] + wa[:, 0:w3]
    nw = -w
    softplus = jnp.maximum(nw, 0.0) + jnp.log(1.0 + jnp.exp(-jnp.abs(nw)))
    decay = jnp.exp(-jnp.exp(-softplus - 0.5))
    a = jax.nn.sigmoid(a0_ref[...] + wa[:, w3:2 * w3])
    g = jnp.dot(jax.nn.sigmoid(gd).astype(BF16), gup_ref[...], preferred_element_type=F32)

    hs = hs_ref[...]
    kk = k * kk_ref[...]
    kk = kk / jnp.maximum(jnp.sqrt(_gdot(kk * kk, hs)), 1e-12)
    kmod = k * (1.0 + (a - 1.0) * ka_ref[...])
    bb = kk * a

    nkk_o[...] = -kk
    w_o[...] = decay
    bb_o[...] = bb
    k_o[...] = kmod
    wr_o[...] = decay * r
    v_o[...] = v
    c1_o[...] = _gdot(bb * r, hs)
    c2_o[...] = _gdot(kmod * r, hs)
    c3_o[...] = _gdot(r * kmod * rk_ref[...], hs)
    g_o[...] = g


def _head_sum_matrix(n, head):
    idx = jnp.arange(n) // head
    return (idx[:, None] == idx[None, :]).astype(BF16)


def _rwkv_prep(proj, p, b, s):
    t = b * s
    tm = _pick(s, 256)
    tiles_per_seq = s // tm
    w3 = RW_WIDTH
    wa = jnp.zeros((128, 2 * w3), F32)
    wa = wa.at[0:RW_DECAY_RANK, 0:w3].set(p["w_up"]).at[RW_DECAY_RANK:128, w3:].set(p["a_up"])
    mu = jnp.pad(p["mu"], (0, RW_PAD - RW_COLS)).reshape(1, RW_PAD)
    row = lambda a: a.reshape(1, w3)
    const = lambda shape: pl.BlockSpec(shape, lambda i: (0, 0))
    out_spec = pl.BlockSpec((tm, w3), lambda i: (i, 0))
    nb8 = tm // SUBLANES
    return pl.pallas_call(
        functools.partial(_rw_prep_kernel, tiles_per_seq=tiles_per_seq),
        grid=(t // tm,),
        in_specs=[
            pl.BlockSpec((tm, RW_PAD), lambda i: (i, 0)),
            pl.BlockSpec((SUBLANES, RW_PAD), lambda i: (jnp.maximum(i * nb8 - 1, 0), 0)),
            const((1, RW_PAD)), const((128, 2 * w3)), const((1, w3)), const((1, w3)),
            const((RW_GATE_RANK, w3)), const((1, w3)), const((1, w3)), const((1, w3)),
            const((w3, w3)),
        ],
        out_specs=[out_spec] * 10,
        out_shape=[jax.ShapeDtypeStruct((t, w3), F32)] * 10,
        compiler_params=_cparams(("parallel",)),
    )(proj, proj, mu, wa.astype(BF16), row(p["w0"]), row(p["a0"]), p["g_up"].astype(BF16),
      row(p["k_k"]), row(p["k_a"]), row(p["r_k"]), _head_sum_matrix(w3, RW_HEAD))


def _rw_scan_kernel(nkk_ref, w_ref, bb_ref, k_ref, wr_ref, v_ref, c1_ref, c2_ref, hs_ref,
                    y_ref, st_ref, e_nkk, e_w, e_bb, e_k, e_wr, *, tt):
    @pl.when(pl.program_id(1) == 0)
    def _():
        st_ref[...] = jnp.zeros_like(st_ref)

    hs = hs_ref[...]
    sub = lax.broadcasted_iota(jnp.int32, (RW_HEAD, RW_WIDTH), 0)
    lane = lax.broadcasted_iota(jnp.int32, (RW_HEAD, RW_WIDTH), 1)
    diag = (lane % RW_HEAD) == sub

    def expand(src, dst):
        for t in range(tt):
            x = src[0, pl.ds(t, 1), :]
            acc = None
            for part in _split3(x):
                lhs = jnp.where(diag, jnp.broadcast_to(part.astype(F32), diag.shape), 0.0)
                d = jnp.dot(lhs.astype(BF16), hs, preferred_element_type=F32)
                acc = d if acc is None else acc + d
            dst[t] = acc

    expand(nkk_ref, e_nkk)
    expand(w_ref, e_w)
    expand(bb_ref, e_bb)
    expand(k_ref, e_k)
    expand(wr_ref, e_wr)

    def body(t, carry):
        st = st_ref[...]
        sa = jnp.sum(st * e_nkk[t], axis=0, keepdims=True)
        yw = jnp.sum(st * e_wr[t], axis=0, keepdims=True)
        v = v_ref[0, pl.ds(t, 1), :]
        y_ref[0, pl.ds(t, 1), :] = (yw + c1_ref[0, pl.ds(t, 1), :] * sa
                                    + c2_ref[0, pl.ds(t, 1), :] * v)
        st_ref[...] = st * e_w[t] + e_bb[t] * sa + e_k[t] * v
        return carry

    lax.fori_loop(0, tt, body, 0)


def _rwkv_scan(nkk, w, bb, k, wr, v, c1, c2, b, s):
    tt = _pick(s, 16)
    w3 = RW_WIDTH
    r3 = lambda a: a.reshape(b, s, w3)
    spec = pl.BlockSpec((1, tt, w3), lambda bi, i: (bi, i, 0))
    ex = pltpu.VMEM((tt, RW_HEAD, w3), F32)
    return pl.pallas_call(
        functools.partial(_rw_scan_kernel, tt=tt),
        grid=(b, s // tt),
        in_specs=[spec] * 8 + [pl.BlockSpec((w3, w3), lambda bi, i: (0, 0))],
        out_specs=spec,
        out_shape=jax.ShapeDtypeStruct((b, s, w3), F32),
        scratch_shapes=[pltpu.VMEM((RW_HEAD, w3), F32), ex, ex, ex, ex, ex],
        compiler_params=_cparams(("parallel", "arbitrary")),
    )(r3(nkk), r3(w), r3(bb), r3(k), r3(wr), r3(v), r3(c1), r3(c2),
      _head_sum_matrix(w3, RW_HEAD)).reshape(b * s, w3)


def _merge_kernel(x_ref, hg_ref, da_ref, y_ref, v_ref, c3_ref, g_ref, g0_ref, g1_ref, g2_ref,
                  bg_ref, lng_ref, lnb_ref, hs_ref, wb_ref, wo_ref, o_ref):
    hs = hs_ref[...]
    y = y_ref[...]
    mu = _gdot(y, hs) * (1.0 / RW_HEAD)
    yc = y - mu
    var = _gdot(yc * yc, hs) * (1.0 / RW_HEAD)
    yn = yc * lax.rsqrt(var + RW_GN_EPS) * lng_ref[...] + lnb_ref[...]
    o_rw = (yn + c3_ref[...] * v_ref[...]) * g_ref[...]

    merged = None
    for n, (br, gp) in enumerate(((hg_ref[...], g0_ref), (da_ref[...], g1_ref), (o_rw, g2_ref))):
        gate = jax.nn.sigmoid(gp[...] + bg_ref[:, n * D_MODEL:(n + 1) * D_MODEL])
        pb = jnp.dot(br.astype(BF16), wb_ref[n], preferred_element_type=F32)
        merged = gate * pb if merged is None else merged + gate * pb
    o_ref[...] = x_ref[...] + jnp.dot(merged.astype(BF16), wo_ref[...],
                                      preferred_element_type=F32)


def _merge(x, o_hg, o_da, y, v, c3, g, proj, b_gate, ln_g, ln_b, w_branch, w_out):
    t = x.shape[0]
    tm = _pick(t, 256)
    bw = BRANCH_WIDTH
    row = pl.BlockSpec((tm, bw), lambda i: (i, 0))
    g0 = P_OFF_GATE // D_MODEL
    gate_spec = lambda n: pl.BlockSpec((tm, D_MODEL), lambda i, n=n: (i, g0 + n))
    const = lambda shape: pl.BlockSpec(shape, lambda i: (0,) * len(shape))
    return pl.pallas_call(
        _merge_kernel,
        grid=(t // tm,),
        in_specs=[pl.BlockSpec((tm, D_MODEL), lambda i: (i, 0))] + [row] * 6
        + [gate_spec(0), gate_spec(1), gate_spec(2),
           const((1, N_BRANCH * D_MODEL)), const((1, bw)), const((1, bw)), const((bw, bw)),
           const((N_BRANCH, bw, D_MODEL)), const((D_MODEL, D_MODEL))],
        out_specs=pl.BlockSpec((tm, D_MODEL), lambda i: (i, 0)),
        out_shape=jax.ShapeDtypeStruct((t, D_MODEL), F32),
        compiler_params=_cparams(("parallel",)),
    )(x, o_hg, o_da, y, v, c3, g, proj, proj, proj, b_gate.reshape(1, -1),
      ln_g.reshape(1, bw), ln_b.reshape(1, bw), _head_sum_matrix(bw, RW_HEAD),
      w_branch.astype(BF16), w_out.astype(BF16))


def _xattn_kernel(x_ref, g_ref, wq_ref, kv_ref, wo_ref, o_ref):
    x = x_ref[...]
    h = _rms(x, g_ref[...]).astype(BF16)
    q = jnp.dot(h, wq_ref[...], preferred_element_type=F32) * (XA_HEAD ** -0.5)
    outs = []
    for hd in range(XA_HEADS):
        qh = q[:, hd * XA_HEAD:(hd + 1) * XA_HEAD].astype(BF16)
        kh = kv_ref[0, :, hd * XA_HEAD:(hd + 1) * XA_HEAD].astype(BF16)
        vh = kv_ref[0, :, D_MODEL + hd * XA_HEAD:D_MODEL + (hd + 1) * XA_HEAD].astype(BF16)
        s = lax.dot_general(qh, kh, (((1,), (1,)), ((), ())), preferred_element_type=F32)
        m = jnp.max(s, axis=-1, keepdims=True)
        p = jnp.exp(s - m)
        p = p / jnp.sum(p, axis=-1, keepdims=True)
        outs.append(jnp.dot(p.astype(BF16), vh, preferred_element_type=F32))
    o = jnp.concatenate(outs, axis=-1).astype(BF16)
    o_ref[...] = x + jnp.dot(o, wo_ref[...], preferred_element_type=F32)


def _cross_attention(x, g, wq, kv, wo, b, s):
    tm = _pick(s, 256)
    nt = s // tm
    m = kv.shape[1]
    return pl.pallas_call(
        _xattn_kernel,
        grid=(b, nt),
        in_specs=[
            pl.BlockSpec((tm, D_MODEL), lambda bi, i: (bi * nt + i, 0)),
            pl.BlockSpec((1, D_MODEL), lambda bi, i: (0, 0)),
            pl.BlockSpec((D_MODEL, D_MODEL), lambda bi, i: (0, 0)),
            pl.BlockSpec((1, m, 2 * D_MODEL), lambda bi, i: (bi, 0, 0)),
            pl.BlockSpec((D_MODEL, D_MODEL), lambda bi, i: (0, 0)),
        ],
        out_specs=pl.BlockSpec((tm, D_MODEL), lambda bi, i: (bi * nt + i, 0)),
        out_shape=jax.ShapeDtypeStruct((b * s, D_MODEL), F32),
        compiler_params=_cparams(("parallel", "parallel")),
    )(x, g.reshape(1, D_MODEL), wq.astype(BF16), kv, wo.astype(BF16))


def _ffn_kernel(x_ref, xp_ref, g_ref, wu_ref, wv_ref, cw_ref, cb_ref, wd_ref, o_ref,
                h_s, hp_s, acc_s, *, tiles_per_seq):
    j = pl.program_id(1)
    tm = x_ref.shape[0]

    @pl.when(j == 0)
    def _():
        h_s[...] = _rms(x_ref[...], g_ref[...]).astype(BF16)
        first = (pl.program_id(0) % tiles_per_seq) == 0
        hp = _rms(xp_ref[...], g_ref[...])
        hp_s[...] = jnp.where(first, 0.0, hp).astype(BF16)
        acc_s[...] = jnp.zeros_like(acc_s)

    wu = wu_ref[...]
    u = jnp.dot(h_s[...], wu, preferred_element_type=F32)
    up = jnp.dot(hp_s[...], wu, preferred_element_type=F32)
    vv = jnp.dot(h_s[...], wv_ref[...], preferred_element_type=F32)
    row = lax.broadcasted_iota(jnp.int32, u.shape, 0)
    p1 = up[SUBLANES - 1:SUBLANES, :]
    p2 = up[SUBLANES - 2:SUBLANES - 1, :]
    u1 = jnp.where(row == 0, p1, pltpu.roll(u, 1, axis=0))
    u2 = jnp.where(row == 0, p2, jnp.where(row == 1, p1, pltpu.roll(u, 2, axis=0)))
    cw = cw_ref[...]
    uc = cw[0:1, :] * u2 + cw[1:2, :] * u1 + cw[2:3, :] * u + cb_ref[...]
    act = (_silu(uc) * vv).astype(BF16)
    acc_s[...] += jnp.dot(act, wd_ref[...], preferred_element_type=F32)

    @pl.when(j == pl.num_programs(1) - 1)
    def _():
        o_ref[...] = x_ref[...] + acc_s[...]


def _ffn(x, g, w_up, conv_w, conv_b, w_down, b, s):
    t = x.shape[0]
    tm = _pick(s, 512)
    tiles_per_seq = s // tm
    tf = 256
    nf = D_FF // tf
    nb8 = tm // SUBLANES
    wu = w_up.astype(BF16)
    return pl.pallas_call(
        functools.partial(_ffn_kernel, tiles_per_seq=tiles_per_seq),
        grid=(t // tm, nf),
        in_specs=[
            pl.BlockSpec((tm, D_MODEL), lambda i, j: (i, 0)),
            pl.BlockSpec((SUBLANES, D_MODEL), lambda i, j: (jnp.maximum(i * nb8 - 1, 0), 0)),
            pl.BlockSpec((1, D_MODEL), lambda i, j: (0, 0)),
            pl.BlockSpec((D_MODEL, tf), lambda i, j: (0, j)),
            pl.BlockSpec((D_MODEL, tf), lambda i, j: (0, nf + j)),
            pl.BlockSpec((CONV_W, tf), lambda i, j: (0, j)),
            pl.BlockSpec((1, tf), lambda i, j: (0, j)),
            pl.BlockSpec((tf, D_MODEL), lambda i, j: (j, 0)),
        ],
        out_specs=pl.BlockSpec((tm, D_MODEL), lambda i, j: (i, 0)),
        out_shape=jax.ShapeDtypeStruct((t, D_MODEL), F32),
        scratch_shapes=[pltpu.VMEM((tm, D_MODEL), BF16), pltpu.VMEM((SUBLANES, D_MODEL), BF16),
                        pltpu.VMEM((tm, D_MODEL), F32)],
        compiler_params=_cparams(("parallel", "arbitrary")),
    )(x, x, g.reshape(1, D_MODEL), wu, wu, conv_w, conv_b.reshape(1, D_FF),
      w_down.astype(BF16))


def _norm_kernel(x_ref, g_ref, o_ref):
    o_ref[...] = _rms(x_ref[...], g_ref[...])


def _final_norm(x, g):
    t, d = x.shape
    tm = _pick(t, 512)
    return pl.pallas_call(
        _norm_kernel,
        grid=(t // tm,),
        in_specs=[pl.BlockSpec((tm, d), lambda i: (i, 0)), pl.BlockSpec((1, d), lambda i: (0, 0))],
        out_specs=pl.BlockSpec((tm, d), lambda i: (i, 0)),
        out_shape=jax.ShapeDtypeStruct((t, d), F32),
        compiler_params=_cparams(("parallel",)),
    )(x, g.reshape(1, d))


def _permute_w_in(w):
    d = w.shape[0]
    return jnp.concatenate([
        w[:, R_OFF_RW:R_OFF_GATE], jnp.zeros((d, RW_PAD - RW_COLS), w.dtype),
        w[:, 0:R_OFF_DA], w[:, R_OFF_GATE:], w[:, R_OFF_DA:R_OFF_RW]], axis=1).astype(BF16)


def kernel(x, mem, norm_mix_g, w_in, b_gate, hgrn_lb_param, hgrn_norm_g, diff_lambda, diff_subln_g, rwkv_mu, rwkv_w0, rwkv_w_up, rwkv_a0, rwkv_a_up, rwkv_g_up, rwkv_k_k, rwkv_k_a, rwkv_r_k, rwkv_ln_g, rwkv_ln_b, w_branch, w_out, norm_xa_g, norm_mem_g, xa_wq, xa_wkv, xa_wo, norm_ffn_g, ffn_w_up, ffn_conv_w, ffn_conv_b, ffn_w_down, final_norm_g):
    b, s, d = x.shape
    depth = w_in.shape[0]
    m = mem.shape[1]
    t = b * s
    lb_p = jax.nn.softmax(hgrn_lb_param.astype(F32), axis=0)
    lower_bounds = jnp.cumsum(lb_p, axis=0) - lb_p[0]

    xf = x.reshape(t, d)
    memf = mem.reshape(b * m, d)
    for l in range(depth):
        proj = _norm_matmul(xf, norm_mix_g[l], _permute_w_in(w_in[l]), 512, 512)
        o_hg = _hgrn2(proj, lower_bounds[l], hgrn_norm_g[l], b, s)
        lambda_init = 0.8 - 0.6 * math.exp(-0.3 * l)
        o_da = _diff_attention(proj.reshape(b, s, N_PROJ), diff_lambda[l], diff_subln_g[l],
                               lambda_init, b, s).reshape(t, DA_WIDTH)
        rw_params = dict(mu=rwkv_mu[l], w0=rwkv_w0[l], w_up=rwkv_w_up[l], a0=rwkv_a0[l],
                         a_up=rwkv_a_up[l], g_up=rwkv_g_up[l], k_k=rwkv_k_k[l], k_a=rwkv_k_a[l],
                         r_k=rwkv_r_k[l].reshape(-1))
        nkk, w, bb, k, wr, v, c1, c2, c3, g = _rwkv_prep(proj, rw_params, b, s)
        y = _rwkv_scan(nkk, w, bb, k, wr, v, c1, c2, b, s)
        xf = _merge(xf, o_hg, o_da, y, v, c3, g, proj, b_gate[l], rwkv_ln_g[l], rwkv_ln_b[l],
                    w_branch[l], w_out[l])
        kv = _norm_matmul(memf, norm_mem_g[l], xa_wkv[l].astype(BF16), 256, 512)
        xf = _cross_attention(xf, norm_xa_g[l], xa_wq[l], kv.reshape(b, m, 2 * d), xa_wo[l], b, s)
        xf = _ffn(xf, norm_ffn_g[l], ffn_w_up[l], ffn_conv_w[l], ffn_conv_b[l], ffn_w_down[l], b, s)
    return _final_norm(xf, final_norm_g).reshape(b, s, d)
```

```python
import functools
import math

import jax
import jax.numpy as jnp
from jax import lax
from jax.experimental import pallas as pl
from jax.experimental.pallas import tpu as pltpu

F32 = jnp.float32
BF16 = jnp.bfloat16

D_MODEL = 1024
NORM_EPS = 1e-6
MASK_VALUE = -1e30
TINY = 1e-30

HG_HEADS = 4
HG_D = 128
HG_WIDTH = HG_HEADS * HG_D
HG_SUB = 16

DA_HEADS = 4
DA_DK = 64
DA_DV = 128
DA_WIDTH = DA_HEADS * DA_DV
ALIBI_MAX_BIAS = 8.0

RW_HEADS = 8
RW_HEAD = 64
RW_WIDTH = RW_HEADS * RW_HEAD
RW_DECAY_RANK = 64
RW_A_RANK = 64
RW_GATE_RANK = 128
RW_COLS = 3 * RW_WIDTH + RW_DECAY_RANK + RW_A_RANK + RW_GATE_RANK
RW_GN_EPS = 64e-5

N_BRANCH = 3
BRANCH_WIDTH = 512

XA_HEADS = 4
XA_HEAD = D_MODEL // XA_HEADS

D_FF = 2816
CONV_W = 3

RW_PAD = 2048
P_OFF_RW = 0
P_OFF_HG = RW_PAD
P_OFF_GATE = P_OFF_HG + 4 * HG_WIDTH
P_OFF_DA = P_OFF_GATE + N_BRANCH * D_MODEL
N_PROJ = P_OFF_DA + 3 * DA_WIDTH

R_OFF_DA = 4 * HG_WIDTH
R_OFF_RW = R_OFF_DA + 3 * DA_WIDTH
R_OFF_GATE = R_OFF_RW + RW_COLS

V7X_VMEM_LIMIT = 56 * 1024 * 1024
SUBLANES = 8


def _cparams(sem):
    return pltpu.CompilerParams(dimension_semantics=sem, vmem_limit_bytes=V7X_VMEM_LIMIT)


def _pick(n, pref):
    t = min(n, pref)
    while n % t:
        t //= 2
    return t


def _rms(x, g):
    ms = jnp.mean(x * x, axis=-1, keepdims=True)
    return x * lax.rsqrt(ms + NORM_EPS) * g


def _split3(x):
    hi = x.astype(BF16)
    r1 = x - hi.astype(F32)
    mid = r1.astype(BF16)
    lo = (r1 - mid.astype(F32)).astype(BF16)
    return hi, mid, lo


def _gdot(x, g, passes=3):
    parts = _split3(x)[:passes]
    acc = jnp.dot(parts[0], g, preferred_element_type=F32)
    for p in parts[1:]:
        acc = acc + jnp.dot(p, g, preferred_element_type=F32)
    return acc


def _silu(x):
    return x * jax.nn.sigmoid(x)


def _norm_mm_kernel(x_ref, g_ref, w_ref, o_ref, h_ref):
    @pl.when(pl.program_id(1) == 0)
    def _():
        h_ref[...] = _rms(x_ref[...], g_ref[...]).astype(BF16)

    o_ref[...] = jnp.dot(h_ref[...], w_ref[...], preferred_element_type=F32)


def _norm_matmul(x, g, w, tm, tn):
    t, d = x.shape
    n = w.shape[1]
    tm = _pick(t, tm)
    tn = _pick(n, tn)
    return pl.pallas_call(
        _norm_mm_kernel,
        grid=(t // tm, n // tn),
        in_specs=[
            pl.BlockSpec((tm, d), lambda i, j: (i, 0)),
            pl.BlockSpec((1, d), lambda i, j: (0, 0)),
            pl.BlockSpec((d, tn), lambda i, j: (0, j)),
        ],
        out_specs=pl.BlockSpec((tm, tn), lambda i, j: (i, j)),
        out_shape=jax.ShapeDtypeStruct((t, n), F32),
        scratch_shapes=[pltpu.VMEM((tm, d), BF16)],
        compiler_params=_cparams(("parallel", "arbitrary")),
    )(x, g.reshape(1, d), w)


def _hgrn_kernel(q_ref, f_ref, i_ref, g_ref, lb_ref, ng_ref, o_ref,
                 st_ref, qs_s, b_s, k_s, o_s, *, ts):
    @pl.when(pl.program_id(2) == 0)
    def _():
        st_ref[...] = jnp.zeros_like(st_ref)

    lb = lb_ref[...]
    fp = f_ref[...]
    f = lb + (1.0 - lb) * jax.nn.sigmoid(fp)
    glog = jnp.log(jnp.maximum(f, TINY))
    k_s[...] = (1.0 - lb) * jax.nn.sigmoid(-fp)
    qs_s[...] = _silu(q_ref[...])

    r16 = lax.broadcasted_iota(jnp.int32, (ts, HG_D), 0) % HG_SUB
    b = glog
    sh = 1
    while sh < HG_SUB:
        b = b + jnp.where(r16 >= sh, pltpu.roll(b, sh, axis=0), 0.0)
        sh *= 2
    b_s[...] = b

    rowi = lax.broadcasted_iota(jnp.int32, (HG_SUB, HG_D), 0)

    def body(ci, carry):
        base = pl.multiple_of(ci * HG_SUB, HG_SUB)
        qc = qs_s[pl.ds(base, HG_SUB), :]
        bc = b_s[pl.ds(base, HG_SUB), :]
        kc = k_s[pl.ds(base, HG_SUB), :]
        vc = i_ref[pl.ds(base, HG_SUB), :]
        st = st_ref[...]
        qe = qc * jnp.exp(bc)
        o = lax.dot_general(qe.astype(BF16), st.astype(BF16), (((1,), (1,)), ((), ())),
                            preferred_element_type=F32)
        for j in range(HG_SUB):
            bj = b_s[pl.ds(base + j, 1), :]
            kj = k_s[pl.ds(base + j, 1), :]
            vj = i_ref[pl.ds(base + j, 1), :]
            dec = jnp.exp(jnp.where(rowi >= j, bc - bj, MASK_VALUE))
            sc = jnp.sum(qc * dec * kj, axis=-1, keepdims=True)
            o = o + sc * vj
        o_s[pl.ds(base, HG_SUB), :] = o
        bl = b_s[pl.ds(base + HG_SUB - 1, 1), :]
        kd = kc * jnp.exp(bl - bc)
        upd = lax.dot_general(vc.astype(BF16), kd.astype(BF16), (((0,), (0,)), ((), ())),
                              preferred_element_type=F32)
        st_ref[...] = st * jnp.exp(bl) + upd
        return carry

    lax.fori_loop(0, ts // HG_SUB, body, 0)

    o = _rms(o_s[...], ng_ref[...])
    o_ref[...] = o * _silu(g_ref[...])


def _hgrn2(proj, lb, norm_g, b, s):
    ts = _pick(s, 256)
    nt = s // ts
    c0 = P_OFF_HG // HG_D

    def col(off):
        return pl.BlockSpec((ts, HG_D), lambda bi, h, c, off=off: (bi * nt + c, c0 + off + h))

    par = pl.BlockSpec((1, HG_D), lambda bi, h, c: (0, h))
    return pl.pallas_call(
        functools.partial(_hgrn_kernel, ts=ts),
        grid=(b, HG_HEADS, nt),
        in_specs=[col(0), col(HG_HEADS), col(2 * HG_HEADS), col(3 * HG_HEADS), par, par],
        out_specs=pl.BlockSpec((ts, HG_D), lambda bi, h, c: (bi * nt + c, h)),
        out_shape=jax.ShapeDtypeStruct((b * s, HG_WIDTH), F32),
        scratch_shapes=[pltpu.VMEM((HG_D, HG_D), F32)] + [pltpu.VMEM((ts, HG_D), F32)] * 4,
        compiler_params=_cparams(("parallel", "parallel", "arbitrary")),
    )(proj, proj, proj, proj, lb.reshape(1, HG_WIDTH), norm_g.reshape(1, HG_WIDTH))


def _da_kernel(q_ref, k_ref, v_ref, sl_ref, lam_ref, sg_ref, o_ref,
               kb_s, vb_s, q2_s, m_s, acc_s, *, tq, lambda_init):
    i = pl.program_id(2)
    nq = pl.num_programs(2)
    lanes = DA_DV

    @pl.when(i == 0)
    def _():
        def cast(c, carry):
            rows = pl.ds(pl.multiple_of(c * tq, tq), tq)
            kb_s[rows, :] = k_ref[rows, :].astype(BF16)
            vb_s[rows, 0:lanes] = v_ref[rows, :].astype(BF16)
            vb_s[rows, lanes:2 * lanes] = jnp.ones((tq, lanes), BF16)
            return carry
        lax.fori_loop(0, nq, cast, 0)

    q = q_ref[...] * (DA_DK ** -0.5)
    lane = lax.broadcasted_iota(jnp.int32, (tq, DA_DV), 1)
    q2_s[pl.ds(0, tq), :] = jnp.where(lane < DA_DK, q, 0.0).astype(BF16)
    q2_s[pl.ds(tq, tq), :] = jnp.where(lane >= DA_DK, q, 0.0).astype(BF16)
    m_s[...] = jnp.full_like(m_s, MASK_VALUE)
    acc_s[...] = jnp.zeros_like(acc_s)

    slope = sl_ref[0]
    colf = lax.broadcasted_iota(jnp.int32, (1, tq), 1).astype(F32)

    def step(j, masked):
        start = pl.multiple_of(j * tq, tq)
        s = lax.dot_general(q2_s[...], kb_s[pl.ds(start, tq), :], (((1,), (1,)), ((), ())),
                            preferred_element_type=F32)
        s = s + slope * (colf + ((j - i) * tq).astype(F32))
        if masked:
            r = lax.broadcasted_iota(jnp.int32, (2 * tq, tq), 0) % tq
            c = lax.broadcasted_iota(jnp.int32, (2 * tq, tq), 1)
            s = jnp.where(c <= r, s, MASK_VALUE)
        m_old = m_s[...]
        m_new = jnp.maximum(m_old, jnp.max(s, axis=-1, keepdims=True))
        alpha = jnp.exp(m_old - m_new)
        p = jnp.exp(s - pltpu.repeat(m_new, tq // lanes, axis=1))
        acc_s[...] = (pltpu.repeat(alpha, 2, axis=1) * acc_s[...]
                      + jnp.dot(p.astype(BF16), vb_s[pl.ds(start, tq), :],
                                preferred_element_type=F32))
        m_s[...] = m_new

    def body(j, carry):
        step(j, False)
        return carry

    lax.fori_loop(0, i, body, 0)
    step(i, True)

    lam = lam_ref[0]
    e1 = jnp.exp(jnp.sum(lam[0:1] * lam[1:2], axis=-1, keepdims=True))
    e2 = jnp.exp(jnp.sum(lam[2:3] * lam[3:4], axis=-1, keepdims=True))
    lam_full = e1 - e2 + lambda_init
    o1 = acc_s[pl.ds(0, tq), 0:lanes] / acc_s[pl.ds(0, tq), lanes:2 * lanes]
    o2 = acc_s[pl.ds(tq, tq), 0:lanes] / acc_s[pl.ds(tq, tq), lanes:2 * lanes]
    o = o1 - lam_full * o2
    o_ref[...] = _rms(o, sg_ref[...]) * (1.0 - lambda_init)


def _diff_attention(proj3, lam, subln_g, lambda_init, b, s):
    tq = _pick(s, 512)
    nq = s // tq
    c0 = P_OFF_DA // DA_DV
    slopes = 2.0 ** (-ALIBI_MAX_BIAS * jnp.arange(1, DA_HEADS + 1, dtype=F32) / DA_HEADS)
    slopes = jnp.broadcast_to(slopes[:, None, None], (DA_HEADS, 1, tq))
    return pl.pallas_call(
        functools.partial(_da_kernel, tq=tq, lambda_init=lambda_init),
        grid=(b, DA_HEADS, nq),
        in_specs=[
            pl.BlockSpec((None, tq, DA_DV), lambda bi, h, i: (bi, i, c0 + h)),
            pl.BlockSpec((None, s, DA_DV), lambda bi, h, i: (bi, 0, c0 + DA_HEADS + h)),
            pl.BlockSpec((None, s, DA_DV), lambda bi, h, i: (bi, 0, c0 + 2 * DA_HEADS + h)),
            pl.BlockSpec((1, 1, tq), lambda bi, h, i: (h, 0, 0)),
            pl.BlockSpec((1, 4, DA_DK), lambda bi, h, i: (0, 0, 0)),
            pl.BlockSpec((1, DA_DV), lambda bi, h, i: (0, 0)),
        ],
        out_specs=pl.BlockSpec((None, tq, DA_DV), lambda bi, h, i: (bi, i, h)),
        out_shape=jax.ShapeDtypeStruct((b, s, DA_WIDTH), F32),
        scratch_shapes=[
            pltpu.VMEM((s, DA_DV), BF16),
            pltpu.VMEM((s, 2 * DA_DV), BF16),
            pltpu.VMEM((2 * tq, DA_DV), BF16),
            pltpu.VMEM((2 * tq, DA_DV), F32),
            pltpu.VMEM((2 * tq, 2 * DA_DV), F32),
        ],
        compiler_params=_cparams(("parallel", "parallel", "arbitrary")),
    )(proj3, proj3, proj3, slopes, lam.reshape(1, 4, DA_DK), subln_g.reshape(1, DA_DV))


def _rw_prep_kernel(z_ref, zp_ref, mu_ref, wa_ref, w0_ref, a0_ref, gup_ref, kk_ref, ka_ref,
                    rk_ref, hs_ref,
                    nkk_o, w_o, bb_o, k_o, wr_o, v_o, c1_o, c2_o, c3_o, g_o, *, tiles_per_seq):
    tm = z_ref.shape[0]
    z = z_ref[...]
    first = (pl.program_id(0) % tiles_per_seq) == 0
    prev = jnp.where(first, 0.0, zp_ref[SUBLANES - 1:SUBLANES, :])
    row = lax.broadcasted_iota(jnp.int32, z.shape, 0)
    zs = jnp.where(row == 0, prev, pltpu.roll(z, 1, axis=0))
    z = z + mu_ref[...] * (zs - z)

    w3 = RW_WIDTH
    r = z[:, 0:w3]
    k = z[:, w3:2 * w3]
    v = z[:, 2 * w3:3 * w3]
    wd_ad = z[:, 3 * w3:3 * w3 + 128]
    gd = z[:, 3 * w3 + 128:3 * w3 + 256]
    lane = lax.broadcasted_iota(jnp.int32, wd_ad.shape, 1)
    lowrank_in = jnp.where(lane < RW_DECAY_RANK, jnp.tanh(wd_ad), wd_ad)
    wa = jnp.dot(lowrank_in.astype(BF16), wa_ref[...], preferred_element_type=F32)
    w = w0_ref[...] + wa[:, 0:w3]
    nw = -w
    softplus = jnp.maximum(nw, 0.0) + jnp.log(1.0 + jnp.exp(-jnp.abs(nw)))
    decay = jnp.exp(-jnp.exp(-softplus - 0.5))
    a = jax.nn.sigmoid(a0_ref[...] + wa[:, w3:2 * w3])
    g = jnp.dot(jax.nn.sigmoid(gd).astype(BF16), gup_ref[...], preferred_element_type=F32)

    hs = hs_ref[...]
    kk = k * kk_ref[...]
    kk = kk / jnp.maximum(jnp.sqrt(_gdot(kk * kk, hs)), 1e-12)
    kmod = k * (1.0 + (a - 1.0) * ka_ref[...])
    bb = kk * a

    nkk_o[...] = -kk
    w_o[...] = decay
    bb_o[...] = bb
    k_o[...] = kmod
    wr_o[...] = decay * r
    v_o[...] = v
    c1_o[...] = _gdot(bb * r, hs)
    c2_o[...] = _gdot(kmod * r, hs)
    c3_o[...] = _gdot(r * kmod * rk_ref[...], hs)
    g_o[...] = g


def _head_sum_matrix(n, head):
    idx = jnp.arange(n) // head
    return (idx[:, None] == idx[None, :]).astype(BF16)


def _rwkv_prep(proj, p, b, s):
    t = b * s
    tm = _pick(s, 256)
    tiles_per_seq = s // tm
    w3 = RW_WIDTH
    wa = jnp.zeros((128, 2 * w3), F32)
    wa = wa.at[0:RW_DECAY_RANK, 0:w3].set(p["w_up"]).at[RW_DECAY_RANK:128, w3:].set(p["a_up"])
    mu = jnp.pad(p["mu"], (0, RW_PAD - RW_COLS)).reshape(1, RW_PAD)
    row = lambda a: a.reshape(1, w3)
    const = lambda shape: pl.BlockSpec(shape, lambda i: (0, 0))
    out_spec = pl.BlockSpec((tm, w3), lambda i: (i, 0))
    nb8 = tm // SUBLANES
    return pl.pallas_call(
        functools.partial(_rw_prep_kernel, tiles_per_seq=tiles_per_seq),
        grid=(t // tm,),
        in_specs=[
            pl.BlockSpec((tm, RW_PAD), lambda i: (i, 0)),
            pl.BlockSpec((SUBLANES, RW_PAD), lambda i: (jnp.maximum(i * nb8 - 1, 0), 0)),
            const((1, RW_PAD)), const((128, 2 * w3)), const((1, w3)), const((1, w3)),
            const((RW_GATE_RANK, w3)), const((1, w3)), const((1, w3)), const((1, w3)),
            const((w3, w3)),
        ],
        out_specs=[out_spec] * 10,
        out_shape=[jax.ShapeDtypeStruct((t, w3), F32)] * 10,
        compiler_params=_cparams(("parallel",)),
    )(proj, proj, mu, wa.astype(BF16), row(p["w0"]), row(p["a0"]), p["g_up"].astype(BF16),
      row(p["k_k"]), row(p["k_a"]), row(p["r_k"]), _head_sum_matrix(w3, RW_HEAD))


def _rw_scan_kernel(nkk_ref, w_ref, bb_ref, k_ref, wr_ref, v_ref, c1_ref, c2_ref, hs_ref, dm_ref,
                    y_ref, st_ref, hl_s, lhs_s, e_nkk, e_w, e_bb, e_k, e_wr, *, tt):
    @pl.when(pl.program_id(1) == 0)
    def _():
        st_ref[...] = jnp.zeros_like(st_ref)

    hs = hs_ref[...]
    diag = dm_ref[...]
    half = RW_WIDTH // 2

    def expand(src, dst):
        x = src[0]
        hi = x.astype(BF16).astype(F32)
        hl_s[0] = hi
        hl_s[1] = x - hi
        for p in range(2):
            for t in range(tt):
                row = jnp.broadcast_to(hl_s[p, pl.ds(t, 1), :], diag.shape)
                lhs_s[p, pl.ds(t * RW_HEAD, RW_HEAD), :] = (row * diag).astype(BF16)
        for c in range(2):
            cols = pl.ds(c * half, half)
            dst[:, cols] = (jnp.dot(lhs_s[0, :, cols], hs, preferred_element_type=F32)
                            + jnp.dot(lhs_s[1, :, cols], hs, preferred_element_type=F32))

    expand(nkk_ref, e_nkk)
    expand(w_ref, e_w)
    expand(bb_ref, e_bb)
    expand(k_ref, e_k)
    expand(wr_ref, e_wr)

    def body(t, carry):
        rows = pl.ds(pl.multiple_of(t * RW_HEAD, RW_HEAD), RW_HEAD)
        st = st_ref[...]
        sa = jnp.sum(st * e_nkk[rows, :], axis=0, keepdims=True)
        yw = jnp.sum(st * e_wr[rows, :], axis=0, keepdims=True)
        v = v_ref[0, pl.ds(t, 1), :]
        y_ref[0, pl.ds(t, 1), :] = (yw + c1_ref[0, pl.ds(t, 1), :] * sa
                                    + c2_ref[0, pl.ds(t, 1), :] * v)
        st_ref[...] = st * e_w[rows, :] + e_bb[rows, :] * sa + e_k[rows, :] * v
        return carry

    lax.fori_loop(0, tt, body, 0)


def _rwkv_scan(nkk, w, bb, k, wr, v, c1, c2, b, s):
    tt = _pick(s, 16)
    w3 = RW_WIDTH
    r3 = lambda a: a.reshape(b, s, w3)
    spec = pl.BlockSpec((1, tt, w3), lambda bi, i: (bi, i, 0))
    ex = pltpu.VMEM((tt * RW_HEAD, w3), F32)
    sub = jnp.arange(RW_HEAD)[:, None]
    diag = ((jnp.arange(w3)[None, :] % RW_HEAD) == sub).astype(F32)
    return pl.pallas_call(
        functools.partial(_rw_scan_kernel, tt=tt),
        grid=(b, s // tt),
        in_specs=[spec] * 8 + [pl.BlockSpec((w3 // 2, w3 // 2), lambda bi, i: (0, 0)),
                               pl.BlockSpec((RW_HEAD, w3), lambda bi, i: (0, 0))],
        out_specs=spec,
        out_shape=jax.ShapeDtypeStruct((b, s, w3), F32),
        scratch_shapes=[pltpu.VMEM((RW_HEAD, w3), F32), pltpu.VMEM((2, tt, w3), F32),
                        pltpu.VMEM((2, tt * RW_HEAD, w3), BF16), ex, ex, ex, ex, ex],
        compiler_params=_cparams(("parallel", "arbitrary")),
    )(r3(nkk), r3(w), r3(bb), r3(k), r3(wr), r3(v), r3(c1), r3(c2),
      _head_sum_matrix(w3 // 2, RW_HEAD), diag).reshape(b * s, w3)


def _merge_kernel(x_ref, hg_ref, da_ref, y_ref, v_ref, c3_ref, g_ref, g0_ref, g1_ref, g2_ref,
                  bg_ref, lng_ref, lnb_ref, hs_ref, wb_ref, wo_ref, o_ref):
    hs = hs_ref[...]
    y = y_ref[...]
    mu = _gdot(y, hs) * (1.0 / RW_HEAD)
    yc = y - mu
    var = _gdot(yc * yc, hs) * (1.0 / RW_HEAD)
    yn = yc * lax.rsqrt(var + RW_GN_EPS) * lng_ref[...] + lnb_ref[...]
    o_rw = (yn + c3_ref[...] * v_ref[...]) * g_ref[...]

    merged = None
    for n, (br, gp) in enumerate(((hg_ref[...], g0_ref), (da_ref[...], g1_ref), (o_rw, g2_ref))):
        gate = jax.nn.sigmoid(gp[...] + bg_ref[:, n * D_MODEL:(n + 1) * D_MODEL])
        pb = jnp.dot(br.astype(BF16), wb_ref[n], preferred_element_type=F32)
        merged = gate * pb if merged is None else merged + gate * pb
    o_ref[...] = x_ref[...] + jnp.dot(merged.astype(BF16), wo_ref[...],
                                      preferred_element_type=F32)


def _merge(x, o_hg, o_da, y, v, c3, g, proj, b_gate, ln_g, ln_b, w_branch, w_out):
    t = x.shape[0]
    tm = _pick(t, 256)
    bw = BRANCH_WIDTH
    row = pl.BlockSpec((tm, bw), lambda i: (i, 0))
    g0 = P_OFF_GATE // D_MODEL
    gate_spec = lambda n: pl.BlockSpec((tm, D_MODEL), lambda i, n=n: (i, g0 + n))
    const = lambda shape: pl.BlockSpec(shape, lambda i: (0,) * len(shape))
    return pl.pallas_call(
        _merge_kernel,
        grid=(t // tm,),
        in_specs=[pl.BlockSpec((tm, D_MODEL), lambda i: (i, 0))] + [row] * 6
        + [gate_spec(0), gate_spec(1), gate_spec(2),
           const((1, N_BRANCH * D_MODEL)), const((1, bw)), const((1, bw)), const((bw, bw)),
           const((N_BRANCH, bw, D_MODEL)), const((D_MODEL, D_MODEL))],
        out_specs=pl.BlockSpec((tm, D_MODEL), lambda i: (i, 0)),
        out_shape=jax.ShapeDtypeStruct((t, D_MODEL), F32),
        compiler_params=_cparams(("parallel",)),
    )(x, o_hg, o_da, y, v, c3, g, proj, proj, proj, b_gate.reshape(1, -1),
      ln_g.reshape(1, bw), ln_b.reshape(1, bw), _head_sum_matrix(bw, RW_HEAD),
      w_branch.astype(BF16), w_out.astype(BF16))


def _xattn_kernel(x_ref, g_ref, wq_ref, kv_ref, wo_ref, o_ref):
    x = x_ref[...]
    h = _rms(x, g_ref[...]).astype(BF16)
    q = jnp.dot(h, wq_ref[...], preferred_element_type=F32) * (XA_HEAD ** -0.5)
    outs = []
    for hd in range(XA_HEADS):
        qh = q[:, hd * XA_HEAD:(hd + 1) * XA_HEAD].astype(BF16)
        kh = kv_ref[0, :, hd * XA_HEAD:(hd + 1) * XA_HEAD].astype(BF16)
        vh = kv_ref[0, :, D_MODEL + hd * XA_HEAD:D_MODEL + (hd + 1) * XA_HEAD].astype(BF16)
        s = lax.dot_general(qh, kh, (((1,), (1,)), ((), ())), preferred_element_type=F32)
        m = jnp.max(s, axis=-1, keepdims=True)
        p = jnp.exp(s - m)
        p = p / jnp.sum(p, axis=-1, keepdims=True)
        outs.append(jnp.dot(p.astype(BF16), vh, preferred_element_type=F32))
    o = jnp.concatenate(outs, axis=-1).astype(BF16)
    o_ref[...] = x + jnp.dot(o, wo_ref[...], preferred_element_type=F32)


def _cross_attention(x, g, wq, kv, wo, b, s):
    tm = _pick(s, 256)
    nt = s // tm
    m = kv.shape[1]
    return pl.pallas_call(
        _xattn_kernel,
        grid=(b, nt),
        in_specs=[
            pl.BlockSpec((tm, D_MODEL), lambda bi, i: (bi * nt + i, 0)),
            pl.BlockSpec((1, D_MODEL), lambda bi, i: (0, 0)),
            pl.BlockSpec((D_MODEL, D_MODEL), lambda bi, i: (0, 0)),
            pl.BlockSpec((1, m, 2 * D_MODEL), lambda bi, i: (bi, 0, 0)),
            pl.BlockSpec((D_MODEL, D_MODEL), lambda bi, i: (0, 0)),
        ],
        out_specs=pl.BlockSpec((tm, D_MODEL), lambda bi, i: (bi * nt + i, 0)),
        out_shape=jax.ShapeDtypeStruct((b * s, D_MODEL), F32),
        compiler_params=_cparams(("parallel", "parallel")),
    )(x, g.reshape(1, D_MODEL), wq.astype(BF16), kv, wo.astype(BF16))


def _ffn_kernel(x_ref, xp_ref, g_ref, wu_ref, wv_ref, cw_ref, cb_ref, wd_ref, o_ref,
                h_s, hp_s, acc_s, *, tiles_per_seq):
    j = pl.program_id(1)
    tm = x_ref.shape[0]

    @pl.when(j == 0)
    def _():
        h_s[...] = _rms(x_ref[...], g_ref[...]).astype(BF16)
        first = (pl.program_id(0) % tiles_per_seq) == 0
        hp = _rms(xp_ref[...], g_ref[...])
        hp_s[...] = jnp.where(first, 0.0, hp).astype(BF16)
        acc_s[...] = jnp.zeros_like(acc_s)

    wu = wu_ref[...]
    u = jnp.dot(h_s[...], wu, preferred_element_type=F32)
    up = jnp.dot(hp_s[...], wu, preferred_element_type=F32)
    vv = jnp.dot(h_s[...], wv_ref[...], preferred_element_type=F32)
    row = lax.broadcasted_iota(jnp.int32, u.shape, 0)
    p1 = up[SUBLANES - 1:SUBLANES, :]
    p2 = up[SUBLANES - 2:SUBLANES - 1, :]
    u1 = jnp.where(row == 0, p1, pltpu.roll(u, 1, axis=0))
    u2 = jnp.where(row == 0, p2, jnp.where(row == 1, p1, pltpu.roll(u, 2, axis=0)))
    cw = cw_ref[...]
    uc = cw[0:1, :] * u2 + cw[1:2, :] * u1 + cw[2:3, :] * u + cb_ref[...]
    act = (_silu(uc) * vv).astype(BF16)
    acc_s[...] += jnp.dot(act, wd_ref[...], preferred_element_type=F32)

    @pl.when(j == pl.num_programs(1) - 1)
    def _():
        o_ref[...] = x_ref[...] + acc_s[...]


def _ffn(x, g, w_up, conv_w, conv_b, w_down, b, s):
    t = x.shape[0]
    tm = _pick(s, 512)
    tiles_per_seq = s // tm
    tf = 256
    nf = D_FF // tf
    nb8 = tm // SUBLANES
    wu = w_up.astype(BF16)
    return pl.pallas_call(
        functools.partial(_ffn_kernel, tiles_per_seq=tiles_per_seq),
        grid=(t // tm, nf),
        in_specs=[
            pl.BlockSpec((tm, D_MODEL), lambda i, j: (i, 0)),
            pl.BlockSpec((SUBLANES, D_MODEL), lambda i, j: (jnp.maximum(i * nb8 - 1, 0), 0)),
            pl.BlockSpec((1, D_MODEL), lambda i, j: (0, 0)),
            pl.BlockSpec((D_MODEL, tf), lambda i, j: (0, j)),
            pl.BlockSpec((D_MODEL, tf), lambda i, j: (0, nf + j)),
            pl.BlockSpec((CONV_W, tf), lambda i, j: (0, j)),
            pl.BlockSpec((1, tf), lambda i, j: (0, j)),
            pl.BlockSpec((tf, D_MODEL), lambda i, j: (j, 0)),
        ],
        out_specs=pl.BlockSpec((tm, D_MODEL), lambda i, j: (i, 0)),
        out_shape=jax.ShapeDtypeStruct((t, D_MODEL), F32),
        scratch_shapes=[pltpu.VMEM((tm, D_MODEL), BF16), pltpu.VMEM((SUBLANES, D_MODEL), BF16),
                        pltpu.VMEM((tm, D_MODEL), F32)],
        compiler_params=_cparams(("parallel", "arbitrary")),
    )(x, x, g.reshape(1, D_MODEL), wu, wu, conv_w, conv_b.reshape(1, D_FF),
      w_down.astype(BF16))


def _norm_kernel(x_ref, g_ref, o_ref):
    o_ref[...] = _rms(x_ref[...], g_ref[...])


def _final_norm(x, g):
    t, d = x.shape
    tm = _pick(t, 512)
    return pl.pallas_call(
        _norm_kernel,
        grid=(t // tm,),
        in_specs=[pl.BlockSpec((tm, d), lambda i: (i, 0)), pl.BlockSpec((1, d), lambda i: (0, 0))],
        out_specs=pl.BlockSpec((tm, d), lambda i: (i, 0)),
        out_shape=jax.ShapeDtypeStruct((t, d), F32),
        compiler_params=_cparams(("parallel",)),
    )(x, g.reshape(1, d))


def _permute_w_in(w):
    d = w.shape[0]
    return jnp.concatenate([
        w[:, R_OFF_RW:R_OFF_GATE], jnp.zeros((d, RW_PAD - RW_COLS), w.dtype),
        w[:, 0:R_OFF_DA], w[:, R_OFF_GATE:], w[:, R_OFF_DA:R_OFF_RW]], axis=1).astype(BF16)


def kernel(x, mem, norm_mix_g, w_in, b_gate, hgrn_lb_param, hgrn_norm_g, diff_lambda, diff_subln_g, rwkv_mu, rwkv_w0, rwkv_w_up, rwkv_a0, rwkv_a_up, rwkv_g_up, rwkv_k_k, rwkv_k_a, rwkv_r_k, rwkv_ln_g, rwkv_ln_b, w_branch, w_out, norm_xa_g, norm_mem_g, xa_wq, xa_wkv, xa_wo, norm_ffn_g, ffn_w_up, ffn_conv_w, ffn_conv_b, ffn_w_down, final_norm_g):
    b, s, d = x.shape
    depth = w_in.shape[0]
    m = mem.shape[1]
    t = b * s
    lb_p = jax.nn.softmax(hgrn_lb_param.astype(F32), axis=0)
    lower_bounds = jnp.cumsum(lb_p, axis=0) - lb_p[0]

    xf = x.reshape(t, d)
    memf = mem.reshape(b * m, d)
    for l in range(depth):
        proj = _norm_matmul(xf, norm_mix_g[l], _permute_w_in(w_in[l]), 512, 512)
        o_hg = _hgrn2(proj, lower_bounds[l], hgrn_norm_g[l], b, s)
        lambda_init = 0.8 - 0.6 * math.exp(-0.3 * l)
        o_da = _diff_attention(proj.reshape(b, s, N_PROJ), diff_lambda[l], diff_subln_g[l],
                               lambda_init, b, s).reshape(t, DA_WIDTH)
        rw_params = dict(mu=rwkv_mu[l], w0=rwkv_w0[l], w_up=rwkv_w_up[l], a0=rwkv_a0[l],
                         a_up=rwkv_a_up[l], g_up=rwkv_g_up[l], k_k=rwkv_k_k[l], k_a=rwkv_k_a[l],
                         r_k=rwkv_r_k[l].reshape(-1))
        nkk, w, bb, k, wr, v, c1, c2, c3, g = _rwkv_prep(proj, rw_params, b, s)
        y = _rwkv_scan(nkk, w, bb, k, wr, v, c1, c2, b, s)
        xf = _merge(xf, o_hg, o_da, y, v, c3, g, proj, b_gate[l], rwkv_ln_g[l], rwkv_ln_b[l],
                    w_branch[l], w_out[l])
        kv = _norm_matmul(memf, norm_mem_g[l], xa_wkv[l].astype(BF16), 256, 512)
        xf = _cross_attention(xf, norm_xa_g[l], xa_wq[l], kv.reshape(b, m, 2 * d), xa_wo[l], b, s)
        xf = _ffn(xf, norm_ffn_g[l], ffn_w_up[l], ffn_conv_w[l], ffn_conv_b[l], ffn_w_down[l], b, s)
    return _final_norm(xf, final_norm_g).reshape(b, s, d)
```

```python
import functools
import math

import jax
import jax.numpy as jnp
from jax import lax
from jax.experimental import pallas as pl
from jax.experimental.pallas import tpu as pltpu

F32 = jnp.float32
BF16 = jnp.bfloat16

D_MODEL = 1024
NORM_EPS = 1e-6
MASK_VALUE = -1e30
TINY = 1e-30

HG_HEADS = 4
HG_D = 128
HG_WIDTH = HG_HEADS * HG_D
HG_SUB = 16

DA_HEADS = 4
DA_DK = 64
DA_DV = 128
DA_WIDTH = DA_HEADS * DA_DV
ALIBI_MAX_BIAS = 8.0

RW_HEADS = 8
RW_HEAD = 64
RW_WIDTH = RW_HEADS * RW_HEAD
RW_DECAY_RANK = 64
RW_A_RANK = 64
RW_GATE_RANK = 128
RW_COLS = 3 * RW_WIDTH + RW_DECAY_RANK + RW_A_RANK + RW_GATE_RANK
RW_GN_EPS = 64e-5

N_BRANCH = 3
BRANCH_WIDTH = 512

XA_HEADS = 4
XA_HEAD = D_MODEL // XA_HEADS

D_FF = 2816
CONV_W = 3

RW_PAD = 2048
P_OFF_RW = 0
P_OFF_HG = RW_PAD
P_OFF_GATE = P_OFF_HG + 4 * HG_WIDTH
P_OFF_DA = P_OFF_GATE + N_BRANCH * D_MODEL
N_PROJ = P_OFF_DA + 3 * DA_WIDTH

R_OFF_DA = 4 * HG_WIDTH
R_OFF_RW = R_OFF_DA + 3 * DA_WIDTH
R_OFF_GATE = R_OFF_RW + RW_COLS

V7X_VMEM_LIMIT = 56 * 1024 * 1024
SUBLANES = 8


def _cparams(sem):
    return pltpu.CompilerParams(dimension_semantics=sem, vmem_limit_bytes=V7X_VMEM_LIMIT)


def _pick(n, pref):
    t = min(n, pref)
    while n % t:
        t //= 2
    return t


def _rms(x, g):
    ms = jnp.mean(x * x, axis=-1, keepdims=True)
    return x * lax.rsqrt(ms + NORM_EPS) * g


def _split3(x):
    hi = x.astype(BF16)
    r1 = x - hi.astype(F32)
    mid = r1.astype(BF16)
    lo = (r1 - mid.astype(F32)).astype(BF16)
    return hi, mid, lo


def _gdot(x, g, passes=3):
    parts = _split3(x)[:passes]
    acc = jnp.dot(parts[0], g, preferred_element_type=F32)
    for p in parts[1:]:
        acc = acc + jnp.dot(p, g, preferred_element_type=F32)
    return acc


def _silu(x):
    return x * jax.nn.sigmoid(x)


def _norm_mm_kernel(x_ref, g_ref, w_ref, o_ref, h_ref):
    @pl.when(pl.program_id(1) == 0)
    def _():
        h_ref[...] = _rms(x_ref[...], g_ref[...]).astype(BF16)

    o_ref[...] = jnp.dot(h_ref[...], w_ref[...], preferred_element_type=F32)


def _norm_matmul(x, g, w, tm, tn):
    t, d = x.shape
    n = w.shape[1]
    tm = _pick(t, tm)
    tn = _pick(n, tn)
    return pl.pallas_call(
        _norm_mm_kernel,
        grid=(t // tm, n // tn),
        in_specs=[
            pl.BlockSpec((tm, d), lambda i, j: (i, 0)),
            pl.BlockSpec((1, d), lambda i, j: (0, 0)),
            pl.BlockSpec((d, tn), lambda i, j: (0, j)),
        ],
        out_specs=pl.BlockSpec((tm, tn), lambda i, j: (i, j)),
        out_shape=jax.ShapeDtypeStruct((t, n), F32),
        scratch_shapes=[pltpu.VMEM((tm, d), BF16)],
        compiler_params=_cparams(("parallel", "arbitrary")),
    )(x, g.reshape(1, d), w)


def _hgrn_kernel(q_ref, f_ref, i_ref, g_ref, lb_ref, ng_ref, o_ref,
                 st_ref, qs_s, b_s, k_s, o_s, *, ts):
    @pl.when(pl.program_id(2) == 0)
    def _():
        st_ref[...] = jnp.zeros_like(st_ref)

    lb = lb_ref[...]
    fp = f_ref[...]
    f = lb + (1.0 - lb) * jax.nn.sigmoid(fp)
    glog = jnp.log(jnp.maximum(f, TINY))
    k_s[...] = (1.0 - lb) * jax.nn.sigmoid(-fp)
    qs_s[...] = _silu(q_ref[...])

    r16 = lax.broadcasted_iota(jnp.int32, (ts, HG_D), 0) % HG_SUB
    b = glog
    sh = 1
    while sh < HG_SUB:
        b = b + jnp.where(r16 >= sh, pltpu.roll(b, sh, axis=0), 0.0)
        sh *= 2
    b_s[...] = b

    rowi = lax.broadcasted_iota(jnp.int32, (HG_SUB, HG_D), 0)

    def body(ci, carry):
        base = pl.multiple_of(ci * HG_SUB, HG_SUB)
        qc = qs_s[pl.ds(base, HG_SUB), :]
        bc = b_s[pl.ds(base, HG_SUB), :]
        kc = k_s[pl.ds(base, HG_SUB), :]
        vc = i_ref[pl.ds(base, HG_SUB), :]
        st = st_ref[...]
        qe = qc * jnp.exp(bc)
        o = lax.dot_general(qe.astype(BF16), st.astype(BF16), (((1,), (1,)), ((), ())),
                            preferred_element_type=F32)
        for j in range(HG_SUB):
            bj = b_s[pl.ds(base + j, 1), :]
            kj = k_s[pl.ds(base + j, 1), :]
            vj = i_ref[pl.ds(base + j, 1), :]
            dec = jnp.exp(jnp.where(rowi >= j, bc - bj, MASK_VALUE))
            sc = jnp.sum(qc * dec * kj, axis=-1, keepdims=True)
            o = o + sc * vj
        o_s[pl.ds(base, HG_SUB), :] = o
        bl = b_s[pl.ds(base + HG_SUB - 1, 1), :]
        kd = kc * jnp.exp(bl - bc)
        upd = lax.dot_general(vc.astype(BF16), kd.astype(BF16), (((0,), (0,)), ((), ())),
                              preferred_element_type=F32)
        st_ref[...] = st * jnp.exp(bl) + upd
        return carry

    lax.fori_loop(0, ts // HG_SUB, body, 0)

    o = _rms(o_s[...], ng_ref[...])
    o_ref[...] = o * _silu(g_ref[...])


def _hgrn2(proj, lb, norm_g, b, s):
    ts = _pick(s, 256)
    nt = s // ts
    c0 = P_OFF_HG // HG_D

    def col(off):
        return pl.BlockSpec((ts, HG_D), lambda bi, h, c, off=off: (bi * nt + c, c0 + off + h))

    par = pl.BlockSpec((1, HG_D), lambda bi, h, c: (0, h))
    return pl.pallas_call(
        functools.partial(_hgrn_kernel, ts=ts),
        grid=(b, HG_HEADS, nt),
        in_specs=[col(0), col(HG_HEADS), col(2 * HG_HEADS), col(3 * HG_HEADS), par, par],
        out_specs=pl.BlockSpec((ts, HG_D), lambda bi, h, c: (bi * nt + c, h)),
        out_shape=jax.ShapeDtypeStruct((b * s, HG_WIDTH), F32),
        scratch_shapes=[pltpu.VMEM((HG_D, HG_D), F32)] + [pltpu.VMEM((ts, HG_D), F32)] * 4,
        compiler_params=_cparams(("parallel", "parallel", "arbitrary")),
    )(proj, proj, proj, proj, lb.reshape(1, HG_WIDTH), norm_g.reshape(1, HG_WIDTH))


def _da_kernel(q_ref, k_ref, v_ref, sl_ref, lam_ref, sg_ref, o_ref,
               kb_s, vb_s, q2_s, m_s, acc_s, *, tq, lambda_init):
    i = pl.program_id(2)
    nq = pl.num_programs(2)
    lanes = DA_DV

    @pl.when(i == 0)
    def _():
        def cast(c, carry):
            rows = pl.ds(pl.multiple_of(c * tq, tq), tq)
            kb_s[rows, :] = k_ref[rows, :].astype(BF16)
            vb_s[rows, 0:lanes] = v_ref[rows, :].astype(BF16)
            vb_s[rows, lanes:2 * lanes] = jnp.ones((tq, lanes), BF16)
            return carry
        lax.fori_loop(0, nq, cast, 0)

    q = q_ref[...] * (DA_DK ** -0.5)
    lane = lax.broadcasted_iota(jnp.int32, (tq, DA_DV), 1)
    q2_s[pl.ds(0, tq), :] = jnp.where(lane < DA_DK, q, 0.0).astype(BF16)
    q2_s[pl.ds(tq, tq), :] = jnp.where(lane >= DA_DK, q, 0.0).astype(BF16)
    m_s[...] = jnp.full_like(m_s, MASK_VALUE)
    acc_s[...] = jnp.zeros_like(acc_s)

    slope = sl_ref[0]
    colf = lax.broadcasted_iota(jnp.int32, (1, tq), 1).astype(F32)

    def step(j, masked):
        start = pl.multiple_of(j * tq, tq)
        s = lax.dot_general(q2_s[...], kb_s[pl.ds(start, tq), :], (((1,), (1,)), ((), ())),
                            preferred_element_type=F32)
        s = s + slope * (colf + ((j - i) * tq).astype(F32))
        if masked:
            r = lax.broadcasted_iota(jnp.int32, (2 * tq, tq), 0) % tq
            c = lax.broadcasted_iota(jnp.int32, (2 * tq, tq), 1)
            s = jnp.where(c <= r, s, MASK_VALUE)
        m_old = m_s[...]
        m_new = jnp.maximum(m_old, jnp.max(s, axis=-1, keepdims=True))
        alpha = jnp.exp(m_old - m_new)
        p = jnp.exp(s - pltpu.repeat(m_new, tq // lanes, axis=1))
        acc_s[...] = (pltpu.repeat(alpha, 2, axis=1) * acc_s[...]
                      + jnp.dot(p.astype(BF16), vb_s[pl.ds(start, tq), :],
                                preferred_element_type=F32))
        m_s[...] = m_new

    def body(j, carry):
        step(j, False)
        return carry

    lax.fori_loop(0, i, body, 0)
    step(i, True)

    lam = lam_ref[0]
    e1 = jnp.exp(jnp.sum(lam[0:1] * lam[1:2], axis=-1, keepdims=True))
    e2 = jnp.exp(jnp.sum(lam[2:3] * lam[3:4], axis=-1, keepdims=True))
    lam_full = e1 - e2 + lambda_init
    o1 = acc_s[pl.ds(0, tq), 0:lanes] / acc_s[pl.ds(0, tq), lanes:2 * lanes]
    o2 = acc_s[pl.ds(tq, tq), 0:lanes] / acc_s[pl.ds(tq, tq), lanes:2 * lanes]
    o = o1 - lam_full * o2
    o_ref[...] = _rms(o, sg_ref[...]) * (1.0 - lambda_init)


def _diff_attention(proj3, lam, subln_g, lambda_init, b, s):
    tq = _pick(s, 512)
    nq = s // tq
    c0 = P_OFF_DA // DA_DV
    slopes = 2.0 ** (-ALIBI_MAX_BIAS * jnp.arange(1, DA_HEADS + 1, dtype=F32) / DA_HEADS)
    slopes = jnp.broadcast_to(slopes[:, None, None], (DA_HEADS, 1, tq))
    return pl.pallas_call(
        functools.partial(_da_kernel, tq=tq, lambda_init=lambda_init),
        grid=(b, DA_HEADS, nq),
        in_specs=[
            pl.BlockSpec((None, tq, DA_DV), lambda bi, h, i: (bi, i, c0 + h)),
            pl.BlockSpec((None, s, DA_DV), lambda bi, h, i: (bi, 0, c0 + DA_HEADS + h)),
            pl.BlockSpec((None, s, DA_DV), lambda bi, h, i: (bi, 0, c0 + 2 * DA_HEADS + h)),
            pl.BlockSpec((1, 1, tq), lambda bi, h, i: (h, 0, 0)),
            pl.BlockSpec((1, 4, DA_DK), lambda bi, h, i: (0, 0, 0)),
            pl.BlockSpec((1, DA_DV), lambda bi, h, i: (0, 0)),
        ],
        out_specs=pl.BlockSpec((None, tq, DA_DV), lambda bi, h, i: (bi, i, h)),
        out_shape=jax.ShapeDtypeStruct((b, s, DA_WIDTH), F32),
        scratch_shapes=[
            pltpu.VMEM((s, DA_DV), BF16),
            pltpu.VMEM((s, 2 * DA_DV), BF16),
            pltpu.VMEM((2 * tq, DA_DV), BF16),
            pltpu.VMEM((2 * tq, DA_DV), F32),
            pltpu.VMEM((2 * tq, 2 * DA_DV), F32),
        ],
        compiler_params=_cparams(("parallel", "parallel", "arbitrary")),
    )(proj3, proj3, proj3, slopes, lam.reshape(1, 4, DA_DK), subln_g.reshape(1, DA_DV))


def _rw_prep_kernel(z_ref, zp_ref, mu_ref, wa_ref, w0_ref, a0_ref, gup_ref, kk_ref, ka_ref,
                    rk_ref, hs_ref,
                    lw_o, r_o, k_o, v_o, kk_o, a_o, c3_o, g_o, *, tiles_per_seq):
    tm = z_ref.shape[0]
    z = z_ref[...]
    first = (pl.program_id(0) % tiles_per_seq) == 0
    prev = jnp.where(first, 0.0, zp_ref[SUBLANES - 1:SUBLANES, :])
    row = lax.broadcasted_iota(jnp.int32, z.shape, 0)
    zs = jnp.where(row == 0, prev, pltpu.roll(z, 1, axis=0))
    z = z + mu_ref[...] * (zs - z)

    w3 = RW_WIDTH
    r = z[:, 0:w3]
    k = z[:, w3:2 * w3]
    v = z[:, 2 * w3:3 * w3]
    wd_ad = z[:, 3 * w3:3 * w3 + 128]
    gd = z[:, 3 * w3 + 128:3 * w3 + 256]
    lane = lax.broadcasted_iota(jnp.int32, wd_ad.shape, 1)
    lowrank_in = jnp.where(lane < RW_DECAY_RANK, jnp.tanh(wd_ad), wd_ad)
    wa = jnp.dot(lowrank_in.astype(BF16), wa_ref[...], preferred_element_type=F32)
    w = w0_ref[...] + wa[:, 0:w3]
    nw = -w
    softplus = jnp.maximum(nw, 0.0) + jnp.log(1.0 + jnp.exp(-jnp.abs(nw)))
    log_decay = -jnp.exp(-softplus - 0.5)
    a = jax.nn.sigmoid(a0_ref[...] + wa[:, w3:2 * w3])
    g = jnp.dot(jax.nn.sigmoid(gd).astype(BF16), gup_ref[...], preferred_element_type=F32)

    hs = hs_ref[...]
    kk = k * kk_ref[...]
    kk = kk / jnp.maximum(jnp.sqrt(_gdot(kk * kk, hs)), 1e-12)
    kmod = k * (1.0 + (a - 1.0) * ka_ref[...])

    lw_o[...] = log_decay
    r_o[...] = r
    k_o[...] = kmod
    v_o[...] = v
    kk_o[...] = kk
    a_o[...] = a
    c3_o[...] = _gdot(r * kmod * rk_ref[...], hs)
    g_o[...] = g


def _head_sum_matrix(n, head):
    idx = jnp.arange(n) // head
    return (idx[:, None] == idx[None, :]).astype(BF16)


def _rwkv_prep(proj, p, b, s):
    t = b * s
    tm = _pick(s, 256)
    tiles_per_seq = s // tm
    w3 = RW_WIDTH
    wa = jnp.zeros((128, 2 * w3), F32)
    wa = wa.at[0:RW_DECAY_RANK, 0:w3].set(p["w_up"]).at[RW_DECAY_RANK:128, w3:].set(p["a_up"])
    mu = jnp.pad(p["mu"], (0, RW_PAD - RW_COLS)).reshape(1, RW_PAD)
    row = lambda a: a.reshape(1, w3)
    const = lambda shape: pl.BlockSpec(shape, lambda i: (0, 0))
    out_spec = pl.BlockSpec((tm, w3), lambda i: (i, 0))
    nb8 = tm // SUBLANES
    return pl.pallas_call(
        functools.partial(_rw_prep_kernel, tiles_per_seq=tiles_per_seq),
        grid=(t // tm,),
        in_specs=[
            pl.BlockSpec((tm, RW_PAD), lambda i: (i, 0)),
            pl.BlockSpec((SUBLANES, RW_PAD), lambda i: (jnp.maximum(i * nb8 - 1, 0), 0)),
            const((1, RW_PAD)), const((128, 2 * w3)), const((1, w3)), const((1, w3)),
            const((RW_GATE_RANK, w3)), const((1, w3)), const((1, w3)), const((1, w3)),
            const((w3, w3)),
        ],
        out_specs=[out_spec] * 8,
        out_shape=[jax.ShapeDtypeStruct((t, w3), F32)] * 8,
        compiler_params=_cparams(("parallel",)),
    )(proj, proj, mu, wa.astype(BF16), row(p["w0"]), row(p["a0"]), p["g_up"].astype(BF16),
      row(p["k_k"]), row(p["k_a"]), row(p["r_k"]), _head_sum_matrix(w3, RW_HEAD))


_NT = (((1,), (1,)), ((), ()))
_TN = (((0,), (0,)), ((), ()))
_NN = (((1,), (0,)), ((), ()))


def _mm(a, b, dims=_NN):
    return lax.dot_general(a.astype(BF16), b.astype(BF16), dims, preferred_element_type=F32)


def _rw_chunk_kernel(lw_ref, r_ref, k_ref, v_ref, kk_ref, a_ref, tri_ref, y_ref, st_ref, *, c):
    @pl.when(pl.program_id(1) == 0)
    def _():
        st_ref[...] = jnp.zeros_like(st_ref)

    lanes = 2 * RW_HEAD
    tri = tri_ref[...]
    row = lax.broadcasted_iota(jnp.int32, (c, c), 0)
    col = lax.broadcasted_iota(jnp.int32, (c, c), 1)
    strict = row > col
    incl = row >= col
    eye = (row == col).astype(F32)
    blk = 16
    diag_blocks = (row // blk) == (col // blk)
    lane = lax.broadcasted_iota(jnp.int32, (c, lanes), 1)
    head_mask = (lane < RW_HEAD, lane >= RW_HEAD)
    srow = lax.broadcasted_iota(jnp.int32, (lanes, lanes), 0)
    scol = lax.broadcasted_iota(jnp.int32, (lanes, lanes), 1)
    same_head = (srow // RW_HEAD) == (scol // RW_HEAD)

    def by_head(x):
        return jnp.concatenate([jnp.where(hm, x, 0.0) for hm in head_mask], axis=0)

    npair = RW_HEADS // 2
    pairs = range(npair)
    heads = range(RW_HEADS)
    cols = [pl.ds(p * lanes, lanes) for p in pairs]
    lw = [lw_ref[0, :, cols[p]] for p in pairs]
    parts = [_split3(lw[p]) for p in pairs]
    lc = [sum(jnp.dot(tri, part, preferred_element_type=F32) for part in parts[p])
          for p in pairs]
    kk = [kk_ref[0, :, cols[p]] for p in pairs]
    k = [k_ref[0, :, cols[p]] for p in pairs]
    v = [v_ref[0, :, cols[p]] for p in pairs]
    beta = [kk[p] * a_ref[0, :, cols[p]] for p in pairs]
    a_t = [-kk[p] * jnp.exp(lc[p] - lw[p]) for p in pairs]
    r_t = [r_ref[0, :, cols[p]] * jnp.exp(lc[p]) for p in pairs]
    e_neg = [jnp.exp(-lc[p]) for p in pairs]
    rhs = [jnp.concatenate([beta[p] * e_neg[p], k[p] * e_neg[p]], axis=0) for p in pairs]

    m = [_mm(jnp.concatenate([jnp.where(head_mask[h % 2], a_t[h // 2], 0.0),
                              jnp.where(head_mask[h % 2], r_t[h // 2], 0.0)], axis=0),
             rhs[h // 2], _NT) for h in heads]
    l_mat = [jnp.where(strict, m[h][0:c, 0:c], 0.0) for h in heads]
    a_ak = [jnp.where(strict, m[h][0:c, c:2 * c], 0.0) for h in heads]
    a_rb = [jnp.where(incl, m[h][c:2 * c, 0:c], 0.0) for h in heads]
    a_rk = [jnp.where(incl, m[h][c:2 * c, c:2 * c], 0.0) for h in heads]

    t_inv = [eye + jnp.where(((row // 2) == (col // 2)) & (row % 2 == 1) & (col % 2 == 0),
                             l_mat[h], 0.0) for h in heads]
    size = 2
    while size < c:
        sel = (((row // (2 * size)) == (col // (2 * size)))
               & ((row // size) % 2 == 1) & ((col // size) % 2 == 0))
        po = [_mm(t_inv[h], jnp.where(sel, l_mat[h], 0.0)) for h in heads]
        t_inv = [t_inv[h] + _mm(po[h], t_inv[h]) for h in heads]
        size *= 2

    s0 = [st_ref[p] for p in pairs]
    xr = [_mm(jnp.concatenate([a_t[p], r_t[p]], axis=0), s0[p], _NT) for p in pairs]
    v2 = [by_head(v[p]) for p in pairs]
    x = [xr[p][0:c] + _mm(jnp.concatenate(a_ak[2 * p:2 * p + 2], axis=1), v2[p]) for p in pairs]
    u = [_mm(jnp.concatenate(t_inv[2 * p:2 * p + 2], axis=1), by_head(x[p]))
         for p in pairs]
    for p in pairs:
        y_ref[0, :, cols[p]] = xr[p][c:2 * c] + _mm(
            jnp.concatenate(a_rb[2 * p:2 * p + 2] + a_rk[2 * p:2 * p + 2], axis=1),
            jnp.concatenate([by_head(u[p]), v2[p]], axis=0))
    for p in pairs:
        lc_end = lc[p][c - 1:c, :]
        e_end = jnp.exp(lc_end - lc[p])
        upd = _mm(jnp.concatenate([u[p], v[p]], axis=0),
                  jnp.concatenate([beta[p] * e_end, k[p] * e_end], axis=0), _TN)
        st_ref[p] = jnp.where(same_head, s0[p] * jnp.exp(lc_end) + upd, 0.0)


def _rwkv_chunked(lw, r, k, v, kk, a, b, s):
    c = _pick(s, 128)
    w3 = RW_WIDTH
    r3 = lambda x: x.reshape(b, s, w3)
    spec = pl.BlockSpec((1, c, w3), lambda bi, i: (bi, i, 0))
    tri = (jnp.arange(c)[:, None] >= jnp.arange(c)[None, :]).astype(BF16)
    return pl.pallas_call(
        functools.partial(_rw_chunk_kernel, c=c),
        grid=(b, s // c),
        in_specs=[spec] * 6 + [pl.BlockSpec((c, c), lambda bi, i: (0, 0))],
        out_specs=spec,
        out_shape=jax.ShapeDtypeStruct((b, s, w3), F32),
        scratch_shapes=[pltpu.VMEM((RW_HEADS // 2, 2 * RW_HEAD, 2 * RW_HEAD), F32)],
        compiler_params=_cparams(("parallel", "arbitrary")),
    )(r3(lw), r3(r), r3(k), r3(v), r3(kk), r3(a), tri).reshape(b * s, w3)


def _merge_kernel(x_ref, hg_ref, da_ref, y_ref, v_ref, c3_ref, g_ref, g0_ref, g1_ref, g2_ref,
                  bg_ref, lng_ref, lnb_ref, hs_ref, wb_ref, wo_ref, o_ref):
    hs = hs_ref[...]
    y = y_ref[...]
    mu = _gdot(y, hs) * (1.0 / RW_HEAD)
    yc = y - mu
    var = _gdot(yc * yc, hs) * (1.0 / RW_HEAD)
    yn = yc * lax.rsqrt(var + RW_GN_EPS) * lng_ref[...] + lnb_ref[...]
    o_rw = (yn + c3_ref[...] * v_ref[...]) * g_ref[...]

    merged = None
    for n, (br, gp) in enumerate(((hg_ref[...], g0_ref), (da_ref[...], g1_ref), (o_rw, g2_ref))):
        gate = jax.nn.sigmoid(gp[...] + bg_ref[:, n * D_MODEL:(n + 1) * D_MODEL])
        pb = jnp.dot(br.astype(BF16), wb_ref[n], preferred_element_type=F32)
        merged = gate * pb if merged is None else merged + gate * pb
    o_ref[...] = x_ref[...] + jnp.dot(merged.astype(BF16), wo_ref[...],
                                      preferred_element_type=F32)


def _merge(x, o_hg, o_da, y, v, c3, g, proj, b_gate, ln_g, ln_b, w_branch, w_out):
    t = x.shape[0]
    tm = _pick(t, 256)
    bw = BRANCH_WIDTH
    row = pl.BlockSpec((tm, bw), lambda i: (i, 0))
    g0 = P_OFF_GATE // D_MODEL
    gate_spec = lambda n: pl.BlockSpec((tm, D_MODEL), lambda i, n=n: (i, g0 + n))
    const = lambda shape: pl.BlockSpec(shape, lambda i: (0,) * len(shape))
    return pl.pallas_call(
        _merge_kernel,
        grid=(t // tm,),
        in_specs=[pl.BlockSpec((tm, D_MODEL), lambda i: (i, 0))] + [row] * 6
        + [gate_spec(0), gate_spec(1), gate_spec(2),
           const((1, N_BRANCH * D_MODEL)), const((1, bw)), const((1, bw)), const((bw, bw)),
           const((N_BRANCH, bw, D_MODEL)), const((D_MODEL, D_MODEL))],
        out_specs=pl.BlockSpec((tm, D_MODEL), lambda i: (i, 0)),
        out_shape=jax.ShapeDtypeStruct((t, D_MODEL), F32),
        compiler_params=_cparams(("parallel",)),
    )(x, o_hg, o_da, y, v, c3, g, proj, proj, proj, b_gate.reshape(1, -1),
      ln_g.reshape(1, bw), ln_b.reshape(1, bw), _head_sum_matrix(bw, RW_HEAD),
      w_branch.astype(BF16), w_out.astype(BF16))


def _xattn_kernel(x_ref, g_ref, wq_ref, kv_ref, wo_ref, o_ref):
    x = x_ref[...]
    h = _rms(x, g_ref[...]).astype(BF16)
    q = jnp.dot(h, wq_ref[...], preferred_element_type=F32) * (XA_HEAD ** -0.5)
    outs = []
    for hd in range(XA_HEADS):
        qh = q[:, hd * XA_HEAD:(hd + 1) * XA_HEAD].astype(BF16)
        kh = kv_ref[0, :, hd * XA_HEAD:(hd + 1) * XA_HEAD].astype(BF16)
        vh = kv_ref[0, :, D_MODEL + hd * XA_HEAD:D_MODEL + (hd + 1) * XA_HEAD].astype(BF16)
        s = lax.dot_general(qh, kh, (((1,), (1,)), ((), ())), preferred_element_type=F32)
        m = jnp.max(s, axis=-1, keepdims=True)
        p = jnp.exp(s - m)
        p = p / jnp.sum(p, axis=-1, keepdims=True)
        outs.append(jnp.dot(p.astype(BF16), vh, preferred_element_type=F32))
    o = jnp.concatenate(outs, axis=-1).astype(BF16)
    o_ref[...] = x + jnp.dot(o, wo_ref[...], preferred_element_type=F32)


def _cross_attention(x, g, wq, kv, wo, b, s):
    tm = _pick(s, 256)
    nt = s // tm
    m = kv.shape[1]
    return pl.pallas_call(
        _xattn_kernel,
        grid=(b, nt),
        in_specs=[
            pl.BlockSpec((tm, D_MODEL), lambda bi, i: (bi * nt + i, 0)),
            pl.BlockSpec((1, D_MODEL), lambda bi, i: (0, 0)),
            pl.BlockSpec((D_MODEL, D_MODEL), lambda bi, i: (0, 0)),
            pl.BlockSpec((1, m, 2 * D_MODEL), lambda bi, i: (bi, 0, 0)),
            pl.BlockSpec((D_MODEL, D_MODEL), lambda bi, i: (0, 0)),
        ],
        out_specs=pl.BlockSpec((tm, D_MODEL), lambda bi, i: (bi * nt + i, 0)),
        out_shape=jax.ShapeDtypeStruct((b * s, D_MODEL), F32),
        compiler_params=_cparams(("parallel", "parallel")),
    )(x, g.reshape(1, D_MODEL), wq.astype(BF16), kv, wo.astype(BF16))


def _ffn_kernel(x_ref, xp_ref, g_ref, wu_ref, wv_ref, cw_ref, cb_ref, wd_ref, o_ref,
                h_s, hp_s, acc_s, *, tiles_per_seq):
    j = pl.program_id(1)
    tm = x_ref.shape[0]

    @pl.when(j == 0)
    def _():
        h_s[...] = _rms(x_ref[...], g_ref[...]).astype(BF16)
        first = (pl.program_id(0) % tiles_per_seq) == 0
        hp = _rms(xp_ref[...], g_ref[...])
        hp_s[...] = jnp.where(first, 0.0, hp).astype(BF16)
        acc_s[...] = jnp.zeros_like(acc_s)

    wu = wu_ref[...]
    u = jnp.dot(h_s[...], wu, preferred_element_type=F32)
    up = jnp.dot(hp_s[...], wu, preferred_element_type=F32)
    vv = jnp.dot(h_s[...], wv_ref[...], preferred_element_type=F32)
    row = lax.broadcasted_iota(jnp.int32, u.shape, 0)
    p1 = up[SUBLANES - 1:SUBLANES, :]
    p2 = up[SUBLANES - 2:SUBLANES - 1, :]
    u1 = jnp.where(row == 0, p1, pltpu.roll(u, 1, axis=0))
    u2 = jnp.where(row == 0, p2, jnp.where(row == 1, p1, pltpu.roll(u, 2, axis=0)))
    cw = cw_ref[...]
    uc = cw[0:1, :] * u2 + cw[1:2, :] * u1 + cw[2:3, :] * u + cb_ref[...]
    act = (_silu(uc) * vv).astype(BF16)
    acc_s[...] += jnp.dot(act, wd_ref[...], preferred_element_type=F32)

    @pl.when(j == pl.num_programs(1) - 1)
    def _():
        o_ref[...] = x_ref[...] + acc_s[...]


def _ffn(x, g, w_up, conv_w, conv_b, w_down, b, s):
    t = x.shape[0]
    tm = _pick(s, 512)
    tiles_per_seq = s // tm
    tf = 256
    nf = D_FF // tf
    nb8 = tm // SUBLANES
    wu = w_up.astype(BF16)
    return pl.pallas_call(
        functools.partial(_ffn_kernel, tiles_per_seq=tiles_per_seq),
        grid=(t // tm, nf),
        in_specs=[
            pl.BlockSpec((tm, D_MODEL), lambda i, j: (i, 0)),
            pl.BlockSpec((SUBLANES, D_MODEL), lambda i, j: (jnp.maximum(i * nb8 - 1, 0), 0)),
            pl.BlockSpec((1, D_MODEL), lambda i, j: (0, 0)),
            pl.BlockSpec((D_MODEL, tf), lambda i, j: (0, j)),
            pl.BlockSpec((D_MODEL, tf), lambda i, j: (0, nf + j)),
            pl.BlockSpec((CONV_W, tf), lambda i, j: (0, j)),
            pl.BlockSpec((1, tf), lambda i, j: (0, j)),
            pl.BlockSpec((tf, D_MODEL), lambda i, j: (j, 0)),
        ],
        out_specs=pl.BlockSpec((tm, D_MODEL), lambda i, j: (i, 0)),
        out_shape=jax.ShapeDtypeStruct((t, D_MODEL), F32),
        scratch_shapes=[pltpu.VMEM((tm, D_MODEL), BF16), pltpu.VMEM((SUBLANES, D_MODEL), BF16),
                        pltpu.VMEM((tm, D_MODEL), F32)],
        compiler_params=_cparams(("parallel", "arbitrary")),
    )(x, x, g.reshape(1, D_MODEL), wu, wu, conv_w, conv_b.reshape(1, D_FF),
      w_down.astype(BF16))


def _norm_kernel(x_ref, g_ref, o_ref):
    o_ref[...] = _rms(x_ref[...], g_ref[...])


def _final_norm(x, g):
    t, d = x.shape
    tm = _pick(t, 512)
    return pl.pallas_call(
        _norm_kernel,
        grid=(t // tm,),
        in_specs=[pl.BlockSpec((tm, d), lambda i: (i, 0)), pl.BlockSpec((1, d), lambda i: (0, 0))],
        out_specs=pl.BlockSpec((tm, d), lambda i: (i, 0)),
        out_shape=jax.ShapeDtypeStruct((t, d), F32),
        compiler_params=_cparams(("parallel",)),
    )(x, g.reshape(1, d))


def _permute_w_in(w):
    d = w.shape[0]
    return jnp.concatenate([
        w[:, R_OFF_RW:R_OFF_GATE], jnp.zeros((d, RW_PAD - RW_COLS), w.dtype),
        w[:, 0:R_OFF_DA], w[:, R_OFF_GATE:], w[:, R_OFF_DA:R_OFF_RW]], axis=1).astype(BF16)


def kernel(x, mem, norm_mix_g, w_in, b_gate, hgrn_lb_param, hgrn_norm_g, diff_lambda, diff_subln_g, rwkv_mu, rwkv_w0, rwkv_w_up, rwkv_a0, rwkv_a_up, rwkv_g_up, rwkv_k_k, rwkv_k_a, rwkv_r_k, rwkv_ln_g, rwkv_ln_b, w_branch, w_out, norm_xa_g, norm_mem_g, xa_wq, xa_wkv, xa_wo, norm_ffn_g, ffn_w_up, ffn_conv_w, ffn_conv_b, ffn_w_down, final_norm_g):
    b, s, d = x.shape
    depth = w_in.shape[0]
    m = mem.shape[1]
    t = b * s
    lb_p = jax.nn.softmax(hgrn_lb_param.astype(F32), axis=0)
    lower_bounds = jnp.cumsum(lb_p, axis=0) - lb_p[0]

    xf = x.reshape(t, d)
    memf = mem.reshape(b * m, d)
    for l in range(depth):
        proj = _norm_matmul(xf, norm_mix_g[l], _permute_w_in(w_in[l]), 512, 512)
        o_hg = _hgrn2(proj, lower_bounds[l], hgrn_norm_g[l], b, s)
        lambda_init = 0.8 - 0.6 * math.exp(-0.3 * l)
        o_da = _diff_attention(proj.reshape(b, s, N_PROJ), diff_lambda[l], diff_subln_g[l],
                               lambda_init, b, s).reshape(t, DA_WIDTH)
        rw_params = dict(mu=rwkv_mu[l], w0=rwkv_w0[l], w_up=rwkv_w_up[l], a0=rwkv_a0[l],
                         a_up=rwkv_a_up[l], g_up=rwkv_g_up[l], k_k=rwkv_k_k[l], k_a=rwkv_k_a[l],
                         r_k=rwkv_r_k[l].reshape(-1))
        lw, r, k, v, kk, a, c3, g = _rwkv_prep(proj, rw_params, b, s)
        y = _rwkv_chunked(lw, r, k, v, kk, a, b, s)
        xf = _merge(xf, o_hg, o_da, y, v, c3, g, proj, b_gate[l], rwkv_ln_g[l], rwkv_ln_b[l],
                    w_branch[l], w_out[l])
        kv = _norm_matmul(memf, norm_mem_g[l], xa_wkv[l].astype(BF16), 256, 512)
        xf = _cross_attention(xf, norm_xa_g[l], xa_wq[l], kv.reshape(b, m, 2 * d), xa_wo[l], b, s)
        xf = _ffn(xf, norm_ffn_g[l], ffn_w_up[l], ffn_conv_w[l], ffn_conv_b[l], ffn_w_down[l], b, s)
    return _final_norm(xf, final_norm_g).reshape(b, s, d)
```

```python
import functools
import math

import jax
import jax.numpy as jnp
from jax import lax
from jax.experimental import pallas as pl
from jax.experimental.pallas import tpu as pltpu

F32 = jnp.float32
BF16 = jnp.bfloat16

D_MODEL = 1024
NORM_EPS = 1e-6
MASK_VALUE = -1e30
TINY = 1e-30

HG_HEADS = 4
HG_D = 128
HG_WIDTH = HG_HEADS * HG_D
HG_SUB = 16

DA_HEADS = 4
DA_DK = 64
DA_DV = 128
DA_WIDTH = DA_HEADS * DA_DV
ALIBI_MAX_BIAS = 8.0
LOG2E = 1.4426950408889634
DA_ROW_BLOCK = 128

RW_HEADS = 8
RW_HEAD = 64
RW_WIDTH = RW_HEADS * RW_HEAD
RW_DECAY_RANK = 64
RW_A_RANK = 64
RW_GATE_RANK = 128
RW_COLS = 3 * RW_WIDTH + RW_DECAY_RANK + RW_A_RANK + RW_GATE_RANK
RW_GN_EPS = 64e-5

N_BRANCH = 3
BRANCH_WIDTH = 512

XA_HEADS = 4
XA_HEAD = D_MODEL // XA_HEADS

D_FF = 2816
CONV_W = 3

RW_PAD = 2048
P_OFF_RW = 0
P_OFF_HG = RW_PAD
P_OFF_GATE = P_OFF_HG + 4 * HG_WIDTH
P_OFF_DA = P_OFF_GATE + N_BRANCH * D_MODEL
N_PROJ = P_OFF_DA + 3 * DA_WIDTH

R_OFF_DA = 4 * HG_WIDTH
R_OFF_RW = R_OFF_DA + 3 * DA_WIDTH
R_OFF_GATE = R_OFF_RW + RW_COLS

V7X_VMEM_LIMIT = 56 * 1024 * 1024
SUBLANES = 8


def _cparams(sem):
    return pltpu.CompilerParams(dimension_semantics=sem, vmem_limit_bytes=V7X_VMEM_LIMIT)


def _pick(n, pref):
    t = min(n, pref)
    while n % t:
        t //= 2
    return t


def _rms(x, g):
    ms = jnp.mean(x * x, axis=-1, keepdims=True)
    return x * lax.rsqrt(ms + NORM_EPS) * g


def _split3(x):
    hi = x.astype(BF16)
    r1 = x - hi.astype(F32)
    mid = r1.astype(BF16)
    lo = (r1 - mid.astype(F32)).astype(BF16)
    return hi, mid, lo


def _gdot(x, g, passes=3):
    parts = _split3(x)[:passes]
    acc = jnp.dot(parts[0], g, preferred_element_type=F32)
    for p in parts[1:]:
        acc = acc + jnp.dot(p, g, preferred_element_type=F32)
    return acc


def _silu(x):
    return x * jax.nn.sigmoid(x)


def _norm_mm_kernel(x_ref, g_ref, w_ref, o_ref, h_ref):
    @pl.when(pl.program_id(1) == 0)
    def _():
        h_ref[...] = _rms(x_ref[...], g_ref[...]).astype(BF16)

    o_ref[...] = jnp.dot(h_ref[...], w_ref[...], preferred_element_type=F32)


def _norm_matmul(x, g, w, tm, tn):
    t, d = x.shape
    n = w.shape[1]
    tm = _pick(t, tm)
    tn = _pick(n, tn)
    return pl.pallas_call(
        _norm_mm_kernel,
        grid=(t // tm, n // tn),
        in_specs=[
            pl.BlockSpec((tm, d), lambda i, j: (i, 0)),
            pl.BlockSpec((1, d), lambda i, j: (0, 0)),
            pl.BlockSpec((d, tn), lambda i, j: (0, j)),
        ],
        out_specs=pl.BlockSpec((tm, tn), lambda i, j: (i, j)),
        out_shape=jax.ShapeDtypeStruct((t, n), F32),
        scratch_shapes=[pltpu.VMEM((tm, d), BF16)],
        compiler_params=_cparams(("parallel", "arbitrary")),
    )(x, g.reshape(1, d), w)


PROJ_SEGMENTS = ((P_OFF_RW, RW_PAD, F32), (P_OFF_HG, 4 * HG_WIDTH, F32),
                 (P_OFF_GATE, N_BRANCH * D_MODEL, F32), (P_OFF_DA, 3 * DA_WIDTH, BF16))
MXU_N = 256


def _in_proj_kernel(x_ref, g_ref, w_ref, *outs):
    h = _rms(x_ref[...], g_ref[...]).astype(BF16)
    step = 2 * MXU_N
    for o_ref, (off, width, dt) in zip(outs, PROJ_SEGMENTS):
        for c0 in range(0, width, step):
            o_ref[:, c0:c0 + step] = jnp.dot(h, w_ref[:, off + c0:off + c0 + step],
                                             preferred_element_type=F32).astype(dt)


def _in_proj(x, g, w):
    t, d = x.shape
    tm = _pick(t, 256)
    return pl.pallas_call(
        _in_proj_kernel,
        grid=(t // tm,),
        in_specs=[
            pl.BlockSpec((tm, d), lambda i: (i, 0)),
            pl.BlockSpec((1, d), lambda i: (0, 0)),
            pl.BlockSpec((d, N_PROJ), lambda i: (0, 0), pipeline_mode=pl.Buffered(1)),
        ],
        out_specs=[pl.BlockSpec((tm, width), lambda i: (i, 0)) for _, width, _ in PROJ_SEGMENTS],
        out_shape=[jax.ShapeDtypeStruct((t, width), dt) for _, width, dt in PROJ_SEGMENTS],
        compiler_params=_cparams(("parallel",)),
    )(x, g.reshape(1, d), w)


def _hgrn_kernel(q_ref, f_ref, i_ref, g_ref, lb_ref, ng_ref, o_ref,
                 st_ref, qs_s, b_s, k_s, o_s, *, ts):
    @pl.when(pl.program_id(2) == 0)
    def _():
        st_ref[...] = jnp.zeros_like(st_ref)

    lb = lb_ref[...]
    fp = f_ref[...]
    f = lb + (1.0 - lb) * jax.nn.sigmoid(fp)
    glog = jnp.log(jnp.maximum(f, TINY))
    k_s[...] = (1.0 - lb) * jax.nn.sigmoid(-fp)
    qs_s[...] = _silu(q_ref[...])

    r16 = lax.broadcasted_iota(jnp.int32, (ts, HG_D), 0) % HG_SUB
    b = glog
    sh = 1
    while sh < HG_SUB:
        b = b + jnp.where(r16 >= sh, pltpu.roll(b, sh, axis=0), 0.0)
        sh *= 2
    b_s[...] = b

    rowi = lax.broadcasted_iota(jnp.int32, (HG_SUB, HG_D), 0)

    def body(ci, carry):
        base = pl.multiple_of(ci * HG_SUB, HG_SUB)
        qc = qs_s[pl.ds(base, HG_SUB), :]
        bc = b_s[pl.ds(base, HG_SUB), :]
        kc = k_s[pl.ds(base, HG_SUB), :]
        vc = i_ref[pl.ds(base, HG_SUB), :]
        st = st_ref[...]
        qe = qc * jnp.exp(bc)
        o = lax.dot_general(qe.astype(BF16), st.astype(BF16), (((1,), (1,)), ((), ())),
                            preferred_element_type=F32)
        for j in range(HG_SUB):
            bj = b_s[pl.ds(base + j, 1), :]
            kj = k_s[pl.ds(base + j, 1), :]
            vj = i_ref[pl.ds(base + j, 1), :]
            dec = jnp.exp(jnp.where(rowi >= j, bc - bj, MASK_VALUE))
            sc = jnp.sum(qc * dec * kj, axis=-1, keepdims=True)
            o = o + sc * vj
        o_s[pl.ds(base, HG_SUB), :] = o
        bl = b_s[pl.ds(base + HG_SUB - 1, 1), :]
        kd = kc * jnp.exp(bl - bc)
        upd = lax.dot_general(vc.astype(BF16), kd.astype(BF16), (((0,), (0,)), ((), ())),
                              preferred_element_type=F32)
        st_ref[...] = st * jnp.exp(bl) + upd
        return carry

    lax.fori_loop(0, ts // HG_SUB, body, 0, unroll=8)

    o = _rms(o_s[...], ng_ref[...])
    o_ref[...] = o * _silu(g_ref[...])


def _hgrn2(proj, lb, norm_g, b, s):
    ts = _pick(s, 256)
    nt = s // ts
    c0 = 0

    def col(off):
        return pl.BlockSpec((ts, HG_D), lambda bi, h, c, off=off: (bi * nt + c, c0 + off + h))

    par = pl.BlockSpec((1, HG_D), lambda bi, h, c: (0, h))
    return pl.pallas_call(
        functools.partial(_hgrn_kernel, ts=ts),
        grid=(b, HG_HEADS, nt),
        in_specs=[col(0), col(HG_HEADS), col(2 * HG_HEADS), col(3 * HG_HEADS), par, par],
        out_specs=pl.BlockSpec((ts, HG_D), lambda bi, h, c: (bi * nt + c, h)),
        out_shape=jax.ShapeDtypeStruct((b * s, HG_WIDTH), F32),
        scratch_shapes=[pltpu.VMEM((HG_D, HG_D), F32)] + [pltpu.VMEM((ts, HG_D), F32)] * 4,
        compiler_params=_cparams(("parallel", "parallel", "arbitrary")),
    )(proj, proj, proj, proj, lb.reshape(1, HG_WIDTH), norm_g.reshape(1, HG_WIDTH))


def _da_kernel(q_ref, k_ref, v_ref, sl_ref, lam_ref, sg_ref, o_ref,
               vb_s, q2_s, m_s, acc_s, *, tq, lambda_init):
    i = pl.program_id(2)
    nq = pl.num_programs(2)
    lanes = DA_DV

    @pl.when(i == 0)
    def _():
        def fill(c, carry):
            rows = pl.ds(pl.multiple_of(c * tq, tq), tq)
            vb_s[rows, 0:lanes] = v_ref[rows, :]
            vb_s[rows, lanes:2 * lanes] = jnp.ones((tq, lanes), BF16)
            return carry
        lax.fori_loop(0, nq, fill, 0)

    q = q_ref[...].astype(F32) * (DA_DK ** -0.5 * LOG2E)
    lane = lax.broadcasted_iota(jnp.int32, (tq, DA_DV), 1)
    q2_s[pl.ds(0, tq), :] = jnp.where(lane < DA_DK, q, 0.0).astype(BF16)
    q2_s[pl.ds(tq, tq), :] = jnp.where(lane >= DA_DK, q, 0.0).astype(BF16)
    m_s[...] = jnp.full_like(m_s, MASK_VALUE)
    acc_s[...] = jnp.zeros_like(acc_s)

    slope = sl_ref[0] * LOG2E
    colf = lax.broadcasted_iota(jnp.int32, (1, tq), 1).astype(F32)
    rb = min(DA_ROW_BLOCK, tq)

    def step(j, masked):
        start = pl.multiple_of(j * tq, tq)
        kb = k_ref[pl.ds(start, tq), :]
        vb = vb_s[pl.ds(start, tq), :]
        bias = slope * (colf + ((j - i) * tq).astype(F32))
        for r0 in range(0, 2 * tq, rb):
            rows = pl.ds(r0, rb)
            s = lax.dot_general(q2_s[rows, :], kb, (((1,), (1,)), ((), ())),
                                preferred_element_type=F32) + bias
            if masked:
                r = lax.broadcasted_iota(jnp.int32, (rb, tq), 0) + (r0 % tq)
                c = lax.broadcasted_iota(jnp.int32, (rb, tq), 1)
                s = jnp.where(c <= r, s, MASK_VALUE)
            m_old = m_s[rows, :]
            m_new = jnp.maximum(m_old, jnp.max(s, axis=-1, keepdims=True))
            alpha = jnp.exp2(m_old - m_new)
            p = jnp.exp2(s - pltpu.repeat(m_new, tq // lanes, axis=1))
            acc_s[rows, :] = (pltpu.repeat(alpha, 2, axis=1) * acc_s[rows, :]
                              + jnp.dot(p.astype(BF16), vb, preferred_element_type=F32))
            m_s[rows, :] = m_new

    def body(j, carry):
        step(j, False)
        return carry

    lax.fori_loop(0, i, body, 0)
    step(i, True)

    lam = lam_ref[0]
    e1 = jnp.exp(jnp.sum(lam[0:1] * lam[1:2], axis=-1, keepdims=True))
    e2 = jnp.exp(jnp.sum(lam[2:3] * lam[3:4], axis=-1, keepdims=True))
    lam_full = e1 - e2 + lambda_init
    o1 = acc_s[pl.ds(0, tq), 0:lanes] / acc_s[pl.ds(0, tq), lanes:2 * lanes]
    o2 = acc_s[pl.ds(tq, tq), 0:lanes] / acc_s[pl.ds(tq, tq), lanes:2 * lanes]
    o = o1 - lam_full * o2
    o_ref[...] = _rms(o, sg_ref[...]) * (1.0 - lambda_init)


def _diff_attention(qkv, lam, subln_g, lambda_init, b, s):
    tq = _pick(s, 512)
    nq = s // tq
    c0 = 0
    slopes = 2.0 ** (-ALIBI_MAX_BIAS * jnp.arange(1, DA_HEADS + 1, dtype=F32) / DA_HEADS)
    slopes = jnp.broadcast_to(slopes[:, None, None], (DA_HEADS, 1, tq))
    return pl.pallas_call(
        functools.partial(_da_kernel, tq=tq, lambda_init=lambda_init),
        grid=(b, DA_HEADS, nq),
        in_specs=[
            pl.BlockSpec((None, tq, DA_DV), lambda bi, h, i: (bi, i, c0 + h)),
            pl.BlockSpec((None, s, DA_DV), lambda bi, h, i: (bi, 0, c0 + DA_HEADS + h)),
            pl.BlockSpec((None, s, DA_DV), lambda bi, h, i: (bi, 0, c0 + 2 * DA_HEADS + h)),
            pl.BlockSpec((1, 1, tq), lambda bi, h, i: (h, 0, 0)),
            pl.BlockSpec((1, 4, DA_DK), lambda bi, h, i: (0, 0, 0)),
            pl.BlockSpec((1, DA_DV), lambda bi, h, i: (0, 0)),
        ],
        out_specs=pl.BlockSpec((None, tq, DA_DV), lambda bi, h, i: (bi, i, h)),
        out_shape=jax.ShapeDtypeStruct((b, s, DA_WIDTH), F32),
        scratch_shapes=[
            pltpu.VMEM((s, 2 * DA_DV), BF16),
            pltpu.VMEM((2 * tq, DA_DV), BF16),
            pltpu.VMEM((2 * tq, DA_DV), F32),
            pltpu.VMEM((2 * tq, 2 * DA_DV), F32),
        ],
        compiler_params=_cparams(("parallel", "parallel", "arbitrary")),
    )(qkv, qkv, qkv, slopes, lam.reshape(1, 4, DA_DK), subln_g.reshape(1, DA_DV))


def _rw_prep_kernel(z_ref, zp_ref, mu_ref, wa_ref, w0_ref, a0_ref, gup_ref, kk_ref, ka_ref,
                    rk_ref, hs_ref,
                    lw_o, r_o, k_o, v_o, kk_o, a_o, c3_o, g_o, *, tiles_per_seq):
    tm = z_ref.shape[0]
    z = z_ref[...]
    first = (pl.program_id(0) % tiles_per_seq) == 0
    prev = jnp.where(first, 0.0, zp_ref[SUBLANES - 1:SUBLANES, :])
    row = lax.broadcasted_iota(jnp.int32, z.shape, 0)
    zs = jnp.where(row == 0, prev, pltpu.roll(z, 1, axis=0))
    z = z + mu_ref[...] * (zs - z)

    w3 = RW_WIDTH
    r = z[:, 0:w3]
    k = z[:, w3:2 * w3]
    v = z[:, 2 * w3:3 * w3]
    wd_ad = z[:, 3 * w3:3 * w3 + 128]
    gd = z[:, 3 * w3 + 128:3 * w3 + 256]
    lane = lax.broadcasted_iota(jnp.int32, wd_ad.shape, 1)
    lowrank_in = jnp.where(lane < RW_DECAY_RANK, jnp.tanh(wd_ad), wd_ad)
    wa = jnp.dot(lowrank_in.astype(BF16), wa_ref[...], preferred_element_type=F32)
    w = w0_ref[...] + wa[:, 0:w3]
    nw = -w
    softplus = jnp.maximum(nw, 0.0) + jnp.log(1.0 + jnp.exp(-jnp.abs(nw)))
    log_decay = -jnp.exp(-softplus - 0.5)
    a = jax.nn.sigmoid(a0_ref[...] + wa[:, w3:2 * w3])
    g = jnp.dot(jax.nn.sigmoid(gd).astype(BF16), gup_ref[...], preferred_element_type=F32)

    hs = hs_ref[...]
    kk = k * kk_ref[...]
    kk = kk / jnp.maximum(jnp.sqrt(_gdot(kk * kk, hs)), 1e-12)
    kmod = k * (1.0 + (a - 1.0) * ka_ref[...])

    lw_o[...] = log_decay
    r_o[...] = r
    k_o[...] = kmod
    v_o[...] = v
    kk_o[...] = kk
    a_o[...] = a
    c3_o[...] = _gdot(r * kmod * rk_ref[...], hs)
    g_o[...] = g


def _head_sum_matrix(n, head):
    idx = jnp.arange(n) // head
    return (idx[:, None] == idx[None, :]).astype(BF16)


def _rwkv_prep(proj, p, b, s):
    t = b * s
    tm = _pick(s, 256)
    tiles_per_seq = s // tm
    w3 = RW_WIDTH
    wa = jnp.zeros((128, 2 * w3), F32)
    wa = wa.at[0:RW_DECAY_RANK, 0:w3].set(p["w_up"]).at[RW_DECAY_RANK:128, w3:].set(p["a_up"])
    mu = jnp.pad(p["mu"], (0, RW_PAD - RW_COLS)).reshape(1, RW_PAD)
    row = lambda a: a.reshape(1, w3)
    const = lambda shape: pl.BlockSpec(shape, lambda i: (0, 0))
    out_spec = pl.BlockSpec((tm, w3), lambda i: (i, 0))
    nb8 = tm // SUBLANES
    return pl.pallas_call(
        functools.partial(_rw_prep_kernel, tiles_per_seq=tiles_per_seq),
        grid=(t // tm,),
        in_specs=[
            pl.BlockSpec((tm, RW_PAD), lambda i: (i, 0)),
            pl.BlockSpec((SUBLANES, RW_PAD), lambda i: (jnp.maximum(i * nb8 - 1, 0), 0)),
            const((1, RW_PAD)), const((128, 2 * w3)), const((1, w3)), const((1, w3)),
            const((RW_GATE_RANK, w3)), const((1, w3)), const((1, w3)), const((1, w3)),
            const((w3, w3)),
        ],
        out_specs=[out_spec] * 8,
        out_shape=[jax.ShapeDtypeStruct((t, w3), F32)] * 8,
        compiler_params=_cparams(("parallel",)),
    )(proj, proj, mu, wa.astype(BF16), row(p["w0"]), row(p["a0"]), p["g_up"].astype(BF16),
      row(p["k_k"]), row(p["k_a"]), row(p["r_k"]), _head_sum_matrix(w3, RW_HEAD))


_NT = (((1,), (1,)), ((), ()))
_TN = (((0,), (0,)), ((), ()))
_NN = (((1,), (0,)), ((), ()))


def _mm(a, b, dims=_NN):
    return lax.dot_general(a.astype(BF16), b.astype(BF16), dims, preferred_element_type=F32)


def _rw_chunk_kernel(lw_ref, r_ref, k_ref, v_ref, kk_ref, a_ref, tri_ref, y_ref, st_ref, *, c):
    @pl.when(pl.program_id(1) == 0)
    def _():
        st_ref[...] = jnp.zeros_like(st_ref)

    lanes = 2 * RW_HEAD
    tri = tri_ref[...]
    row = lax.broadcasted_iota(jnp.int32, (c, c), 0)
    col = lax.broadcasted_iota(jnp.int32, (c, c), 1)
    strict = row > col
    incl = row >= col
    eye = (row == col).astype(F32)
    blk = 16
    diag_blocks = (row // blk) == (col // blk)
    lane = lax.broadcasted_iota(jnp.int32, (c, lanes), 1)
    head_mask = (lane < RW_HEAD, lane >= RW_HEAD)
    srow = lax.broadcasted_iota(jnp.int32, (lanes, lanes), 0)
    scol = lax.broadcasted_iota(jnp.int32, (lanes, lanes), 1)
    same_head = (srow // RW_HEAD) == (scol // RW_HEAD)

    def by_head(x):
        return jnp.concatenate([jnp.where(hm, x, 0.0) for hm in head_mask], axis=0)

    npair = RW_HEADS // 2
    pairs = range(npair)
    heads = range(RW_HEADS)
    cols = [pl.ds(p * lanes, lanes) for p in pairs]
    lw = [lw_ref[0, :, cols[p]] for p in pairs]
    parts = [_split3(lw[p]) for p in pairs]
    lc = [sum(jnp.dot(tri, part, preferred_element_type=F32) for part in parts[p])
          for p in pairs]
    kk = [kk_ref[0, :, cols[p]] for p in pairs]
    k = [k_ref[0, :, cols[p]] for p in pairs]
    v = [v_ref[0, :, cols[p]] for p in pairs]
    beta = [kk[p] * a_ref[0, :, cols[p]] for p in pairs]
    a_t = [-kk[p] * jnp.exp(lc[p] - lw[p]) for p in pairs]
    r_t = [r_ref[0, :, cols[p]] * jnp.exp(lc[p]) for p in pairs]
    e_neg = [jnp.exp(-lc[p]) for p in pairs]
    rhs = [jnp.concatenate([beta[p] * e_neg[p], k[p] * e_neg[p]], axis=0) for p in pairs]

    m = [_mm(jnp.concatenate([jnp.where(head_mask[h % 2], a_t[h // 2], 0.0),
                              jnp.where(head_mask[h % 2], r_t[h // 2], 0.0)], axis=0),
             rhs[h // 2], _NT) for h in heads]
    l_mat = [jnp.where(strict, m[h][0:c, 0:c], 0.0) for h in heads]
    a_ak = [jnp.where(strict, m[h][0:c, c:2 * c], 0.0) for h in heads]
    a_rb = [jnp.where(incl, m[h][c:2 * c, 0:c], 0.0) for h in heads]
    a_rk = [jnp.where(incl, m[h][c:2 * c, c:2 * c], 0.0) for h in heads]

    t_inv = [eye + jnp.where(((row // 2) == (col // 2)) & (row % 2 == 1) & (col % 2 == 0),
                             l_mat[h], 0.0) for h in heads]
    size = 2
    while size < c:
        sel = (((row // (2 * size)) == (col // (2 * size)))
               & ((row // size) % 2 == 1) & ((col // size) % 2 == 0))
        po = [_mm(t_inv[h], jnp.where(sel, l_mat[h], 0.0)) for h in heads]
        t_inv = [t_inv[h] + _mm(po[h], t_inv[h]) for h in heads]
        size *= 2

    s0 = [st_ref[p] for p in pairs]
    xr = [_mm(jnp.concatenate([a_t[p], r_t[p]], axis=0), s0[p], _NT) for p in pairs]
    v2 = [by_head(v[p]) for p in pairs]
    x = [xr[p][0:c] + _mm(jnp.concatenate(a_ak[2 * p:2 * p + 2], axis=1), v2[p]) for p in pairs]
    u = [_mm(jnp.concatenate(t_inv[2 * p:2 * p + 2], axis=1), by_head(x[p]))
         for p in pairs]
    for p in pairs:
        y_ref[0, :, cols[p]] = xr[p][c:2 * c] + _mm(
            jnp.concatenate(a_rb[2 * p:2 * p + 2] + a_rk[2 * p:2 * p + 2], axis=1),
            jnp.concatenate([by_head(u[p]), v2[p]], axis=0))
    for p in pairs:
        lc_end = lc[p][c - 1:c, :]
        e_end = jnp.exp(lc_end - lc[p])
        upd = _mm(jnp.concatenate([u[p], v[p]], axis=0),
                  jnp.concatenate([beta[p] * e_end, k[p] * e_end], axis=0), _TN)
        st_ref[p] = jnp.where(same_head, s0[p] * jnp.exp(lc_end) + upd, 0.0)


def _rwkv_chunked(lw, r, k, v, kk, a, b, s):
    c = _pick(s, 128)
    w3 = RW_WIDTH
    r3 = lambda x: x.reshape(b, s, w3)
    spec = pl.BlockSpec((1, c, w3), lambda bi, i: (bi, i, 0))
    tri = (jnp.arange(c)[:, None] >= jnp.arange(c)[None, :]).astype(BF16)
    return pl.pallas_call(
        functools.partial(_rw_chunk_kernel, c=c),
        grid=(b, s // c),
        in_specs=[spec] * 6 + [pl.BlockSpec((c, c), lambda bi, i: (0, 0))],
        out_specs=spec,
        out_shape=jax.ShapeDtypeStruct((b, s, w3), F32),
        scratch_shapes=[pltpu.VMEM((RW_HEADS // 2, 2 * RW_HEAD, 2 * RW_HEAD), F32)],
        compiler_params=_cparams(("parallel", "arbitrary")),
    )(r3(lw), r3(r), r3(k), r3(v), r3(kk), r3(a), tri).reshape(b * s, w3)


def _merge_kernel(x_ref, hg_ref, da_ref, y_ref, v_ref, c3_ref, g_ref, g0_ref, g1_ref, g2_ref,
                  bg_ref, lng_ref, lnb_ref, hs_ref, wb_ref, wo_ref, o_ref):
    hs = hs_ref[...]
    y = y_ref[...]
    mu = _gdot(y, hs) * (1.0 / RW_HEAD)
    yc = y - mu
    var = _gdot(yc * yc, hs) * (1.0 / RW_HEAD)
    yn = yc * lax.rsqrt(var + RW_GN_EPS) * lng_ref[...] + lnb_ref[...]
    o_rw = (yn + c3_ref[...] * v_ref[...]) * g_ref[...]

    merged = None
    for n, (br, gp) in enumerate(((hg_ref[...], g0_ref), (da_ref[...], g1_ref), (o_rw, g2_ref))):
        gate = jax.nn.sigmoid(gp[...] + bg_ref[:, n * D_MODEL:(n + 1) * D_MODEL])
        pb = jnp.dot(br.astype(BF16), wb_ref[n], preferred_element_type=F32)
        merged = gate * pb if merged is None else merged + gate * pb
    o_ref[...] = x_ref[...] + jnp.dot(merged.astype(BF16), wo_ref[...],
                                      preferred_element_type=F32)


def _merge(x, o_hg, o_da, y, v, c3, g, proj, b_gate, ln_g, ln_b, w_branch, w_out):
    t = x.shape[0]
    tm = _pick(t, 256)
    bw = BRANCH_WIDTH
    row = pl.BlockSpec((tm, bw), lambda i: (i, 0))
    g0 = 0
    gate_spec = lambda n: pl.BlockSpec((tm, D_MODEL), lambda i, n=n: (i, g0 + n))
    const = lambda shape: pl.BlockSpec(shape, lambda i: (0,) * len(shape))
    return pl.pallas_call(
        _merge_kernel,
        grid=(t // tm,),
        in_specs=[pl.BlockSpec((tm, D_MODEL), lambda i: (i, 0))] + [row] * 6
        + [gate_spec(0), gate_spec(1), gate_spec(2),
           const((1, N_BRANCH * D_MODEL)), const((1, bw)), const((1, bw)), const((bw, bw)),
           const((N_BRANCH, bw, D_MODEL)), const((D_MODEL, D_MODEL))],
        out_specs=pl.BlockSpec((tm, D_MODEL), lambda i: (i, 0)),
        out_shape=jax.ShapeDtypeStruct((t, D_MODEL), F32),
        compiler_params=_cparams(("parallel",)),
    )(x, o_hg, o_da, y, v, c3, g, proj, proj, proj, b_gate.reshape(1, -1),
      ln_g.reshape(1, bw), ln_b.reshape(1, bw), _head_sum_matrix(bw, RW_HEAD),
      w_branch.astype(BF16), w_out.astype(BF16))


def _xattn_kernel(x_ref, g_ref, wq_ref, kv_ref, wo_ref, o_ref):
    x = x_ref[...]
    h = _rms(x, g_ref[...]).astype(BF16)
    q = jnp.dot(h, wq_ref[...], preferred_element_type=F32) * (XA_HEAD ** -0.5)
    outs = []
    for hd in range(XA_HEADS):
        qh = q[:, hd * XA_HEAD:(hd + 1) * XA_HEAD].astype(BF16)
        kh = kv_ref[0, :, hd * XA_HEAD:(hd + 1) * XA_HEAD].astype(BF16)
        vh = kv_ref[0, :, D_MODEL + hd * XA_HEAD:D_MODEL + (hd + 1) * XA_HEAD].astype(BF16)
        s = lax.dot_general(qh, kh, (((1,), (1,)), ((), ())), preferred_element_type=F32)
        m = jnp.max(s, axis=-1, keepdims=True)
        p = jnp.exp(s - m)
        p = p / jnp.sum(p, axis=-1, keepdims=True)
        outs.append(jnp.dot(p.astype(BF16), vh, preferred_element_type=F32))
    o = jnp.concatenate(outs, axis=-1).astype(BF16)
    o_ref[...] = x + jnp.dot(o, wo_ref[...], preferred_element_type=F32)


def _cross_attention(x, g, wq, kv, wo, b, s):
    tm = _pick(s, 256)
    nt = s // tm
    m = kv.shape[1]
    return pl.pallas_call(
        _xattn_kernel,
        grid=(b, nt),
        in_specs=[
            pl.BlockSpec((tm, D_MODEL), lambda bi, i: (bi * nt + i, 0)),
            pl.BlockSpec((1, D_MODEL), lambda bi, i: (0, 0)),
            pl.BlockSpec((D_MODEL, D_MODEL), lambda bi, i: (0, 0)),
            pl.BlockSpec((1, m, 2 * D_MODEL), lambda bi, i: (bi, 0, 0)),
            pl.BlockSpec((D_MODEL, D_MODEL), lambda bi, i: (0, 0)),
        ],
        out_specs=pl.BlockSpec((tm, D_MODEL), lambda bi, i: (bi * nt + i, 0)),
        out_shape=jax.ShapeDtypeStruct((b * s, D_MODEL), F32),
        compiler_params=_cparams(("parallel", "parallel")),
    )(x, g.reshape(1, D_MODEL), wq.astype(BF16), kv, wo.astype(BF16))


FFN_HALO = 16
FFN_CHUNK = 256


def _ffn_kernel(x_ref, xp_ref, g_ref, wu_ref, wv_ref, cw_ref, cb_ref, wd_ref, o_ref,
                h_s, act_s, *, tiles_per_seq):
    tm = x_ref.shape[0]
    first = (pl.program_id(0) % tiles_per_seq) == 0
    hp = _rms(xp_ref[...], g_ref[...])
    h_s[0:FFN_HALO, :] = jnp.where(first, 0.0, hp).astype(BF16)
    h_s[FFN_HALO:, :] = _rms(x_ref[...], g_ref[...]).astype(BF16)
    row = lax.broadcasted_iota(jnp.int32, (tm, FFN_CHUNK), 0)
    for c0 in range(0, D_FF, FFN_CHUNK):
        cols = slice(c0, c0 + FFN_CHUNK)
        ue = jnp.dot(h_s[...], wu_ref[:, cols], preferred_element_type=F32)
        vv = jnp.dot(h_s[FFN_HALO:, :], wv_ref[:, cols], preferred_element_type=F32)
        u = ue[FFN_HALO:, :]
        p1 = ue[FFN_HALO - 1:FFN_HALO, :]
        p2 = ue[FFN_HALO - 2:FFN_HALO - 1, :]
        u1 = jnp.where(row == 0, p1, pltpu.roll(u, 1, axis=0))
        u2 = jnp.where(row == 0, p2, jnp.where(row == 1, p1, pltpu.roll(u, 2, axis=0)))
        uc = (cw_ref[0:1, cols] * u2 + cw_ref[1:2, cols] * u1 + cw_ref[2:3, cols] * u
              + cb_ref[:, cols])
        act_s[:, cols] = (_silu(uc) * vv).astype(BF16)
    o_ref[...] = x_ref[...] + jnp.dot(act_s[...], wd_ref[...], preferred_element_type=F32)


def _ffn(x, g, w_up, conv_w, conv_b, w_down, b, s):
    t = x.shape[0]
    tm = _pick(s, 512)
    tiles_per_seq = s // tm
    nbh = tm // FFN_HALO
    wu = w_up.astype(BF16)
    resident = lambda shape, idx: pl.BlockSpec(shape, lambda i: idx, pipeline_mode=pl.Buffered(1))
    return pl.pallas_call(
        functools.partial(_ffn_kernel, tiles_per_seq=tiles_per_seq),
        grid=(t // tm,),
        in_specs=[
            pl.BlockSpec((tm, D_MODEL), lambda i: (i, 0)),
            pl.BlockSpec((FFN_HALO, D_MODEL), lambda i: (jnp.maximum(i * nbh - 1, 0), 0)),
            resident((1, D_MODEL), (0, 0)),
            resident((D_MODEL, D_FF), (0, 0)),
            resident((D_MODEL, D_FF), (0, 1)),
            resident((CONV_W, D_FF), (0, 0)),
            resident((1, D_FF), (0, 0)),
            resident((D_FF, D_MODEL), (0, 0)),
        ],
        out_specs=pl.BlockSpec((tm, D_MODEL), lambda i: (i, 0)),
        out_shape=jax.ShapeDtypeStruct((t, D_MODEL), F32),
        scratch_shapes=[pltpu.VMEM((tm + FFN_HALO, D_MODEL), BF16), pltpu.VMEM((tm, D_FF), BF16)],
        compiler_params=_cparams(("parallel",)),
    )(x, x, g.reshape(1, D_MODEL), wu, wu, conv_w, conv_b.reshape(1, D_FF),
      w_down.astype(BF16))


def _norm_kernel(x_ref, g_ref, o_ref):
    o_ref[...] = _rms(x_ref[...], g_ref[...])


def _final_norm(x, g):
    t, d = x.shape
    tm = _pick(t, 512)
    return pl.pallas_call(
        _norm_kernel,
        grid=(t // tm,),
        in_specs=[pl.BlockSpec((tm, d), lambda i: (i, 0)), pl.BlockSpec((1, d), lambda i: (0, 0))],
        out_specs=pl.BlockSpec((tm, d), lambda i: (i, 0)),
        out_shape=jax.ShapeDtypeStruct((t, d), F32),
        compiler_params=_cparams(("parallel",)),
    )(x, g.reshape(1, d))


def _permute_w_in(w):
    d = w.shape[0]
    return jnp.concatenate([
        w[:, R_OFF_RW:R_OFF_GATE], jnp.zeros((d, RW_PAD - RW_COLS), w.dtype),
        w[:, 0:R_OFF_DA], w[:, R_OFF_GATE:], w[:, R_OFF_DA:R_OFF_RW]], axis=1).astype(BF16)


def kernel(x, mem, norm_mix_g, w_in, b_gate, hgrn_lb_param, hgrn_norm_g, diff_lambda, diff_subln_g, rwkv_mu, rwkv_w0, rwkv_w_up, rwkv_a0, rwkv_a_up, rwkv_g_up, rwkv_k_k, rwkv_k_a, rwkv_r_k, rwkv_ln_g, rwkv_ln_b, w_branch, w_out, norm_xa_g, norm_mem_g, xa_wq, xa_wkv, xa_wo, norm_ffn_g, ffn_w_up, ffn_conv_w, ffn_conv_b, ffn_w_down, final_norm_g):
    b, s, d = x.shape
    depth = w_in.shape[0]
    m = mem.shape[1]
    t = b * s
    lb_p = jax.nn.softmax(hgrn_lb_param.astype(F32), axis=0)
    lower_bounds = jnp.cumsum(lb_p, axis=0) - lb_p[0]

    xf = x.reshape(t, d)
    memf = mem.reshape(b * m, d)
    for l in range(depth):
        p_rw, p_hg, p_gate, p_da = _in_proj(xf, norm_mix_g[l], _permute_w_in(w_in[l]))
        o_hg = _hgrn2(p_hg, lower_bounds[l], hgrn_norm_g[l], b, s)
        lambda_init = 0.8 - 0.6 * math.exp(-0.3 * l)
        o_da = _diff_attention(p_da.reshape(b, s, 3 * DA_WIDTH), diff_lambda[l], diff_subln_g[l],
                               lambda_init, b, s).reshape(t, DA_WIDTH)
        rw_params = dict(mu=rwkv_mu[l], w0=rwkv_w0[l], w_up=rwkv_w_up[l], a0=rwkv_a0[l],
                         a_up=rwkv_a_up[l], g_up=rwkv_g_up[l], k_k=rwkv_k_k[l], k_a=rwkv_k_a[l],
                         r_k=rwkv_r_k[l].reshape(-1))
        lw, r, k, v, kk, a, c3, g = _rwkv_prep(p_rw, rw_params, b, s)
        y = _rwkv_chunked(lw, r, k, v, kk, a, b, s)
        xf = _merge(xf, o_hg, o_da, y, v, c3, g, p_gate, b_gate[l], rwkv_ln_g[l], rwkv_ln_b[l],
                    w_branch[l], w_out[l])
        kv = _norm_matmul(memf, norm_mem_g[l], xa_wkv[l].astype(BF16), 256, 512)
        xf = _cross_attention(xf, norm_xa_g[l], xa_wq[l], kv.reshape(b, m, 2 * d), xa_wo[l], b, s)
        xf = _ffn(xf, norm_ffn_g[l], ffn_w_up[l], ffn_conv_w[l], ffn_conv_b[l], ffn_w_down[l], b, s)
    return _final_norm(xf, final_norm_g).reshape(b, s, d)
```

```python
import functools
import math

import jax
import jax.numpy as jnp
from jax import lax
from jax.experimental import pallas as pl
from jax.experimental.pallas import tpu as pltpu

F32 = jnp.float32
BF16 = jnp.bfloat16

D_MODEL = 1024
NORM_EPS = 1e-6
MASK_VALUE = -1e30
TINY = 1e-30

HG_HEADS = 4
HG_D = 128
HG_WIDTH = HG_HEADS * HG_D
HG_SUB = 16

DA_HEADS = 4
DA_DK = 64
DA_DV = 128
DA_WIDTH = DA_HEADS * DA_DV
ALIBI_MAX_BIAS = 8.0
LOG2E = 1.4426950408889634
DA_ROW_BLOCK = 128
DA_UNROLL = 4

RW_HEADS = 8
RW_HEAD = 64
RW_WIDTH = RW_HEADS * RW_HEAD
RW_DECAY_RANK = 64
RW_A_RANK = 64
RW_GATE_RANK = 128
RW_COLS = 3 * RW_WIDTH + RW_DECAY_RANK + RW_A_RANK + RW_GATE_RANK
RW_GN_EPS = 64e-5

N_BRANCH = 3
BRANCH_WIDTH = 512

XA_HEADS = 4
XA_HEAD = D_MODEL // XA_HEADS

D_FF = 2816
CONV_W = 3

RW_PAD = 2048
P_OFF_RW = 0
P_OFF_HG = RW_PAD
P_OFF_GATE = P_OFF_HG + 4 * HG_WIDTH
P_OFF_DA = P_OFF_GATE + N_BRANCH * D_MODEL
N_PROJ = P_OFF_DA + 3 * DA_WIDTH

R_OFF_DA = 4 * HG_WIDTH
R_OFF_RW = R_OFF_DA + 3 * DA_WIDTH
R_OFF_GATE = R_OFF_RW + RW_COLS

V7X_VMEM_LIMIT = 56 * 1024 * 1024
SUBLANES = 8


def _cparams(sem):
    return pltpu.CompilerParams(dimension_semantics=sem, vmem_limit_bytes=V7X_VMEM_LIMIT)


def _pick(n, pref):
    t = min(n, pref)
    while n % t:
        t //= 2
    return t


def _rms(x, g):
    ms = jnp.mean(x * x, axis=-1, keepdims=True)
    return x * lax.rsqrt(ms + NORM_EPS) * g


def _split3(x):
    hi = x.astype(BF16)
    r1 = x - hi.astype(F32)
    mid = r1.astype(BF16)
    lo = (r1 - mid.astype(F32)).astype(BF16)
    return hi, mid, lo


def _gdot(x, g, passes=3):
    parts = _split3(x)[:passes]
    acc = jnp.dot(parts[0], g, preferred_element_type=F32)
    for p in parts[1:]:
        acc = acc + jnp.dot(p, g, preferred_element_type=F32)
    return acc


def _silu(x):
    return x * jax.nn.sigmoid(x)


def _norm_mm_kernel(x_ref, g_ref, w_ref, o_ref, h_ref):
    @pl.when(pl.program_id(1) == 0)
    def _():
        h_ref[...] = _rms(x_ref[...], g_ref[...]).astype(BF16)

    o_ref[...] = jnp.dot(h_ref[...], w_ref[...], preferred_element_type=F32)


def _norm_matmul(x, g, w, tm, tn):
    t, d = x.shape
    n = w.shape[1]
    tm = _pick(t, tm)
    tn = _pick(n, tn)
    return pl.pallas_call(
        _norm_mm_kernel,
        grid=(t // tm, n // tn),
        in_specs=[
            pl.BlockSpec((tm, d), lambda i, j: (i, 0)),
            pl.BlockSpec((1, d), lambda i, j: (0, 0)),
            pl.BlockSpec((d, tn), lambda i, j: (0, j)),
        ],
        out_specs=pl.BlockSpec((tm, tn), lambda i, j: (i, j)),
        out_shape=jax.ShapeDtypeStruct((t, n), F32),
        scratch_shapes=[pltpu.VMEM((tm, d), BF16)],
        compiler_params=_cparams(("parallel", "arbitrary")),
    )(x, g.reshape(1, d), w)


PROJ_SEGMENTS = ((P_OFF_RW, RW_PAD, F32), (P_OFF_HG, 4 * HG_WIDTH, F32),
                 (P_OFF_GATE, N_BRANCH * D_MODEL, F32), (P_OFF_DA, 3 * DA_WIDTH, BF16))
MXU_N = 256


def _in_proj_kernel(x_ref, g_ref, w_ref, *outs):
    h = _rms(x_ref[...], g_ref[...]).astype(BF16)
    step = 2 * MXU_N
    for o_ref, (off, width, dt) in zip(outs, PROJ_SEGMENTS):
        for c0 in range(0, width, step):
            o_ref[:, c0:c0 + step] = jnp.dot(h, w_ref[:, off + c0:off + c0 + step],
                                             preferred_element_type=F32).astype(dt)


def _in_proj(x, g, w):
    t, d = x.shape
    tm = _pick(t, 256)
    return pl.pallas_call(
        _in_proj_kernel,
        grid=(t // tm,),
        in_specs=[
            pl.BlockSpec((tm, d), lambda i: (i, 0)),
            pl.BlockSpec((1, d), lambda i: (0, 0)),
            pl.BlockSpec((d, N_PROJ), lambda i: (0, 0), pipeline_mode=pl.Buffered(1)),
        ],
        out_specs=[pl.BlockSpec((tm, width), lambda i: (i, 0)) for _, width, _ in PROJ_SEGMENTS],
        out_shape=[jax.ShapeDtypeStruct((t, width), dt) for _, width, dt in PROJ_SEGMENTS],
        compiler_params=_cparams(("parallel",)),
    )(x, g.reshape(1, d), w)


def _hgrn_kernel(q_ref, f_ref, i_ref, g_ref, lb_ref, ng_ref, o_ref,
                 st_ref, qs_s, b_s, k_s, o_s, *, ts):
    @pl.when(pl.program_id(2) == 0)
    def _():
        st_ref[...] = jnp.zeros_like(st_ref)

    lb = lb_ref[...]
    fp = f_ref[...]
    f = lb + (1.0 - lb) * jax.nn.sigmoid(fp)
    glog = jnp.log(jnp.maximum(f, TINY))
    k_s[...] = (1.0 - lb) * jax.nn.sigmoid(-fp)
    qs_s[...] = _silu(q_ref[...])

    r16 = lax.broadcasted_iota(jnp.int32, (ts, HG_D), 0) % HG_SUB
    b = glog
    sh = 1
    while sh < HG_SUB:
        b = b + jnp.where(r16 >= sh, pltpu.roll(b, sh, axis=0), 0.0)
        sh *= 2
    b_s[...] = b * LOG2E

    rowi = lax.broadcasted_iota(jnp.int32, (HG_SUB, HG_D), 0)

    def body(ci, carry):
        base = pl.multiple_of(ci * HG_SUB, HG_SUB)
        qc = qs_s[pl.ds(base, HG_SUB), :]
        bc = b_s[pl.ds(base, HG_SUB), :]
        kc = k_s[pl.ds(base, HG_SUB), :]
        vc = i_ref[pl.ds(base, HG_SUB), :]
        st = st_ref[...]
        qe = qc * jnp.exp2(bc)
        o = lax.dot_general(qe.astype(BF16), st.astype(BF16), (((1,), (1,)), ((), ())),
                            preferred_element_type=F32)
        for j in range(HG_SUB):
            bj = b_s[pl.ds(base + j, 1), :]
            kj = k_s[pl.ds(base + j, 1), :]
            vj = i_ref[pl.ds(base + j, 1), :]
            dec = jnp.exp2(jnp.minimum(bc - bj, 0.0))
            sc = jnp.where(rowi >= j, jnp.sum(qc * dec * kj, axis=-1, keepdims=True), 0.0)
            o = o + sc * vj
        o_s[pl.ds(base, HG_SUB), :] = o
        bl = b_s[pl.ds(base + HG_SUB - 1, 1), :]
        kd = kc * jnp.exp2(bl - bc)
        upd = lax.dot_general(vc.astype(BF16), kd.astype(BF16), (((0,), (0,)), ((), ())),
                              preferred_element_type=F32)
        st_ref[...] = st * jnp.exp2(bl) + upd
        return carry

    lax.fori_loop(0, ts // HG_SUB, body, 0, unroll=8)

    o = _rms(o_s[...], ng_ref[...])
    o_ref[...] = o * _silu(g_ref[...])


def _hgrn2(proj, lb, norm_g, b, s):
    ts = _pick(s, 256)
    nt = s // ts
    c0 = 0

    def col(off):
        return pl.BlockSpec((ts, HG_D), lambda bi, h, c, off=off: (bi * nt + c, c0 + off + h))

    par = pl.BlockSpec((1, HG_D), lambda bi, h, c: (0, h))
    return pl.pallas_call(
        functools.partial(_hgrn_kernel, ts=ts),
        grid=(b, HG_HEADS, nt),
        in_specs=[col(0), col(HG_HEADS), col(2 * HG_HEADS), col(3 * HG_HEADS), par, par],
        out_specs=pl.BlockSpec((ts, HG_D), lambda bi, h, c: (bi * nt + c, h)),
        out_shape=jax.ShapeDtypeStruct((b * s, HG_WIDTH), F32),
        scratch_shapes=[pltpu.VMEM((HG_D, HG_D), F32)] + [pltpu.VMEM((ts, HG_D), F32)] * 4,
        compiler_params=_cparams(("parallel", "parallel", "arbitrary")),
    )(proj, proj, proj, proj, lb.reshape(1, HG_WIDTH), norm_g.reshape(1, HG_WIDTH))


def _da_kernel(q_ref, k_ref, v_ref, sl_ref, lam_ref, sg_ref, o_ref,
               vb_s, q2_s, m_s, acc_s, *, tq, lambda_init):
    i = pl.program_id(2)
    nq = pl.num_programs(2)
    lanes = DA_DV

    @pl.when(i == 0)
    def _():
        def fill(c, carry):
            rows = pl.ds(pl.multiple_of(c * tq, tq), tq)
            vb_s[rows, 0:lanes] = v_ref[rows, :]
            vb_s[rows, lanes:2 * lanes] = jnp.ones((tq, lanes), BF16)
            return carry
        lax.fori_loop(0, nq, fill, 0)

    q = q_ref[...].astype(F32) * (DA_DK ** -0.5 * LOG2E)
    lane = lax.broadcasted_iota(jnp.int32, (tq, DA_DV), 1)
    q2_s[pl.ds(0, tq), :] = jnp.where(lane < DA_DK, q, 0.0).astype(BF16)
    q2_s[pl.ds(tq, tq), :] = jnp.where(lane >= DA_DK, q, 0.0).astype(BF16)
    m_s[...] = jnp.full_like(m_s, MASK_VALUE)
    acc_s[...] = jnp.zeros_like(acc_s)

    slope = sl_ref[0] * LOG2E
    colf = lax.broadcasted_iota(jnp.int32, (1, tq), 1).astype(F32)
    rb = min(DA_ROW_BLOCK, tq)

    def step(j, masked):
        start = pl.multiple_of(j * tq, tq)
        kb = k_ref[pl.ds(start, tq), :]
        vb = vb_s[pl.ds(start, tq), :]
        bias = slope * (colf + ((j - i) * tq).astype(F32))
        for r0 in range(0, 2 * tq, rb):
            rows = pl.ds(r0, rb)
            s = lax.dot_general(q2_s[rows, :], kb, (((1,), (1,)), ((), ())),
                                preferred_element_type=F32) + bias
            if masked:
                r = lax.broadcasted_iota(jnp.int32, (rb, tq), 0) + (r0 % tq)
                c = lax.broadcasted_iota(jnp.int32, (rb, tq), 1)
                s = jnp.where(c <= r, s, MASK_VALUE)
            m_old = m_s[rows, :]
            m_new = jnp.maximum(m_old, jnp.max(s, axis=-1, keepdims=True))
            alpha = jnp.exp2(m_old - m_new)
            p = jnp.exp2(s - pltpu.repeat(m_new, tq // lanes, axis=1))
            acc_s[rows, :] = (pltpu.repeat(alpha, 2, axis=1) * acc_s[rows, :]
                              + jnp.dot(p.astype(BF16), vb, preferred_element_type=F32))
            m_s[rows, :] = m_new

    def body_many(jj, carry):
        for u in range(DA_UNROLL):
            step(DA_UNROLL * jj + u, False)
        return carry

    full = i // DA_UNROLL
    lax.fori_loop(0, full, body_many, 0)
    for rem in range(DA_UNROLL):
        @pl.when(i % DA_UNROLL == rem)
        def _(rem=rem):
            for u in range(rem):
                step(full * DA_UNROLL + u, False)
            step(i, True)

    lam = lam_ref[0]
    e1 = jnp.exp(jnp.sum(lam[0:1] * lam[1:2], axis=-1, keepdims=True))
    e2 = jnp.exp(jnp.sum(lam[2:3] * lam[3:4], axis=-1, keepdims=True))
    lam_full = e1 - e2 + lambda_init
    o1 = acc_s[pl.ds(0, tq), 0:lanes] / acc_s[pl.ds(0, tq), lanes:2 * lanes]
    o2 = acc_s[pl.ds(tq, tq), 0:lanes] / acc_s[pl.ds(tq, tq), lanes:2 * lanes]
    o = o1 - lam_full * o2
    o_ref[...] = _rms(o, sg_ref[...]) * (1.0 - lambda_init)


def _diff_attention(qkv, lam, subln_g, lambda_init, b, s):
    tq = _pick(s, 512)
    nq = s // tq
    c0 = 0
    slopes = 2.0 ** (-ALIBI_MAX_BIAS * jnp.arange(1, DA_HEADS + 1, dtype=F32) / DA_HEADS)
    slopes = jnp.broadcast_to(slopes[:, None, None], (DA_HEADS, 1, tq))
    return pl.pallas_call(
        functools.partial(_da_kernel, tq=tq, lambda_init=lambda_init),
        grid=(b, DA_HEADS, nq),
        in_specs=[
            pl.BlockSpec((None, tq, DA_DV), lambda bi, h, i: (bi, i, c0 + h)),
            pl.BlockSpec((None, s, DA_DV), lambda bi, h, i: (bi, 0, c0 + DA_HEADS + h)),
            pl.BlockSpec((None, s, DA_DV), lambda bi, h, i: (bi, 0, c0 + 2 * DA_HEADS + h)),
            pl.BlockSpec((1, 1, tq), lambda bi, h, i: (h, 0, 0)),
            pl.BlockSpec((1, 4, DA_DK), lambda bi, h, i: (0, 0, 0)),
            pl.BlockSpec((1, DA_DV), lambda bi, h, i: (0, 0)),
        ],
        out_specs=pl.BlockSpec((None, tq, DA_DV), lambda bi, h, i: (bi, i, h)),
        out_shape=jax.ShapeDtypeStruct((b, s, DA_WIDTH), F32),
        scratch_shapes=[
            pltpu.VMEM((s, 2 * DA_DV), BF16),
            pltpu.VMEM((2 * tq, DA_DV), BF16),
            pltpu.VMEM((2 * tq, DA_DV), F32),
            pltpu.VMEM((2 * tq, 2 * DA_DV), F32),
        ],
        compiler_params=_cparams(("parallel", "parallel", "arbitrary")),
    )(qkv, qkv, qkv, slopes, lam.reshape(1, 4, DA_DK), subln_g.reshape(1, DA_DV))


def _rw_prep_kernel(z_ref, zp_ref, mu_ref, wa_ref, w0_ref, a0_ref, gup_ref, kk_ref, ka_ref,
                    rk_ref, hs_ref,
                    lw_o, r_o, k_o, v_o, kk_o, a_o, c3_o, g_o, *, tiles_per_seq):
    tm = z_ref.shape[0]
    z = z_ref[...]
    first = (pl.program_id(0) % tiles_per_seq) == 0
    prev = jnp.where(first, 0.0, zp_ref[SUBLANES - 1:SUBLANES, :])
    row = lax.broadcasted_iota(jnp.int32, z.shape, 0)
    zs = jnp.where(row == 0, prev, pltpu.roll(z, 1, axis=0))
    z = z + mu_ref[...] * (zs - z)

    w3 = RW_WIDTH
    r = z[:, 0:w3]
    k = z[:, w3:2 * w3]
    v = z[:, 2 * w3:3 * w3]
    wd_ad = z[:, 3 * w3:3 * w3 + 128]
    gd = z[:, 3 * w3 + 128:3 * w3 + 256]
    lane = lax.broadcasted_iota(jnp.int32, wd_ad.shape, 1)
    lowrank_in = jnp.where(lane < RW_DECAY_RANK, jnp.tanh(wd_ad), wd_ad)
    wa = jnp.dot(lowrank_in.astype(BF16), wa_ref[...], preferred_element_type=F32)
    w = w0_ref[...] + wa[:, 0:w3]
    nw = -w
    softplus = jnp.maximum(nw, 0.0) + jnp.log(1.0 + jnp.exp(-jnp.abs(nw)))
    log_decay = -jnp.exp(-softplus - 0.5)
    a = jax.nn.sigmoid(a0_ref[...] + wa[:, w3:2 * w3])
    g = jnp.dot(jax.nn.sigmoid(gd).astype(BF16), gup_ref[...], preferred_element_type=F32)

    hs = hs_ref[...]
    kk = k * kk_ref[...]
    kk = kk / jnp.maximum(jnp.sqrt(_gdot(kk * kk, hs)), 1e-12)
    kmod = k * (1.0 + (a - 1.0) * ka_ref[...])

    lw_o[...] = log_decay
    r_o[...] = r
    k_o[...] = kmod
    v_o[...] = v
    kk_o[...] = kk
    a_o[...] = a
    c3_o[...] = _gdot(r * kmod * rk_ref[...], hs)
    g_o[...] = g


def _head_sum_matrix(n, head):
    idx = jnp.arange(n) // head
    return (idx[:, None] == idx[None, :]).astype(BF16)


def _rwkv_prep(proj, p, b, s):
    t = b * s
    tm = _pick(s, 256)
    tiles_per_seq = s // tm
    w3 = RW_WIDTH
    wa = jnp.zeros((128, 2 * w3), F32)
    wa = wa.at[0:RW_DECAY_RANK, 0:w3].set(p["w_up"]).at[RW_DECAY_RANK:128, w3:].set(p["a_up"])
    mu = jnp.pad(p["mu"], (0, RW_PAD - RW_COLS)).reshape(1, RW_PAD)
    row = lambda a: a.reshape(1, w3)
    const = lambda shape: pl.BlockSpec(shape, lambda i: (0, 0))
    out_spec = pl.BlockSpec((tm, w3), lambda i: (i, 0))
    nb8 = tm // SUBLANES
    return pl.pallas_call(
        functools.partial(_rw_prep_kernel, tiles_per_seq=tiles_per_seq),
        grid=(t // tm,),
        in_specs=[
            pl.BlockSpec((tm, RW_PAD), lambda i: (i, 0)),
            pl.BlockSpec((SUBLANES, RW_PAD), lambda i: (jnp.maximum(i * nb8 - 1, 0), 0)),
            const((1, RW_PAD)), const((128, 2 * w3)), const((1, w3)), const((1, w3)),
            const((RW_GATE_RANK, w3)), const((1, w3)), const((1, w3)), const((1, w3)),
            const((w3, w3)),
        ],
        out_specs=[out_spec] * 8,
        out_shape=[jax.ShapeDtypeStruct((t, w3), F32)] * 8,
        compiler_params=_cparams(("parallel",)),
    )(proj, proj, mu, wa.astype(BF16), row(p["w0"]), row(p["a0"]), p["g_up"].astype(BF16),
      row(p["k_k"]), row(p["k_a"]), row(p["r_k"]), _head_sum_matrix(w3, RW_HEAD))


_NT = (((1,), (1,)), ((), ()))
_TN = (((0,), (0,)), ((), ()))
_NN = (((1,), (0,)), ((), ()))


def _mm(a, b, dims=_NN):
    return lax.dot_general(a.astype(BF16), b.astype(BF16), dims, preferred_element_type=F32)


def _rw_chunk_kernel(lw_ref, r_ref, k_ref, v_ref, kk_ref, a_ref, tri_ref, y_ref, st_ref, *, c):
    @pl.when(pl.program_id(1) == 0)
    def _():
        st_ref[...] = jnp.zeros_like(st_ref)

    lanes = 2 * RW_HEAD
    tri = tri_ref[...]
    row = lax.broadcasted_iota(jnp.int32, (c, c), 0)
    col = lax.broadcasted_iota(jnp.int32, (c, c), 1)
    strict = row > col
    incl = row >= col
    eye = (row == col).astype(F32)
    blk = 16
    diag_blocks = (row // blk) == (col // blk)
    lane = lax.broadcasted_iota(jnp.int32, (c, lanes), 1)
    head_mask = (lane < RW_HEAD, lane >= RW_HEAD)
    srow = lax.broadcasted_iota(jnp.int32, (lanes, lanes), 0)
    scol = lax.broadcasted_iota(jnp.int32, (lanes, lanes), 1)
    same_head = (srow // RW_HEAD) == (scol // RW_HEAD)

    def by_head(x):
        return jnp.concatenate([jnp.where(hm, x, 0.0) for hm in head_mask], axis=0)

    npair = RW_HEADS // 2
    pairs = range(npair)
    heads = range(RW_HEADS)
    cols = [pl.ds(p * lanes, lanes) for p in pairs]
    lw = [lw_ref[0, :, cols[p]] for p in pairs]
    parts = [_split3(lw[p]) for p in pairs]
    lc = [sum(jnp.dot(tri, part, preferred_element_type=F32) for part in parts[p])
          for p in pairs]
    kk = [kk_ref[0, :, cols[p]] for p in pairs]
    k = [k_ref[0, :, cols[p]] for p in pairs]
    v = [v_ref[0, :, cols[p]] for p in pairs]
    beta = [kk[p] * a_ref[0, :, cols[p]] for p in pairs]
    a_t = [-kk[p] * jnp.exp(lc[p] - lw[p]) for p in pairs]
    r_t = [r_ref[0, :, cols[p]] * jnp.exp(lc[p]) for p in pairs]
    e_neg = [jnp.exp(-lc[p]) for p in pairs]
    rhs = [jnp.concatenate([beta[p] * e_neg[p], k[p] * e_neg[p]], axis=0) for p in pairs]

    m = [_mm(jnp.concatenate([jnp.where(head_mask[h % 2], a_t[h // 2], 0.0),
                              jnp.where(head_mask[h % 2], r_t[h // 2], 0.0)], axis=0),
             rhs[h // 2], _NT) for h in heads]
    l_mat = [jnp.where(strict, m[h][0:c, 0:c], 0.0) for h in heads]
    a_ak = [jnp.where(strict, m[h][0:c, c:2 * c], 0.0) for h in heads]
    a_rb = [jnp.where(incl, m[h][c:2 * c, 0:c], 0.0) for h in heads]
    a_rk = [jnp.where(incl, m[h][c:2 * c, c:2 * c], 0.0) for h in heads]

    t_inv = [eye + jnp.where(((row // 2) == (col // 2)) & (row % 2 == 1) & (col % 2 == 0),
                             l_mat[h], 0.0) for h in heads]
    size = 2
    while size < c:
        sel = (((row // (2 * size)) == (col // (2 * size)))
               & ((row // size) % 2 == 1) & ((col // size) % 2 == 0))
        po = [_mm(t_inv[h], jnp.where(sel, l_mat[h], 0.0)) for h in heads]
        t_inv = [t_inv[h] + _mm(po[h], t_inv[h]) for h in heads]
        size *= 2

    s0 = [st_ref[p] for p in pairs]
    xr = [_mm(jnp.concatenate([a_t[p], r_t[p]], axis=0), s0[p], _NT) for p in pairs]
    v2 = [by_head(v[p]) for p in pairs]
    x = [xr[p][0:c] + _mm(jnp.concatenate(a_ak[2 * p:2 * p + 2], axis=1), v2[p]) for p in pairs]
    u = [_mm(jnp.concatenate(t_inv[2 * p:2 * p + 2], axis=1), by_head(x[p]))
         for p in pairs]
    for p in pairs:
        y_ref[0, :, cols[p]] = xr[p][c:2 * c] + _mm(
            jnp.concatenate(a_rb[2 * p:2 * p + 2] + a_rk[2 * p:2 * p + 2], axis=1),
            jnp.concatenate([by_head(u[p]), v2[p]], axis=0))
    for p in pairs:
        lc_end = lc[p][c - 1:c, :]
        e_end = jnp.exp(lc_end - lc[p])
        upd = _mm(jnp.concatenate([u[p], v[p]], axis=0),
                  jnp.concatenate([beta[p] * e_end, k[p] * e_end], axis=0), _TN)
        st_ref[p] = jnp.where(same_head, s0[p] * jnp.exp(lc_end) + upd, 0.0)


def _rwkv_chunked(lw, r, k, v, kk, a, b, s):
    c = _pick(s, 128)
    w3 = RW_WIDTH
    r3 = lambda x: x.reshape(b, s, w3)
    spec = pl.BlockSpec((1, c, w3), lambda bi, i: (bi, i, 0))
    tri = (jnp.arange(c)[:, None] >= jnp.arange(c)[None, :]).astype(BF16)
    return pl.pallas_call(
        functools.partial(_rw_chunk_kernel, c=c),
        grid=(b, s // c),
        in_specs=[spec] * 6 + [pl.BlockSpec((c, c), lambda bi, i: (0, 0))],
        out_specs=spec,
        out_shape=jax.ShapeDtypeStruct((b, s, w3), F32),
        scratch_shapes=[pltpu.VMEM((RW_HEADS // 2, 2 * RW_HEAD, 2 * RW_HEAD), F32)],
        compiler_params=_cparams(("parallel", "arbitrary")),
    )(r3(lw), r3(r), r3(k), r3(v), r3(kk), r3(a), tri).reshape(b * s, w3)


def _merge_kernel(x_ref, hg_ref, da_ref, y_ref, v_ref, c3_ref, g_ref, g0_ref, g1_ref, g2_ref,
                  bg_ref, lng_ref, lnb_ref, hs_ref, wb_ref, wo_ref, o_ref):
    hs = hs_ref[...]
    y = y_ref[...]
    mu = _gdot(y, hs, passes=2) * (1.0 / RW_HEAD)
    yc = y - mu
    var = _gdot(yc * yc, hs, passes=2) * (1.0 / RW_HEAD)
    yn = yc * lax.rsqrt(var + RW_GN_EPS) * lng_ref[...] + lnb_ref[...]
    o_rw = (yn + c3_ref[...] * v_ref[...]) * g_ref[...]

    merged = None
    for n, (br, gp) in enumerate(((hg_ref[...], g0_ref), (da_ref[...], g1_ref), (o_rw, g2_ref))):
        gate = jax.nn.sigmoid(gp[...] + bg_ref[:, n * D_MODEL:(n + 1) * D_MODEL])
        pb = jnp.dot(br.astype(BF16), wb_ref[n], preferred_element_type=F32)
        merged = gate * pb if merged is None else merged + gate * pb
    o_ref[...] = x_ref[...] + jnp.dot(merged.astype(BF16), wo_ref[...],
                                      preferred_element_type=F32)


def _merge(x, o_hg, o_da, y, v, c3, g, proj, b_gate, ln_g, ln_b, w_branch, w_out):
    t = x.shape[0]
    tm = _pick(t, 256)
    bw = BRANCH_WIDTH
    row = pl.BlockSpec((tm, bw), lambda i: (i, 0))
    g0 = 0
    gate_spec = lambda n: pl.BlockSpec((tm, D_MODEL), lambda i, n=n: (i, g0 + n))
    const = lambda shape: pl.BlockSpec(shape, lambda i: (0,) * len(shape))
    return pl.pallas_call(
        _merge_kernel,
        grid=(t // tm,),
        in_specs=[pl.BlockSpec((tm, D_MODEL), lambda i: (i, 0))] + [row] * 6
        + [gate_spec(0), gate_spec(1), gate_spec(2),
           const((1, N_BRANCH * D_MODEL)), const((1, bw)), const((1, bw)), const((bw, bw)),
           const((N_BRANCH, bw, D_MODEL)), const((D_MODEL, D_MODEL))],
        out_specs=pl.BlockSpec((tm, D_MODEL), lambda i: (i, 0)),
        out_shape=jax.ShapeDtypeStruct((t, D_MODEL), F32),
        compiler_params=_cparams(("parallel",)),
    )(x, o_hg, o_da, y, v, c3, g, proj, proj, proj, b_gate.reshape(1, -1),
      ln_g.reshape(1, bw), ln_b.reshape(1, bw), _head_sum_matrix(bw, RW_HEAD),
      w_branch.astype(BF16), w_out.astype(BF16))


def _xattn_kernel(x_ref, g_ref, wq_ref, kv_ref, wo_ref, o_ref):
    x = x_ref[...]
    h = _rms(x, g_ref[...]).astype(BF16)
    q = jnp.dot(h, wq_ref[...], preferred_element_type=F32) * (XA_HEAD ** -0.5)
    outs = []
    for hd in range(XA_HEADS):
        qh = q[:, hd * XA_HEAD:(hd + 1) * XA_HEAD].astype(BF16)
        kh = kv_ref[0, :, hd * XA_HEAD:(hd + 1) * XA_HEAD].astype(BF16)
        vh = kv_ref[0, :, D_MODEL + hd * XA_HEAD:D_MODEL + (hd + 1) * XA_HEAD].astype(BF16)
        s = lax.dot_general(qh, kh, (((1,), (1,)), ((), ())), preferred_element_type=F32)
        m = jnp.max(s, axis=-1, keepdims=True)
        p = jnp.exp(s - m)
        p = p / jnp.sum(p, axis=-1, keepdims=True)
        outs.append(jnp.dot(p.astype(BF16), vh, preferred_element_type=F32))
    o = jnp.concatenate(outs, axis=-1).astype(BF16)
    o_ref[...] = x + jnp.dot(o, wo_ref[...], preferred_element_type=F32)


def _cross_attention(x, g, wq, kv, wo, b, s):
    tm = _pick(s, 256)
    nt = s // tm
    m = kv.shape[1]
    return pl.pallas_call(
        _xattn_kernel,
        grid=(b, nt),
        in_specs=[
            pl.BlockSpec((tm, D_MODEL), lambda bi, i: (bi * nt + i, 0)),
            pl.BlockSpec((1, D_MODEL), lambda bi, i: (0, 0)),
            pl.BlockSpec((D_MODEL, D_MODEL), lambda bi, i: (0, 0)),
            pl.BlockSpec((1, m, 2 * D_MODEL), lambda bi, i: (bi, 0, 0)),
            pl.BlockSpec((D_MODEL, D_MODEL), lambda bi, i: (0, 0)),
        ],
        out_specs=pl.BlockSpec((tm, D_MODEL), lambda bi, i: (bi * nt + i, 0)),
        out_shape=jax.ShapeDtypeStruct((b * s, D_MODEL), F32),
        compiler_params=_cparams(("parallel", "parallel")),
    )(x, g.reshape(1, D_MODEL), wq.astype(BF16), kv, wo.astype(BF16))


FFN_HALO = 16
FFN_CHUNK = 256


def _ffn_kernel(x_ref, xp_ref, g_ref, wu_ref, wv_ref, cw_ref, cb_ref, wd_ref, *rest,
                tiles_per_seq, out_norm):
    if out_norm:
        og_ref, o_ref, h_s, act_s = rest
    else:
        o_ref, h_s, act_s = rest
    tm = x_ref.shape[0]
    first = (pl.program_id(0) % tiles_per_seq) == 0
    hp = _rms(xp_ref[...], g_ref[...])
    h_s[0:FFN_HALO, :] = jnp.where(first, 0.0, hp).astype(BF16)
    h_s[FFN_HALO:, :] = _rms(x_ref[...], g_ref[...]).astype(BF16)
    row = lax.broadcasted_iota(jnp.int32, (tm, FFN_CHUNK), 0)
    for c0 in range(0, D_FF, FFN_CHUNK):
        cols = slice(c0, c0 + FFN_CHUNK)
        ue = jnp.dot(h_s[...], wu_ref[:, cols], preferred_element_type=F32)
        vv = jnp.dot(h_s[FFN_HALO:, :], wv_ref[:, cols], preferred_element_type=F32)
        u = ue[FFN_HALO:, :]
        p1 = ue[FFN_HALO - 1:FFN_HALO, :]
        p2 = ue[FFN_HALO - 2:FFN_HALO - 1, :]
        u1 = jnp.where(row == 0, p1, pltpu.roll(u, 1, axis=0))
        u2 = jnp.where(row == 0, p2, jnp.where(row == 1, p1, pltpu.roll(u, 2, axis=0)))
        uc = (cw_ref[0:1, cols] * u2 + cw_ref[1:2, cols] * u1 + cw_ref[2:3, cols] * u
              + cb_ref[:, cols])
        act_s[:, cols] = (_silu(uc) * vv).astype(BF16)
    out = x_ref[...] + jnp.dot(act_s[...], wd_ref[...], preferred_element_type=F32)
    o_ref[...] = _rms(out, og_ref[...]) if out_norm else out


def _ffn(x, g, w_up, conv_w, conv_b, w_down, b, s, out_gain=None):
    t = x.shape[0]
    tm = _pick(s, 512)
    tiles_per_seq = s // tm
    nbh = tm // FFN_HALO
    out_norm = out_gain is not None
    wu = w_up.astype(BF16)
    resident = lambda shape, idx: pl.BlockSpec(shape, lambda i: idx, pipeline_mode=pl.Buffered(1))
    return pl.pallas_call(
        functools.partial(_ffn_kernel, tiles_per_seq=tiles_per_seq, out_norm=out_norm),
        grid=(t // tm,),
        in_specs=[
            pl.BlockSpec((tm, D_MODEL), lambda i: (i, 0)),
            pl.BlockSpec((FFN_HALO, D_MODEL), lambda i: (jnp.maximum(i * nbh - 1, 0), 0)),
            resident((1, D_MODEL), (0, 0)),
            resident((D_MODEL, D_FF), (0, 0)),
            resident((D_MODEL, D_FF), (0, 1)),
            resident((CONV_W, D_FF), (0, 0)),
            resident((1, D_FF), (0, 0)),
            resident((D_FF, D_MODEL), (0, 0)),
        ] + ([resident((1, D_MODEL), (0, 0))] if out_norm else []),
        out_specs=pl.BlockSpec((tm, D_MODEL), lambda i: (i, 0)),
        out_shape=jax.ShapeDtypeStruct((t, D_MODEL), F32),
        scratch_shapes=[pltpu.VMEM((tm + FFN_HALO, D_MODEL), BF16), pltpu.VMEM((tm, D_FF), BF16)],
        compiler_params=_cparams(("parallel",)),
    )(x, x, g.reshape(1, D_MODEL), wu, wu, conv_w, conv_b.reshape(1, D_FF),
      w_down.astype(BF16), *([out_gain.reshape(1, D_MODEL)] if out_norm else []))


def _permute_w_in(w):
    d = w.shape[0]
    return jnp.concatenate([
        w[:, R_OFF_RW:R_OFF_GATE], jnp.zeros((d, RW_PAD - RW_COLS), w.dtype),
        w[:, 0:R_OFF_DA], w[:, R_OFF_GATE:], w[:, R_OFF_DA:R_OFF_RW]], axis=1).astype(BF16)


def kernel(x, mem, norm_mix_g, w_in, b_gate, hgrn_lb_param, hgrn_norm_g, diff_lambda, diff_subln_g, rwkv_mu, rwkv_w0, rwkv_w_up, rwkv_a0, rwkv_a_up, rwkv_g_up, rwkv_k_k, rwkv_k_a, rwkv_r_k, rwkv_ln_g, rwkv_ln_b, w_branch, w_out, norm_xa_g, norm_mem_g, xa_wq, xa_wkv, xa_wo, norm_ffn_g, ffn_w_up, ffn_conv_w, ffn_conv_b, ffn_w_down, final_norm_g):
    b, s, d = x.shape
    depth = w_in.shape[0]
    m = mem.shape[1]
    t = b * s
    lb_p = jax.nn.softmax(hgrn_lb_param.astype(F32), axis=0)
    lower_bounds = jnp.cumsum(lb_p, axis=0) - lb_p[0]

    xf = x.reshape(t, d)
    memf = mem.reshape(b * m, d)
    for l in range(depth):
        p_rw, p_hg, p_gate, p_da = _in_proj(xf, norm_mix_g[l], _permute_w_in(w_in[l]))
        o_hg = _hgrn2(p_hg, lower_bounds[l], hgrn_norm_g[l], b, s)
        lambda_init = 0.8 - 0.6 * math.exp(-0.3 * l)
        o_da = _diff_attention(p_da.reshape(b, s, 3 * DA_WIDTH), diff_lambda[l], diff_subln_g[l],
                               lambda_init, b, s).reshape(t, DA_WIDTH)
        rw_params = dict(mu=rwkv_mu[l], w0=rwkv_w0[l], w_up=rwkv_w_up[l], a0=rwkv_a0[l],
                         a_up=rwkv_a_up[l], g_up=rwkv_g_up[l], k_k=rwkv_k_k[l], k_a=rwkv_k_a[l],
                         r_k=rwkv_r_k[l].reshape(-1))
        lw, r, k, v, kk, a, c3, g = _rwkv_prep(p_rw, rw_params, b, s)
        y = _rwkv_chunked(lw, r, k, v, kk, a, b, s)
        xf = _merge(xf, o_hg, o_da, y, v, c3, g, p_gate, b_gate[l], rwkv_ln_g[l], rwkv_ln_b[l],
                    w_branch[l], w_out[l])
        kv = _norm_matmul(memf, norm_mem_g[l], xa_wkv[l].astype(BF16), 256, 512)
        xf = _cross_attention(xf, norm_xa_g[l], xa_wq[l], kv.reshape(b, m, 2 * d), xa_wo[l], b, s)
        xf = _ffn(xf, norm_ffn_g[l], ffn_w_up[l], ffn_conv_w[l], ffn_conv_b[l], ffn_w_down[l], b, s,
                  out_gain=final_norm_g if l == depth - 1 else None)
    return xf.reshape(b, s, d)
```

```python
import functools
import math

import jax
import jax.numpy as jnp
from jax import lax
from jax.experimental import pallas as pl
from jax.experimental.pallas import tpu as pltpu

F32 = jnp.float32
BF16 = jnp.bfloat16

D_MODEL = 1024
NORM_EPS = 1e-6
MASK_VALUE = -1e30
TINY = 1e-30

HG_HEADS = 4
HG_D = 128
HG_WIDTH = HG_HEADS * HG_D
HG_SUB = 16

DA_HEADS = 4
DA_DK = 64
DA_DV = 128
DA_WIDTH = DA_HEADS * DA_DV
ALIBI_MAX_BIAS = 8.0
LOG2E = 1.4426950408889634
DA_ROW_BLOCK = 128
DA_UNROLL = 4

RW_HEADS = 8
RW_HEAD = 64
RW_WIDTH = RW_HEADS * RW_HEAD
RW_DECAY_RANK = 64
RW_A_RANK = 64
RW_GATE_RANK = 128
RW_COLS = 3 * RW_WIDTH + RW_DECAY_RANK + RW_A_RANK + RW_GATE_RANK
RW_GN_EPS = 64e-5

N_BRANCH = 3
BRANCH_WIDTH = 512

XA_HEADS = 4
XA_HEAD = D_MODEL // XA_HEADS

D_FF = 2816
CONV_W = 3

RW_PAD = 2048
P_OFF_RW = 0
P_OFF_HG = RW_PAD
P_OFF_GATE = P_OFF_HG + 4 * HG_WIDTH
P_OFF_DA = P_OFF_GATE + N_BRANCH * D_MODEL
N_PROJ = P_OFF_DA + 3 * DA_WIDTH

R_OFF_DA = 4 * HG_WIDTH
R_OFF_RW = R_OFF_DA + 3 * DA_WIDTH
R_OFF_GATE = R_OFF_RW + RW_COLS

V7X_VMEM_LIMIT = 56 * 1024 * 1024
SUBLANES = 8


def _cparams(sem):
    return pltpu.CompilerParams(dimension_semantics=sem, vmem_limit_bytes=V7X_VMEM_LIMIT)


def _pick(n, pref):
    t = min(n, pref)
    while n % t:
        t //= 2
    return t


def _rms(x, g):
    ms = jnp.mean(x * x, axis=-1, keepdims=True)
    return x * lax.rsqrt(ms + NORM_EPS) * g


def _split3(x):
    hi = x.astype(BF16)
    r1 = x - hi.astype(F32)
    mid = r1.astype(BF16)
    lo = (r1 - mid.astype(F32)).astype(BF16)
    return hi, mid, lo


def _gdot(x, g):
    hi, lo, _ = _split3(x)
    return (jnp.dot(hi, g, preferred_element_type=F32)
            + jnp.dot(lo, g, preferred_element_type=F32))


def _silu(x):
    return x * jax.nn.sigmoid(x)


def _norm_mm_kernel(x_ref, g_ref, w_ref, o_ref, h_ref):
    @pl.when(pl.program_id(1) == 0)
    def _():
        h_ref[...] = _rms(x_ref[...], g_ref[...]).astype(BF16)

    o_ref[...] = jnp.dot(h_ref[...], w_ref[...], preferred_element_type=F32)


def _norm_matmul(x, g, w, tm, tn):
    t, d = x.shape
    n = w.shape[1]
    tm = _pick(t, tm)
    tn = _pick(n, tn)
    return pl.pallas_call(
        _norm_mm_kernel,
        grid=(t // tm, n // tn),
        in_specs=[
            pl.BlockSpec((tm, d), lambda i, j: (i, 0)),
            pl.BlockSpec((1, d), lambda i, j: (0, 0)),
            pl.BlockSpec((d, tn), lambda i, j: (0, j)),
        ],
        out_specs=pl.BlockSpec((tm, tn), lambda i, j: (i, j)),
        out_shape=jax.ShapeDtypeStruct((t, n), F32),
        scratch_shapes=[pltpu.VMEM((tm, d), BF16)],
        compiler_params=_cparams(("parallel", "arbitrary")),
    )(x, g.reshape(1, d), w)


PROJ_SEGMENTS = ((P_OFF_RW, RW_PAD, F32), (P_OFF_HG, 4 * HG_WIDTH, F32),
                 (P_OFF_GATE, N_BRANCH * D_MODEL, F32), (P_OFF_DA, 3 * DA_WIDTH, BF16))
MXU_N = 256


def _in_proj_kernel(x_ref, g_ref, w_ref, *outs):
    h = _rms(x_ref[...], g_ref[...]).astype(BF16)
    step = 2 * MXU_N
    for o_ref, (off, width, dt) in zip(outs, PROJ_SEGMENTS):
        for c0 in range(0, width, step):
            o_ref[:, c0:c0 + step] = jnp.dot(h, w_ref[:, off + c0:off + c0 + step],
                                             preferred_element_type=F32).astype(dt)


def _in_proj(x, g, w):
    t, d = x.shape
    tm = _pick(t, 256)
    return pl.pallas_call(
        _in_proj_kernel,
        grid=(t // tm,),
        in_specs=[
            pl.BlockSpec((tm, d), lambda i: (i, 0)),
            pl.BlockSpec((1, d), lambda i: (0, 0)),
            pl.BlockSpec((d, N_PROJ), lambda i: (0, 0), pipeline_mode=pl.Buffered(1)),
        ],
        out_specs=[pl.BlockSpec((tm, width), lambda i: (i, 0)) for _, width, _ in PROJ_SEGMENTS],
        out_shape=[jax.ShapeDtypeStruct((t, width), dt) for _, width, dt in PROJ_SEGMENTS],
        compiler_params=_cparams(("parallel",)),
    )(x, g.reshape(1, d), w)


HG_GROUP = 2


def _hgrn_kernel(q_ref, f_ref, i_ref, g_ref, lb_ref, ng_ref, o_ref,
                 st_ref, qs_s, b_s, k_s, v_s, o_s, *, ts):
    @pl.when(pl.program_id(2) == 0)
    def _():
        st_ref[...] = jnp.zeros_like(st_ref)

    hs = range(HG_GROUP)
    r16 = lax.broadcasted_iota(jnp.int32, (ts, HG_D), 0) % HG_SUB
    for hh in hs:
        cols = slice(hh * HG_D, (hh + 1) * HG_D)
        lb = lb_ref[:, cols]
        fp = f_ref[:, cols]
        f = lb + (1.0 - lb) * jax.nn.sigmoid(fp)
        k_s[hh] = (1.0 - lb) * jax.nn.sigmoid(-fp)
        qs_s[hh] = _silu(q_ref[:, cols])
        v_s[hh] = i_ref[:, cols]
        b = jnp.log(jnp.maximum(f, TINY))
        sh = 1
        while sh < HG_SUB:
            b = b + jnp.where(r16 >= sh, pltpu.roll(b, sh, axis=0), 0.0)
            sh *= 2
        b_s[hh] = b * LOG2E

    rowi = lax.broadcasted_iota(jnp.int32, (HG_SUB, HG_D), 0)

    def body(ci, carry):
        base = pl.multiple_of(ci * HG_SUB, HG_SUB)
        rows = pl.ds(base, HG_SUB)
        qc = [qs_s[hh, rows, :] for hh in hs]
        bc = [b_s[hh, rows, :] for hh in hs]
        st = [st_ref[hh] for hh in hs]
        o = [lax.dot_general((qc[hh] * jnp.exp2(bc[hh])).astype(BF16), st[hh].astype(BF16),
                             (((1,), (1,)), ((), ())), preferred_element_type=F32) for hh in hs]
        for j in range(HG_SUB):
            for hh in hs:
                bj = b_s[hh, pl.ds(base + j, 1), :]
                kj = k_s[hh, pl.ds(base + j, 1), :]
                vj = v_s[hh, pl.ds(base + j, 1), :]
                dec = jnp.exp2(jnp.minimum(bc[hh] - bj, 0.0))
                sc = jnp.where(rowi >= j,
                               jnp.sum(qc[hh] * dec * kj, axis=-1, keepdims=True), 0.0)
                o[hh] = o[hh] + sc * vj
        for hh in hs:
            o_s[hh, rows, :] = o[hh]
            bl = b_s[hh, pl.ds(base + HG_SUB - 1, 1), :]
            kd = k_s[hh, rows, :] * jnp.exp2(bl - bc[hh])
            upd = lax.dot_general(v_s[hh, rows, :].astype(BF16), kd.astype(BF16),
                                  (((0,), (0,)), ((), ())), preferred_element_type=F32)
            st_ref[hh] = st[hh] * jnp.exp2(bl) + upd
        return carry

    lax.fori_loop(0, ts // HG_SUB, body, 0, unroll=4)

    for hh in hs:
        cols = slice(hh * HG_D, (hh + 1) * HG_D)
        o = _rms(o_s[hh], ng_ref[:, cols])
        o_ref[:, cols] = o * _silu(g_ref[:, cols])


def _hgrn2(proj, lb, norm_g, b, s):
    ts = _pick(s, 256)
    nt = s // ts
    width = HG_GROUP * HG_D
    groups = HG_HEADS // HG_GROUP

    def col(section):
        return pl.BlockSpec((ts, width),
                            lambda bi, h, c, section=section: (bi * nt + c, section * groups + h))

    par = pl.BlockSpec((1, width), lambda bi, h, c: (0, h))
    return pl.pallas_call(
        functools.partial(_hgrn_kernel, ts=ts),
        grid=(b, groups, nt),
        in_specs=[col(0), col(1), col(2), col(3), par, par],
        out_specs=pl.BlockSpec((ts, width), lambda bi, h, c: (bi * nt + c, h)),
        out_shape=jax.ShapeDtypeStruct((b * s, HG_WIDTH), F32),
        scratch_shapes=[pltpu.VMEM((HG_GROUP, HG_D, HG_D), F32)]
        + [pltpu.VMEM((HG_GROUP, ts, HG_D), F32)] * 5,
        compiler_params=_cparams(("parallel", "parallel", "arbitrary")),
    )(proj, proj, proj, proj, lb.reshape(1, HG_WIDTH), norm_g.reshape(1, HG_WIDTH))


def _da_kernel(q_ref, k_ref, v_ref, sl_ref, lam_ref, sg_ref, o_ref,
               vb_s, q2_s, m_s, acc_s, *, tq, lambda_init):
    i = pl.program_id(2)
    nq = pl.num_programs(2)
    lanes = DA_DV

    @pl.when(i == 0)
    def _():
        def fill(c, carry):
            rows = pl.ds(pl.multiple_of(c * tq, tq), tq)
            vb_s[rows, 0:lanes] = v_ref[rows, :]
            vb_s[rows, lanes:2 * lanes] = jnp.ones((tq, lanes), BF16)
            return carry
        lax.fori_loop(0, nq, fill, 0)

    q = q_ref[...].astype(F32) * (DA_DK ** -0.5 * LOG2E)
    lane = lax.broadcasted_iota(jnp.int32, (tq, DA_DV), 1)
    q2_s[pl.ds(0, tq), :] = jnp.where(lane < DA_DK, q, 0.0).astype(BF16)
    q2_s[pl.ds(tq, tq), :] = jnp.where(lane >= DA_DK, q, 0.0).astype(BF16)
    m_s[...] = jnp.full_like(m_s, MASK_VALUE)
    acc_s[...] = jnp.zeros_like(acc_s)

    slope = sl_ref[0] * LOG2E
    colf = lax.broadcasted_iota(jnp.int32, (1, tq), 1).astype(F32)
    rb = min(DA_ROW_BLOCK, tq)

    def step(j, masked):
        start = pl.multiple_of(j * tq, tq)
        kb = k_ref[pl.ds(start, tq), :]
        vb = vb_s[pl.ds(start, tq), :]
        bias = slope * (colf + ((j - i) * tq).astype(F32))
        for r0 in range(0, 2 * tq, rb):
            rows = pl.ds(r0, rb)
            s = lax.dot_general(q2_s[rows, :], kb, (((1,), (1,)), ((), ())),
                                preferred_element_type=F32) + bias
            if masked:
                r = lax.broadcasted_iota(jnp.int32, (rb, tq), 0) + (r0 % tq)
                c = lax.broadcasted_iota(jnp.int32, (rb, tq), 1)
                s = jnp.where(c <= r, s, MASK_VALUE)
            m_old = m_s[rows, :]
            m_new = jnp.maximum(m_old, jnp.max(s, axis=-1, keepdims=True))
            alpha = jnp.exp2(m_old - m_new)
            p = jnp.exp2(s - pltpu.repeat(m_new, tq // lanes, axis=1))
            acc_s[rows, :] = (pltpu.repeat(alpha, 2, axis=1) * acc_s[rows, :]
                              + jnp.dot(p.astype(BF16), vb, preferred_element_type=F32))
            m_s[rows, :] = m_new

    def body_many(jj, carry):
        for u in range(DA_UNROLL):
            step(DA_UNROLL * jj + u, False)
        return carry

    full = i // DA_UNROLL
    lax.fori_loop(0, full, body_many, 0)
    for rem in range(DA_UNROLL):
        @pl.when(i % DA_UNROLL == rem)
        def _(rem=rem):
            for u in range(rem):
                step(full * DA_UNROLL + u, False)
            step(i, True)

    lam = lam_ref[0]
    e1 = jnp.exp(jnp.sum(lam[0:1] * lam[1:2], axis=-1, keepdims=True))
    e2 = jnp.exp(jnp.sum(lam[2:3] * lam[3:4], axis=-1, keepdims=True))
    lam_full = e1 - e2 + lambda_init
    o1 = acc_s[pl.ds(0, tq), 0:lanes] / acc_s[pl.ds(0, tq), lanes:2 * lanes]
    o2 = acc_s[pl.ds(tq, tq), 0:lanes] / acc_s[pl.ds(tq, tq), lanes:2 * lanes]
    o = o1 - lam_full * o2
    o_ref[...] = _rms(o, sg_ref[...]) * (1.0 - lambda_init)


def _diff_attention(qkv, lam, subln_g, lambda_init, b, s):
    tq = _pick(s, 512)
    nq = s // tq
    c0 = 0
    slopes = 2.0 ** (-ALIBI_MAX_BIAS * jnp.arange(1, DA_HEADS + 1, dtype=F32) / DA_HEADS)
    slopes = jnp.broadcast_to(slopes[:, None, None], (DA_HEADS, 1, tq))
    return pl.pallas_call(
        functools.partial(_da_kernel, tq=tq, lambda_init=lambda_init),
        grid=(b, DA_HEADS, nq),
        in_specs=[
            pl.BlockSpec((None, tq, DA_DV), lambda bi, h, i: (bi, i, c0 + h)),
            pl.BlockSpec((None, s, DA_DV), lambda bi, h, i: (bi, 0, c0 + DA_HEADS + h)),
            pl.BlockSpec((None, s, DA_DV), lambda bi, h, i: (bi, 0, c0 + 2 * DA_HEADS + h)),
            pl.BlockSpec((1, 1, tq), lambda bi, h, i: (h, 0, 0)),
            pl.BlockSpec((1, 4, DA_DK), lambda bi, h, i: (0, 0, 0)),
            pl.BlockSpec((1, DA_DV), lambda bi, h, i: (0, 0)),
        ],
        out_specs=pl.BlockSpec((None, tq, DA_DV), lambda bi, h, i: (bi, i, h)),
        out_shape=jax.ShapeDtypeStruct((b, s, DA_WIDTH), F32),
        scratch_shapes=[
            pltpu.VMEM((s, 2 * DA_DV), BF16),
            pltpu.VMEM((2 * tq, DA_DV), BF16),
            pltpu.VMEM((2 * tq, DA_DV), F32),
            pltpu.VMEM((2 * tq, 2 * DA_DV), F32),
        ],
        compiler_params=_cparams(("parallel", "parallel", "arbitrary")),
    )(qkv, qkv, qkv, slopes, lam.reshape(1, 4, DA_DK), subln_g.reshape(1, DA_DV))


def _rw_prep_kernel(z_ref, zp_ref, mu_ref, wa_ref, w0_ref, a0_ref, gup_ref, kk_ref, ka_ref,
                    rk_ref, hs_ref,
                    lw_o, r_o, k_o, v_o, kk_o, a_o, c3_o, g_o, *, tiles_per_seq):
    tm = z_ref.shape[0]
    z = z_ref[...]
    first = (pl.program_id(0) % tiles_per_seq) == 0
    prev = jnp.where(first, 0.0, zp_ref[SUBLANES - 1:SUBLANES, :])
    row = lax.broadcasted_iota(jnp.int32, z.shape, 0)
    zs = jnp.where(row == 0, prev, pltpu.roll(z, 1, axis=0))
    z = z + mu_ref[...] * (zs - z)

    w3 = RW_WIDTH
    r = z[:, 0:w3]
    k = z[:, w3:2 * w3]
    v = z[:, 2 * w3:3 * w3]
    wd_ad = z[:, 3 * w3:3 * w3 + 128]
    gd = z[:, 3 * w3 + 128:3 * w3 + 256]
    lane = lax.broadcasted_iota(jnp.int32, wd_ad.shape, 1)
    lowrank_in = jnp.where(lane < RW_DECAY_RANK, jnp.tanh(wd_ad), wd_ad)
    wa = jnp.dot(lowrank_in.astype(BF16), wa_ref[...], preferred_element_type=F32)
    w = w0_ref[...] + wa[:, 0:w3]
    nw = -w
    softplus = jnp.maximum(nw, 0.0) + jnp.log(1.0 + jnp.exp(-jnp.abs(nw)))
    log_decay = -jnp.exp(-softplus - 0.5)
    a = jax.nn.sigmoid(a0_ref[...] + wa[:, w3:2 * w3])
    g = jnp.dot(jax.nn.sigmoid(gd).astype(BF16), gup_ref[...], preferred_element_type=F32)

    hs = hs_ref[...]
    kk = k * kk_ref[...]
    kk = kk / jnp.maximum(jnp.sqrt(_gdot(kk * kk, hs)), 1e-12)
    kmod = k * (1.0 + (a - 1.0) * ka_ref[...])

    lw_o[...] = log_decay
    r_o[...] = r
    k_o[...] = kmod
    v_o[...] = v
    kk_o[...] = kk
    a_o[...] = a
    c3_o[...] = _gdot(r * kmod * rk_ref[...], hs)
    g_o[...] = g


def _head_sum_matrix(n, head):
    idx = jnp.arange(n) // head
    return (idx[:, None] == idx[None, :]).astype(BF16)


def _rwkv_prep(proj, p, b, s):
    t = b * s
    tm = _pick(s, 256)
    tiles_per_seq = s // tm
    w3 = RW_WIDTH
    wa = jnp.zeros((128, 2 * w3), F32)
    wa = wa.at[0:RW_DECAY_RANK, 0:w3].set(p["w_up"]).at[RW_DECAY_RANK:128, w3:].set(p["a_up"])
    mu = jnp.pad(p["mu"], (0, RW_PAD - RW_COLS)).reshape(1, RW_PAD)
    row = lambda a: a.reshape(1, w3)
    const = lambda shape: pl.BlockSpec(shape, lambda i: (0, 0))
    out_spec = pl.BlockSpec((tm, w3), lambda i: (i, 0))
    nb8 = tm // SUBLANES
    return pl.pallas_call(
        functools.partial(_rw_prep_kernel, tiles_per_seq=tiles_per_seq),
        grid=(t // tm,),
        in_specs=[
            pl.BlockSpec((tm, RW_PAD), lambda i: (i, 0)),
            pl.BlockSpec((SUBLANES, RW_PAD), lambda i: (jnp.maximum(i * nb8 - 1, 0), 0)),
            const((1, RW_PAD)), const((128, 2 * w3)), const((1, w3)), const((1, w3)),
            const((RW_GATE_RANK, w3)), const((1, w3)), const((1, w3)), const((1, w3)),
            const((w3, w3)),
        ],
        out_specs=[out_spec] * 8,
        out_shape=[jax.ShapeDtypeStruct((t, w3), F32)] * 8,
        compiler_params=_cparams(("parallel",)),
    )(proj, proj, mu, wa.astype(BF16), row(p["w0"]), row(p["a0"]), p["g_up"].astype(BF16),
      row(p["k_k"]), row(p["k_a"]), row(p["r_k"]), _head_sum_matrix(w3, RW_HEAD))


_NT = (((1,), (1,)), ((), ()))
_TN = (((0,), (0,)), ((), ()))
_NN = (((1,), (0,)), ((), ()))


def _mm(a, b, dims=_NN):
    return lax.dot_general(a.astype(BF16), b.astype(BF16), dims, preferred_element_type=F32)


def _rw_chunk_kernel(lw_ref, r_ref, k_ref, v_ref, kk_ref, a_ref, tri_ref, y_ref, st_ref, *, c):
    @pl.when(pl.program_id(1) == 0)
    def _():
        st_ref[...] = jnp.zeros_like(st_ref)

    lanes = 2 * RW_HEAD
    tri = tri_ref[...]
    row = lax.broadcasted_iota(jnp.int32, (c, c), 0)
    col = lax.broadcasted_iota(jnp.int32, (c, c), 1)
    strict = row > col
    incl = row >= col
    eye = (row == col).astype(F32)
    blk = 16
    diag_blocks = (row // blk) == (col // blk)
    lane = lax.broadcasted_iota(jnp.int32, (c, lanes), 1)
    head_mask = (lane < RW_HEAD, lane >= RW_HEAD)
    srow = lax.broadcasted_iota(jnp.int32, (lanes, lanes), 0)
    scol = lax.broadcasted_iota(jnp.int32, (lanes, lanes), 1)
    same_head = (srow // RW_HEAD) == (scol // RW_HEAD)

    def by_head(x):
        return jnp.concatenate([jnp.where(hm, x, 0.0) for hm in head_mask], axis=0)

    npair = RW_HEADS // 2
    pairs = range(npair)
    heads = range(RW_HEADS)
    cols = [pl.ds(p * lanes, lanes) for p in pairs]
    lw = [lw_ref[0, :, cols[p]] for p in pairs]
    parts = [_split3(lw[p]) for p in pairs]
    lc = [sum(jnp.dot(tri, part, preferred_element_type=F32) for part in parts[p])
          for p in pairs]
    kk = [kk_ref[0, :, cols[p]] for p in pairs]
    k = [k_ref[0, :, cols[p]] for p in pairs]
    v = [v_ref[0, :, cols[p]] for p in pairs]
    beta = [kk[p] * a_ref[0, :, cols[p]] for p in pairs]
    a_t = [-kk[p] * jnp.exp(lc[p] - lw[p]) for p in pairs]
    r_t = [r_ref[0, :, cols[p]] * jnp.exp(lc[p]) for p in pairs]
    e_neg = [jnp.exp(-lc[p]) for p in pairs]
    rhs = [jnp.concatenate([beta[p] * e_neg[p], k[p] * e_neg[p]], axis=0) for p in pairs]

    m = [_mm(jnp.concatenate([jnp.where(head_mask[h % 2], a_t[h // 2], 0.0),
                              jnp.where(head_mask[h % 2], r_t[h // 2], 0.0)], axis=0),
             rhs[h // 2], _NT) for h in heads]
    l_mat = [jnp.where(strict, m[h][0:c, 0:c], 0.0) for h in heads]
    a_ak = [jnp.where(strict, m[h][0:c, c:2 * c], 0.0) for h in heads]
    a_rb = [jnp.where(incl, m[h][c:2 * c, 0:c], 0.0) for h in heads]
    a_rk = [jnp.where(incl, m[h][c:2 * c, c:2 * c], 0.0) for h in heads]

    t_inv = [eye + jnp.where(((row // 2) == (col // 2)) & (row % 2 == 1) & (col % 2 == 0),
                             l_mat[h], 0.0) for h in heads]
    size = 2
    while size < c:
        sel = (((row // (2 * size)) == (col // (2 * size)))
               & ((row // size) % 2 == 1) & ((col // size) % 2 == 0))
        po = [_mm(t_inv[h], jnp.where(sel, l_mat[h], 0.0)) for h in heads]
        t_inv = [t_inv[h] + _mm(po[h], t_inv[h]) for h in heads]
        size *= 2

    s0 = [st_ref[p] for p in pairs]
    xr = [_mm(jnp.concatenate([a_t[p], r_t[p]], axis=0), s0[p], _NT) for p in pairs]
    v2 = [by_head(v[p]) for p in pairs]
    x = [xr[p][0:c] + _mm(jnp.concatenate(a_ak[2 * p:2 * p + 2], axis=1), v2[p]) for p in pairs]
    u = [_mm(jnp.concatenate(t_inv[2 * p:2 * p + 2], axis=1), by_head(x[p]))
         for p in pairs]
    for p in pairs:
        y_ref[0, :, cols[p]] = xr[p][c:2 * c] + _mm(
            jnp.concatenate(a_rb[2 * p:2 * p + 2] + a_rk[2 * p:2 * p + 2], axis=1),
            jnp.concatenate([by_head(u[p]), v2[p]], axis=0))
    for p in pairs:
        lc_end = lc[p][c - 1:c, :]
        e_end = jnp.exp(lc_end - lc[p])
        upd = _mm(jnp.concatenate([u[p], v[p]], axis=0),
                  jnp.concatenate([beta[p] * e_end, k[p] * e_end], axis=0), _TN)
        st_ref[p] = jnp.where(same_head, s0[p] * jnp.exp(lc_end) + upd, 0.0)


def _rwkv_chunked(lw, r, k, v, kk, a, b, s):
    c = _pick(s, 128)
    w3 = RW_WIDTH
    r3 = lambda x: x.reshape(b, s, w3)
    spec = pl.BlockSpec((1, c, w3), lambda bi, i: (bi, i, 0))
    tri = (jnp.arange(c)[:, None] >= jnp.arange(c)[None, :]).astype(BF16)
    return pl.pallas_call(
        functools.partial(_rw_chunk_kernel, c=c),
        grid=(b, s // c),
        in_specs=[spec] * 6 + [pl.BlockSpec((c, c), lambda bi, i: (0, 0))],
        out_specs=spec,
        out_shape=jax.ShapeDtypeStruct((b, s, w3), F32),
        scratch_shapes=[pltpu.VMEM((RW_HEADS // 2, 2 * RW_HEAD, 2 * RW_HEAD), F32)],
        compiler_params=_cparams(("parallel", "arbitrary")),
    )(r3(lw), r3(r), r3(k), r3(v), r3(kk), r3(a), tri).reshape(b * s, w3)


def _merge_kernel(x_ref, hg_ref, da_ref, y_ref, v_ref, c3_ref, g_ref, g0_ref, g1_ref, g2_ref,
                  bg_ref, lng_ref, lnb_ref, hs_ref, wb_ref, wo_ref, o_ref):
    hs = hs_ref[...]
    y = y_ref[...]
    mu = _gdot(y, hs) * (1.0 / RW_HEAD)
    yc = y - mu
    var = _gdot(yc * yc, hs) * (1.0 / RW_HEAD)
    yn = yc * lax.rsqrt(var + RW_GN_EPS) * lng_ref[...] + lnb_ref[...]
    o_rw = (yn + c3_ref[...] * v_ref[...]) * g_ref[...]

    merged = None
    for n, (br, gp) in enumerate(((hg_ref[...], g0_ref), (da_ref[...], g1_ref), (o_rw, g2_ref))):
        gate = jax.nn.sigmoid(gp[...] + bg_ref[:, n * D_MODEL:(n + 1) * D_MODEL])
        pb = jnp.dot(br.astype(BF16), wb_ref[n], preferred_element_type=F32)
        merged = gate * pb if merged is None else merged + gate * pb
    o_ref[...] = x_ref[...] + jnp.dot(merged.astype(BF16), wo_ref[...],
                                      preferred_element_type=F32)


def _merge(x, o_hg, o_da, y, v, c3, g, proj, b_gate, ln_g, ln_b, w_branch, w_out):
    t = x.shape[0]
    tm = _pick(t, 512)
    bw = BRANCH_WIDTH
    row = pl.BlockSpec((tm, bw), lambda i: (i, 0))
    g0 = 0
    gate_spec = lambda n: pl.BlockSpec((tm, D_MODEL), lambda i, n=n: (i, g0 + n))
    const = lambda shape: pl.BlockSpec(shape, lambda i: (0,) * len(shape),
                                       pipeline_mode=pl.Buffered(1))
    return pl.pallas_call(
        _merge_kernel,
        grid=(t // tm,),
        in_specs=[pl.BlockSpec((tm, D_MODEL), lambda i: (i, 0))] + [row] * 6
        + [gate_spec(0), gate_spec(1), gate_spec(2),
           const((1, N_BRANCH * D_MODEL)), const((1, bw)), const((1, bw)), const((bw, bw)),
           const((N_BRANCH, bw, D_MODEL)), const((D_MODEL, D_MODEL))],
        out_specs=pl.BlockSpec((tm, D_MODEL), lambda i: (i, 0)),
        out_shape=jax.ShapeDtypeStruct((t, D_MODEL), F32),
        compiler_params=_cparams(("parallel",)),
    )(x, o_hg, o_da, y, v, c3, g, proj, proj, proj, b_gate.reshape(1, -1),
      ln_g.reshape(1, bw), ln_b.reshape(1, bw), _head_sum_matrix(bw, RW_HEAD),
      w_branch.astype(BF16), w_out.astype(BF16))


def _xattn_kernel(x_ref, g_ref, wq_ref, kv_ref, wo_ref, o_ref):
    x = x_ref[...]
    h = _rms(x, g_ref[...]).astype(BF16)
    q = jnp.dot(h, wq_ref[...], preferred_element_type=F32) * (XA_HEAD ** -0.5)
    outs = []
    for hd in range(XA_HEADS):
        qh = q[:, hd * XA_HEAD:(hd + 1) * XA_HEAD].astype(BF16)
        kh = kv_ref[0, :, hd * XA_HEAD:(hd + 1) * XA_HEAD].astype(BF16)
        vh = kv_ref[0, :, D_MODEL + hd * XA_HEAD:D_MODEL + (hd + 1) * XA_HEAD].astype(BF16)
        s = lax.dot_general(qh, kh, (((1,), (1,)), ((), ())), preferred_element_type=F32)
        m = jnp.max(s, axis=-1, keepdims=True)
        p = jnp.exp(s - m)
        p = p / jnp.sum(p, axis=-1, keepdims=True)
        outs.append(jnp.dot(p.astype(BF16), vh, preferred_element_type=F32))
    o = jnp.concatenate(outs, axis=-1).astype(BF16)
    o_ref[...] = x + jnp.dot(o, wo_ref[...], preferred_element_type=F32)


def _cross_attention(x, g, wq, kv, wo, b, s):
    tm = _pick(s, 512)
    nt = s // tm
    m = kv.shape[1]
    return pl.pallas_call(
        _xattn_kernel,
        grid=(b, nt),
        in_specs=[
            pl.BlockSpec((tm, D_MODEL), lambda bi, i: (bi * nt + i, 0)),
            pl.BlockSpec((1, D_MODEL), lambda bi, i: (0, 0)),
            pl.BlockSpec((D_MODEL, D_MODEL), lambda bi, i: (0, 0)),
            pl.BlockSpec((1, m, 2 * D_MODEL), lambda bi, i: (bi, 0, 0)),
            pl.BlockSpec((D_MODEL, D_MODEL), lambda bi, i: (0, 0)),
        ],
        out_specs=pl.BlockSpec((tm, D_MODEL), lambda bi, i: (bi * nt + i, 0)),
        out_shape=jax.ShapeDtypeStruct((b * s, D_MODEL), F32),
        compiler_params=_cparams(("parallel", "parallel")),
    )(x, g.reshape(1, D_MODEL), wq.astype(BF16), kv, wo.astype(BF16))


FFN_HALO = 16
FFN_CHUNK = 256


def _ffn_kernel(x_ref, xp_ref, g_ref, wu_ref, wv_ref, cw_ref, cb_ref, wd_ref, *rest,
                tiles_per_seq, out_norm):
    if out_norm:
        og_ref, o_ref, h_s, act_s = rest
    else:
        o_ref, h_s, act_s = rest
    tm = x_ref.shape[0]
    first = (pl.program_id(0) % tiles_per_seq) == 0
    hp = _rms(xp_ref[...], g_ref[...])
    h_s[0:FFN_HALO, :] = jnp.where(first, 0.0, hp).astype(BF16)
    h_s[FFN_HALO:, :] = _rms(x_ref[...], g_ref[...]).astype(BF16)
    row = lax.broadcasted_iota(jnp.int32, (tm, FFN_CHUNK), 0)
    for c0 in range(0, D_FF, FFN_CHUNK):
        cols = slice(c0, c0 + FFN_CHUNK)
        ue = jnp.dot(h_s[...], wu_ref[:, cols], preferred_element_type=F32)
        vv = jnp.dot(h_s[FFN_HALO:, :], wv_ref[:, cols], preferred_element_type=F32)
        u = ue[FFN_HALO:, :]
        p1 = ue[FFN_HALO - 1:FFN_HALO, :]
        p2 = ue[FFN_HALO - 2:FFN_HALO - 1, :]
        u1 = jnp.where(row == 0, p1, pltpu.roll(u, 1, axis=0))
        u2 = jnp.where(row == 0, p2, jnp.where(row == 1, p1, pltpu.roll(u, 2, axis=0)))
        uc = (cw_ref[0:1, cols] * u2 + cw_ref[1:2, cols] * u1 + cw_ref[2:3, cols] * u
              + cb_ref[:, cols])
        act_s[:, cols] = (_silu(uc) * vv).astype(BF16)
    out = x_ref[...] + jnp.dot(act_s[...], wd_ref[...], preferred_element_type=F32)
    o_ref[...] = _rms(out, og_ref[...]) if out_norm else out


def _ffn(x, g, w_up, conv_w, conv_b, w_down, b, s, out_gain=None):
    t = x.shape[0]
    tm = _pick(s, 512)
    tiles_per_seq = s // tm
    nbh = tm // FFN_HALO
    out_norm = out_gain is not None
    wu = w_up.astype(BF16)
    resident = lambda shape, idx: pl.BlockSpec(shape, lambda i: idx, pipeline_mode=pl.Buffered(1))
    return pl.pallas_call(
        functools.partial(_ffn_kernel, tiles_per_seq=tiles_per_seq, out_norm=out_norm),
        grid=(t // tm,),
        in_specs=[
            pl.BlockSpec((tm, D_MODEL), lambda i: (i, 0)),
            pl.BlockSpec((FFN_HALO, D_MODEL), lambda i: (jnp.maximum(i * nbh - 1, 0), 0)),
            resident((1, D_MODEL), (0, 0)),
            resident((D_MODEL, D_FF), (0, 0)),
            resident((D_MODEL, D_FF), (0, 1)),
            resident((CONV_W, D_FF), (0, 0)),
            resident((1, D_FF), (0, 0)),
            resident((D_FF, D_MODEL), (0, 0)),
        ] + ([resident((1, D_MODEL), (0, 0))] if out_norm else []),
        out_specs=pl.BlockSpec((tm, D_MODEL), lambda i: (i, 0)),
        out_shape=jax.ShapeDtypeStruct((t, D_MODEL), F32),
        scratch_shapes=[pltpu.VMEM((tm + FFN_HALO, D_MODEL), BF16), pltpu.VMEM((tm, D_FF), BF16)],
        compiler_params=_cparams(("parallel",)),
    )(x, x, g.reshape(1, D_MODEL), wu, wu, conv_w, conv_b.reshape(1, D_FF),
      w_down.astype(BF16), *([out_gain.reshape(1, D_MODEL)] if out_norm else []))


def _permute_w_in(w):
    d = w.shape[0]
    return jnp.concatenate([
        w[:, R_OFF_RW:R_OFF_GATE], jnp.zeros((d, RW_PAD - RW_COLS), w.dtype),
        w[:, 0:R_OFF_DA], w[:, R_OFF_GATE:], w[:, R_OFF_DA:R_OFF_RW]], axis=1).astype(BF16)


def kernel(x, mem, norm_mix_g, w_in, b_gate, hgrn_lb_param, hgrn_norm_g, diff_lambda, diff_subln_g, rwkv_mu, rwkv_w0, rwkv_w_up, rwkv_a0, rwkv_a_up, rwkv_g_up, rwkv_k_k, rwkv_k_a, rwkv_r_k, rwkv_ln_g, rwkv_ln_b, w_branch, w_out, norm_xa_g, norm_mem_g, xa_wq, xa_wkv, xa_wo, norm_ffn_g, ffn_w_up, ffn_conv_w, ffn_conv_b, ffn_w_down, final_norm_g):
    b, s, d = x.shape
    depth = w_in.shape[0]
    m = mem.shape[1]
    t = b * s
    lb_p = jax.nn.softmax(hgrn_lb_param.astype(F32), axis=0)
    lower_bounds = jnp.cumsum(lb_p, axis=0) - lb_p[0]

    xf = x.reshape(t, d)
    memf = mem.reshape(b * m, d)
    for l in range(depth):
        p_rw, p_hg, p_gate, p_da = _in_proj(xf, norm_mix_g[l], _permute_w_in(w_in[l]))
        o_hg = _hgrn2(p_hg, lower_bounds[l], hgrn_norm_g[l], b, s)
        lambda_init = 0.8 - 0.6 * math.exp(-0.3 * l)
        o_da = _diff_attention(p_da.reshape(b, s, 3 * DA_WIDTH), diff_lambda[l], diff_subln_g[l],
                               lambda_init, b, s).reshape(t, DA_WIDTH)
        rw_params = dict(mu=rwkv_mu[l], w0=rwkv_w0[l], w_up=rwkv_w_up[l], a0=rwkv_a0[l],
                         a_up=rwkv_a_up[l], g_up=rwkv_g_up[l], k_k=rwkv_k_k[l], k_a=rwkv_k_a[l],
                         r_k=rwkv_r_k[l].reshape(-1))
        lw, r, k, v, kk, a, c3, g = _rwkv_prep(p_rw, rw_params, b, s)
        y = _rwkv_chunked(lw, r, k, v, kk, a, b, s)
        xf = _merge(xf, o_hg, o_da, y, v, c3, g, p_gate, b_gate[l], rwkv_ln_g[l], rwkv_ln_b[l],
                    w_branch[l], w_out[l])
        kv = _norm_matmul(memf, norm_mem_g[l], xa_wkv[l].astype(BF16), 256, 512)
        xf = _cross_attention(xf, norm_xa_g[l], xa_wq[l], kv.reshape(b, m, 2 * d), xa_wo[l], b, s)
        xf = _ffn(xf, norm_ffn_g[l], ffn_w_up[l], ffn_conv_w[l], ffn_conv_b[l], ffn_w_down[l], b, s,
                  out_gain=final_norm_g if l == depth - 1 else None)
    return xf.reshape(b, s, d)
```

```python
import functools
import math

import jax
import jax.numpy as jnp
from jax import lax
from jax.experimental import pallas as pl
from jax.experimental.pallas import tpu as pltpu

F32 = jnp.float32
BF16 = jnp.bfloat16

D_MODEL = 1024
NORM_EPS = 1e-6
MASK_VALUE = -1e30
TINY = 1e-30

HG_HEADS = 4
HG_D = 128
HG_WIDTH = HG_HEADS * HG_D
HG_SUB = 16

DA_HEADS = 4
DA_DK = 64
DA_DV = 128
DA_WIDTH = DA_HEADS * DA_DV
ALIBI_MAX_BIAS = 8.0
LOG2E = 1.4426950408889634
DA_ROW_BLOCK = 128
DA_UNROLL = 4

RW_HEADS = 8
RW_HEAD = 64
RW_WIDTH = RW_HEADS * RW_HEAD
RW_DECAY_RANK = 64
RW_A_RANK = 64
RW_GATE_RANK = 128
RW_COLS = 3 * RW_WIDTH + RW_DECAY_RANK + RW_A_RANK + RW_GATE_RANK
RW_GN_EPS = 64e-5

N_BRANCH = 3
BRANCH_WIDTH = 512

XA_HEADS = 4
XA_HEAD = D_MODEL // XA_HEADS

D_FF = 2816
CONV_W = 3

RW_PAD = 2048
P_OFF_RW = 0
P_OFF_HG = RW_PAD
P_OFF_GATE = P_OFF_HG + 4 * HG_WIDTH
P_OFF_DA = P_OFF_GATE + N_BRANCH * D_MODEL
N_PROJ = P_OFF_DA + 3 * DA_WIDTH

R_OFF_DA = 4 * HG_WIDTH
R_OFF_RW = R_OFF_DA + 3 * DA_WIDTH
R_OFF_GATE = R_OFF_RW + RW_COLS

V7X_VMEM_LIMIT = 56 * 1024 * 1024
SUBLANES = 8


def _cparams(sem):
    return pltpu.CompilerParams(dimension_semantics=sem, vmem_limit_bytes=V7X_VMEM_LIMIT)


def _pick(n, pref):
    t = min(n, pref)
    while n % t:
        t //= 2
    return t


def _rms(x, g):
    ms = jnp.mean(x * x, axis=-1, keepdims=True)
    return x * lax.rsqrt(ms + NORM_EPS) * g


def _split3(x):
    hi = x.astype(BF16)
    r1 = x - hi.astype(F32)
    mid = r1.astype(BF16)
    lo = (r1 - mid.astype(F32)).astype(BF16)
    return hi, mid, lo


def _gdot(x, g):
    hi, lo, _ = _split3(x)
    return (jnp.dot(hi, g, preferred_element_type=F32)
            + jnp.dot(lo, g, preferred_element_type=F32))


def _silu(x):
    return x * jax.nn.sigmoid(x)


def _norm_mm_kernel(x_ref, g_ref, w_ref, o_ref, h_ref):
    @pl.when(pl.program_id(1) == 0)
    def _():
        h_ref[...] = _rms(x_ref[...], g_ref[...]).astype(BF16)

    o_ref[...] = jnp.dot(h_ref[...], w_ref[...], preferred_element_type=F32)


def _norm_matmul(x, g, w, tm, tn):
    t, d = x.shape
    n = w.shape[1]
    tm = _pick(t, tm)
    tn = _pick(n, tn)
    return pl.pallas_call(
        _norm_mm_kernel,
        grid=(t // tm, n // tn),
        in_specs=[
            pl.BlockSpec((tm, d), lambda i, j: (i, 0)),
            pl.BlockSpec((1, d), lambda i, j: (0, 0)),
            pl.BlockSpec((d, tn), lambda i, j: (0, j)),
        ],
        out_specs=pl.BlockSpec((tm, tn), lambda i, j: (i, j)),
        out_shape=jax.ShapeDtypeStruct((t, n), F32),
        scratch_shapes=[pltpu.VMEM((tm, d), BF16)],
        compiler_params=_cparams(("parallel", "arbitrary")),
    )(x, g.reshape(1, d), w)


PROJ_SEGMENTS = ((P_OFF_RW, RW_PAD, F32), (P_OFF_HG, 4 * HG_WIDTH, F32),
                 (P_OFF_GATE, N_BRANCH * D_MODEL, F32), (P_OFF_DA, 3 * DA_WIDTH, BF16))
MXU_N = 256


def _in_proj_kernel(x_ref, g_ref, w_ref, *outs):
    h = _rms(x_ref[...], g_ref[...]).astype(BF16)
    step = 2 * MXU_N
    for o_ref, (off, width, dt) in zip(outs, PROJ_SEGMENTS):
        for c0 in range(0, width, step):
            o_ref[:, c0:c0 + step] = jnp.dot(h, w_ref[:, off + c0:off + c0 + step],
                                             preferred_element_type=F32).astype(dt)


def _in_proj(x, g, w):
    t, d = x.shape
    tm = _pick(t, 256)
    return pl.pallas_call(
        _in_proj_kernel,
        grid=(t // tm,),
        in_specs=[
            pl.BlockSpec((tm, d), lambda i: (i, 0)),
            pl.BlockSpec((1, d), lambda i: (0, 0)),
            pl.BlockSpec((d, N_PROJ), lambda i: (0, 0), pipeline_mode=pl.Buffered(1)),
        ],
        out_specs=[pl.BlockSpec((tm, width), lambda i: (i, 0)) for _, width, _ in PROJ_SEGMENTS],
        out_shape=[jax.ShapeDtypeStruct((t, width), dt) for _, width, dt in PROJ_SEGMENTS],
        compiler_params=_cparams(("parallel",)),
    )(x, g.reshape(1, d), w)


HG_GROUP = 2


def _hgrn_kernel(q_ref, f_ref, i_ref, g_ref, lb_ref, ng_ref, o_ref,
                 st_ref, qs_s, b_s, k_s, v_s, o_s, *, ts):
    @pl.when(pl.program_id(2) == 0)
    def _():
        st_ref[...] = jnp.zeros_like(st_ref)

    hs = range(HG_GROUP)
    r16 = lax.broadcasted_iota(jnp.int32, (ts, HG_D), 0) % HG_SUB
    for hh in hs:
        cols = slice(hh * HG_D, (hh + 1) * HG_D)
        lb = lb_ref[:, cols]
        fp = f_ref[:, cols]
        f = lb + (1.0 - lb) * jax.nn.sigmoid(fp)
        k_s[hh] = (1.0 - lb) * jax.nn.sigmoid(-fp)
        qs_s[hh] = _silu(q_ref[:, cols])
        v_s[hh] = i_ref[:, cols]
        b = jnp.log(jnp.maximum(f, TINY))
        sh = 1
        while sh < HG_SUB:
            b = b + jnp.where(r16 >= sh, pltpu.roll(b, sh, axis=0), 0.0)
            sh *= 2
        b_s[hh] = b * LOG2E

    rowi8 = lax.broadcasted_iota(jnp.int32, (SUBLANES, HG_D), 0)

    def body(ci, carry):
        base = pl.multiple_of(ci * HG_SUB, HG_SUB)
        rows = pl.ds(base, HG_SUB)
        qc = [qs_s[hh, rows, :] for hh in hs]
        bc = [b_s[hh, rows, :] for hh in hs]
        st = [st_ref[hh] for hh in hs]
        o = [lax.dot_general((qc[hh] * jnp.exp2(bc[hh])).astype(BF16), st[hh].astype(BF16),
                             (((1,), (1,)), ((), ())), preferred_element_type=F32) for hh in hs]
        pieces = range(HG_SUB // SUBLANES)
        sl = [slice(p * SUBLANES, (p + 1) * SUBLANES) for p in pieces]
        qp = [[qc[hh][sl[p]] for p in pieces] for hh in hs]
        bp = [[bc[hh][sl[p]] for p in pieces] for hh in hs]
        op = [[None for p in pieces] for hh in hs]
        for j in range(HG_SUB):
            for hh in hs:
                bj = b_s[hh, pl.ds(base + j, 1), :]
                kj = k_s[hh, pl.ds(base + j, 1), :]
                vj = v_s[hh, pl.ds(base + j, 1), :]
                for p in pieces:
                    if (p + 1) * SUBLANES <= j:
                        continue
                    dec = jnp.exp2(jnp.minimum(bp[hh][p] - bj, 0.0))
                    sc = jnp.sum(qp[hh][p] * dec * kj, axis=-1, keepdims=True)
                    if p * SUBLANES < j:
                        sc = jnp.where(rowi8 >= j - p * SUBLANES, sc, 0.0)
                    op[hh][p] = sc * vj if op[hh][p] is None else op[hh][p] + sc * vj
        for hh in hs:
            for p in pieces:
                o_s[hh, pl.ds(base + p * SUBLANES, SUBLANES), :] = o[hh][sl[p]] + op[hh][p]
            bl = b_s[hh, pl.ds(base + HG_SUB - 1, 1), :]
            kd = k_s[hh, rows, :] * jnp.exp2(bl - bc[hh])
            upd = lax.dot_general(v_s[hh, rows, :].astype(BF16), kd.astype(BF16),
                                  (((0,), (0,)), ((), ())), preferred_element_type=F32)
            st_ref[hh] = st[hh] * jnp.exp2(bl) + upd
        return carry

    lax.fori_loop(0, ts // HG_SUB, body, 0, unroll=4)

    for hh in hs:
        cols = slice(hh * HG_D, (hh + 1) * HG_D)
        o = _rms(o_s[hh], ng_ref[:, cols])
        o_ref[:, cols] = o * _silu(g_ref[:, cols])


def _hgrn2(proj, lb, norm_g, b, s):
    ts = _pick(s, 256)
    nt = s // ts
    width = HG_GROUP * HG_D
    groups = HG_HEADS // HG_GROUP

    def col(section):
        return pl.BlockSpec((ts, width),
                            lambda bi, h, c, section=section: (bi * nt + c, section * groups + h))

    par = pl.BlockSpec((1, width), lambda bi, h, c: (0, h))
    return pl.pallas_call(
        functools.partial(_hgrn_kernel, ts=ts),
        grid=(b, groups, nt),
        in_specs=[col(0), col(1), col(2), col(3), par, par],
        out_specs=pl.BlockSpec((ts, width), lambda bi, h, c: (bi * nt + c, h)),
        out_shape=jax.ShapeDtypeStruct((b * s, HG_WIDTH), F32),
        scratch_shapes=[pltpu.VMEM((HG_GROUP, HG_D, HG_D), F32)]
        + [pltpu.VMEM((HG_GROUP, ts, HG_D), F32)] * 5,
        compiler_params=_cparams(("parallel", "parallel", "arbitrary")),
    )(proj, proj, proj, proj, lb.reshape(1, HG_WIDTH), norm_g.reshape(1, HG_WIDTH))


def _da_kernel(q_ref, k_ref, v_ref, sl_ref, lam_ref, sg_ref, o_ref,
               vb_s, q2_s, m_s, acc_s, *, tq, lambda_init):
    i = pl.program_id(2)
    nq = pl.num_programs(2)
    lanes = DA_DV

    @pl.when(i == 0)
    def _():
        def fill(c, carry):
            rows = pl.ds(pl.multiple_of(c * tq, tq), tq)
            vb_s[rows, 0:lanes] = v_ref[rows, :]
            vb_s[rows, lanes:2 * lanes] = jnp.ones((tq, lanes), BF16)
            return carry
        lax.fori_loop(0, nq, fill, 0)

    q = q_ref[...].astype(F32) * (DA_DK ** -0.5 * LOG2E)
    lane = lax.broadcasted_iota(jnp.int32, (tq, DA_DV), 1)
    q2_s[pl.ds(0, tq), :] = jnp.where(lane < DA_DK, q, 0.0).astype(BF16)
    q2_s[pl.ds(tq, tq), :] = jnp.where(lane >= DA_DK, q, 0.0).astype(BF16)
    m_s[...] = jnp.full_like(m_s, MASK_VALUE)
    acc_s[...] = jnp.zeros_like(acc_s)

    slope = sl_ref[0] * LOG2E
    colf = lax.broadcasted_iota(jnp.int32, (1, tq), 1).astype(F32)
    rb = min(DA_ROW_BLOCK, tq)

    def step(j, masked):
        start = pl.multiple_of(j * tq, tq)
        kb = k_ref[pl.ds(start, tq), :]
        vb = vb_s[pl.ds(start, tq), :]
        bias = slope * (colf + ((j - i) * tq).astype(F32))
        for r0 in range(0, 2 * tq, rb):
            rows = pl.ds(r0, rb)
            nk = min(tq, -(-((r0 % tq) + rb) // lanes) * lanes) if masked else tq
            s = lax.dot_general(q2_s[rows, :], kb[0:nk], (((1,), (1,)), ((), ())),
                                preferred_element_type=F32) + bias[:, 0:nk]
            if masked:
                r = lax.broadcasted_iota(jnp.int32, (rb, nk), 0) + (r0 % tq)
                c = lax.broadcasted_iota(jnp.int32, (rb, nk), 1)
                s = jnp.where(c <= r, s, MASK_VALUE)
            m_old = m_s[rows, :]
            m_new = jnp.maximum(m_old, jnp.max(s, axis=-1, keepdims=True))
            alpha = jnp.exp2(m_old - m_new)
            p = jnp.exp2(s - pltpu.repeat(m_new, nk // lanes, axis=1))
            acc_s[rows, :] = (pltpu.repeat(alpha, 2, axis=1) * acc_s[rows, :]
                              + jnp.dot(p.astype(BF16), vb[0:nk], preferred_element_type=F32))
            m_s[rows, :] = m_new

    def body_many(jj, carry):
        for u in range(DA_UNROLL):
            step(DA_UNROLL * jj + u, False)
        return carry

    full = i // DA_UNROLL
    lax.fori_loop(0, full, body_many, 0)
    for rem in range(DA_UNROLL):
        @pl.when(i % DA_UNROLL == rem)
        def _(rem=rem):
            for u in range(rem):
                step(full * DA_UNROLL + u, False)
            step(i, True)

    lam = lam_ref[0]
    e1 = jnp.exp(jnp.sum(lam[0:1] * lam[1:2], axis=-1, keepdims=True))
    e2 = jnp.exp(jnp.sum(lam[2:3] * lam[3:4], axis=-1, keepdims=True))
    lam_full = e1 - e2 + lambda_init
    o1 = acc_s[pl.ds(0, tq), 0:lanes] / acc_s[pl.ds(0, tq), lanes:2 * lanes]
    o2 = acc_s[pl.ds(tq, tq), 0:lanes] / acc_s[pl.ds(tq, tq), lanes:2 * lanes]
    o = o1 - lam_full * o2
    o_ref[...] = _rms(o, sg_ref[...]) * (1.0 - lambda_init)


def _diff_attention(qkv, lam, subln_g, lambda_init, b, s):
    tq = _pick(s, 512)
    nq = s // tq
    c0 = 0
    slopes = 2.0 ** (-ALIBI_MAX_BIAS * jnp.arange(1, DA_HEADS + 1, dtype=F32) / DA_HEADS)
    slopes = jnp.broadcast_to(slopes[:, None, None], (DA_HEADS, 1, tq))
    return pl.pallas_call(
        functools.partial(_da_kernel, tq=tq, lambda_init=lambda_init),
        grid=(b, DA_HEADS, nq),
        in_specs=[
            pl.BlockSpec((None, tq, DA_DV), lambda bi, h, i: (bi, i, c0 + h)),
            pl.BlockSpec((None, s, DA_DV), lambda bi, h, i: (bi, 0, c0 + DA_HEADS + h)),
            pl.BlockSpec((None, s, DA_DV), lambda bi, h, i: (bi, 0, c0 + 2 * DA_HEADS + h)),
            pl.BlockSpec((1, 1, tq), lambda bi, h, i: (h, 0, 0)),
            pl.BlockSpec((1, 4, DA_DK), lambda bi, h, i: (0, 0, 0)),
            pl.BlockSpec((1, DA_DV), lambda bi, h, i: (0, 0)),
        ],
        out_specs=pl.BlockSpec((None, tq, DA_DV), lambda bi, h, i: (bi, i, h)),
        out_shape=jax.ShapeDtypeStruct((b, s, DA_WIDTH), F32),
        scratch_shapes=[
            pltpu.VMEM((s, 2 * DA_DV), BF16),
            pltpu.VMEM((2 * tq, DA_DV), BF16),
            pltpu.VMEM((2 * tq, DA_DV), F32),
            pltpu.VMEM((2 * tq, 2 * DA_DV), F32),
        ],
        compiler_params=_cparams(("parallel", "parallel", "arbitrary")),
    )(qkv, qkv, qkv, slopes, lam.reshape(1, 4, DA_DK), subln_g.reshape(1, DA_DV))


def _rw_prep_kernel(z_ref, zp_ref, mu_ref, wa_ref, w0_ref, a0_ref, gup_ref, kk_ref, ka_ref,
                    rk_ref, hs_ref,
                    lw_o, r_o, k_o, v_o, kk_o, a_o, c3_o, g_o, *, tiles_per_seq):
    tm = z_ref.shape[0]
    z = z_ref[...]
    first = (pl.program_id(0) % tiles_per_seq) == 0
    prev = jnp.where(first, 0.0, zp_ref[SUBLANES - 1:SUBLANES, :])
    row = lax.broadcasted_iota(jnp.int32, z.shape, 0)
    zs = jnp.where(row == 0, prev, pltpu.roll(z, 1, axis=0))
    z = z + mu_ref[...] * (zs - z)

    w3 = RW_WIDTH
    r = z[:, 0:w3]
    k = z[:, w3:2 * w3]
    v = z[:, 2 * w3:3 * w3]
    wd_ad = z[:, 3 * w3:3 * w3 + 128]
    gd = z[:, 3 * w3 + 128:3 * w3 + 256]
    lane = lax.broadcasted_iota(jnp.int32, wd_ad.shape, 1)
    lowrank_in = jnp.where(lane < RW_DECAY_RANK, jnp.tanh(wd_ad), wd_ad)
    wa = jnp.dot(lowrank_in.astype(BF16), wa_ref[...], preferred_element_type=F32)
    w = w0_ref[...] + wa[:, 0:w3]
    nw = -w
    softplus = jnp.maximum(nw, 0.0) + jnp.log(1.0 + jnp.exp(-jnp.abs(nw)))
    log_decay = -jnp.exp(-softplus - 0.5)
    a = jax.nn.sigmoid(a0_ref[...] + wa[:, w3:2 * w3])
    g = jnp.dot(jax.nn.sigmoid(gd).astype(BF16), gup_ref[...], preferred_element_type=F32)

    hs = hs_ref[...]
    kk = k * kk_ref[...]
    kk = kk / jnp.maximum(jnp.sqrt(_gdot(kk * kk, hs)), 1e-12)
    kmod = k * (1.0 + (a - 1.0) * ka_ref[...])

    lw_o[...] = log_decay
    r_o[...] = r
    k_o[...] = kmod
    v_o[...] = v
    kk_o[...] = kk
    a_o[...] = a
    c3_o[...] = _gdot(r * kmod * rk_ref[...], hs)
    g_o[...] = g


def _head_sum_matrix(n, head):
    idx = jnp.arange(n) // head
    return (idx[:, None] == idx[None, :]).astype(BF16)


def _rwkv_prep(proj, p, b, s):
    t = b * s
    tm = _pick(s, 256)
    tiles_per_seq = s // tm
    w3 = RW_WIDTH
    wa = jnp.zeros((128, 2 * w3), F32)
    wa = wa.at[0:RW_DECAY_RANK, 0:w3].set(p["w_up"]).at[RW_DECAY_RANK:128, w3:].set(p["a_up"])
    mu = jnp.pad(p["mu"], (0, RW_PAD - RW_COLS)).reshape(1, RW_PAD)
    row = lambda a: a.reshape(1, w3)
    const = lambda shape: pl.BlockSpec(shape, lambda i: (0, 0))
    out_spec = pl.BlockSpec((tm, w3), lambda i: (i, 0))
    nb8 = tm // SUBLANES
    return pl.pallas_call(
        functools.partial(_rw_prep_kernel, tiles_per_seq=tiles_per_seq),
        grid=(t // tm,),
        in_specs=[
            pl.BlockSpec((tm, RW_PAD), lambda i: (i, 0)),
            pl.BlockSpec((SUBLANES, RW_PAD), lambda i: (jnp.maximum(i * nb8 - 1, 0), 0)),
            const((1, RW_PAD)), const((128, 2 * w3)), const((1, w3)), const((1, w3)),
            const((RW_GATE_RANK, w3)), const((1, w3)), const((1, w3)), const((1, w3)),
            const((w3, w3)),
        ],
        out_specs=[out_spec] * 8,
        out_shape=[jax.ShapeDtypeStruct((t, w3), F32)] * 8,
        compiler_params=_cparams(("parallel",)),
    )(proj, proj, mu, wa.astype(BF16), row(p["w0"]), row(p["a0"]), p["g_up"].astype(BF16),
      row(p["k_k"]), row(p["k_a"]), row(p["r_k"]), _head_sum_matrix(w3, RW_HEAD))


_NT = (((1,), (1,)), ((), ()))
_TN = (((0,), (0,)), ((), ()))
_NN = (((1,), (0,)), ((), ()))


def _mm(a, b, dims=_NN):
    return lax.dot_general(a.astype(BF16), b.astype(BF16), dims, preferred_element_type=F32)


def _rw_chunk_kernel(lw_ref, r_ref, k_ref, v_ref, kk_ref, a_ref, tri_ref, y_ref, st_ref, *, c):
    @pl.when(pl.program_id(1) == 0)
    def _():
        st_ref[...] = jnp.zeros_like(st_ref)

    lanes = 2 * RW_HEAD
    tri = tri_ref[...]
    row = lax.broadcasted_iota(jnp.int32, (c, c), 0)
    col = lax.broadcasted_iota(jnp.int32, (c, c), 1)
    strict = row > col
    incl = row >= col
    eye = (row == col).astype(F32)
    blk = 16
    diag_blocks = (row // blk) == (col // blk)
    lane = lax.broadcasted_iota(jnp.int32, (c, lanes), 1)
    head_mask = (lane < RW_HEAD, lane >= RW_HEAD)
    srow = lax.broadcasted_iota(jnp.int32, (lanes, lanes), 0)
    scol = lax.broadcasted_iota(jnp.int32, (lanes, lanes), 1)
    same_head = (srow // RW_HEAD) == (scol // RW_HEAD)

    def by_head(x):
        return jnp.concatenate([jnp.where(hm, x, 0.0) for hm in head_mask], axis=0)

    npair = RW_HEADS // 2
    pairs = range(npair)
    heads = range(RW_HEADS)
    cols = [pl.ds(p * lanes, lanes) for p in pairs]
    lw = [lw_ref[0, :, cols[p]] for p in pairs]
    parts = [_split3(lw[p]) for p in pairs]
    lc = [sum(jnp.dot(tri, part, preferred_element_type=F32) for part in parts[p])
          for p in pairs]
    kk = [kk_ref[0, :, cols[p]] for p in pairs]
    k = [k_ref[0, :, cols[p]] for p in pairs]
    v = [v_ref[0, :, cols[p]] for p in pairs]
    beta = [kk[p] * a_ref[0, :, cols[p]] for p in pairs]
    a_t = [-kk[p] * jnp.exp(lc[p] - lw[p]) for p in pairs]
    r_t = [r_ref[0, :, cols[p]] * jnp.exp(lc[p]) for p in pairs]
    e_neg = [jnp.exp(-lc[p]) for p in pairs]
    rhs = [jnp.concatenate([beta[p] * e_neg[p], k[p] * e_neg[p]], axis=0) for p in pairs]

    m = [_mm(jnp.concatenate([jnp.where(head_mask[h % 2], a_t[h // 2], 0.0),
                              jnp.where(head_mask[h % 2], r_t[h // 2], 0.0)], axis=0),
             rhs[h // 2], _NT) for h in heads]
    l_mat = [jnp.where(strict, m[h][0:c, 0:c], 0.0) for h in heads]
    a_ak = [jnp.where(strict, m[h][0:c, c:2 * c], 0.0) for h in heads]
    a_rb = [jnp.where(incl, m[h][c:2 * c, 0:c], 0.0) for h in heads]
    a_rk = [jnp.where(incl, m[h][c:2 * c, c:2 * c], 0.0) for h in heads]

    t_inv = [eye + jnp.where(((row // 2) == (col // 2)) & (row % 2 == 1) & (col % 2 == 0),
                             l_mat[h], 0.0) for h in heads]
    size = 2
    while size < c:
        sel = (((row // (2 * size)) == (col // (2 * size)))
               & ((row // size) % 2 == 1) & ((col // size) % 2 == 0))
        po = [_mm(t_inv[h], jnp.where(sel, l_mat[h], 0.0)) for h in heads]
        t_inv = [t_inv[h] + _mm(po[h], t_inv[h]) for h in heads]
        size *= 2

    s0 = [st_ref[p] for p in pairs]
    xr = [_mm(jnp.concatenate([a_t[p], r_t[p]], axis=0), s0[p], _NT) for p in pairs]
    v2 = [by_head(v[p]) for p in pairs]
    x = [xr[p][0:c] + _mm(jnp.concatenate(a_ak[2 * p:2 * p + 2], axis=1), v2[p]) for p in pairs]
    u = [_mm(jnp.concatenate(t_inv[2 * p:2 * p + 2], axis=1), by_head(x[p]))
         for p in pairs]
    for p in pairs:
        y_ref[0, :, cols[p]] = xr[p][c:2 * c] + _mm(
            jnp.concatenate(a_rb[2 * p:2 * p + 2] + a_rk[2 * p:2 * p + 2], axis=1),
            jnp.concatenate([by_head(u[p]), v2[p]], axis=0))
    for p in pairs:
        lc_end = lc[p][c - 1:c, :]
        e_end = jnp.exp(lc_end - lc[p])
        upd = _mm(jnp.concatenate([u[p], v[p]], axis=0),
                  jnp.concatenate([beta[p] * e_end, k[p] * e_end], axis=0), _TN)
        st_ref[p] = jnp.where(same_head, s0[p] * jnp.exp(lc_end) + upd, 0.0)


def _rwkv_chunked(lw, r, k, v, kk, a, b, s):
    c = _pick(s, 128)
    w3 = RW_WIDTH
    r3 = lambda x: x.reshape(b, s, w3)
    spec = pl.BlockSpec((1, c, w3), lambda bi, i: (bi, i, 0))
    tri = (jnp.arange(c)[:, None] >= jnp.arange(c)[None, :]).astype(BF16)
    return pl.pallas_call(
        functools.partial(_rw_chunk_kernel, c=c),
        grid=(b, s // c),
        in_specs=[spec] * 6 + [pl.BlockSpec((c, c), lambda bi, i: (0, 0))],
        out_specs=spec,
        out_shape=jax.ShapeDtypeStruct((b, s, w3), F32),
        scratch_shapes=[pltpu.VMEM((RW_HEADS // 2, 2 * RW_HEAD, 2 * RW_HEAD), F32)],
        compiler_params=_cparams(("parallel", "arbitrary")),
    )(r3(lw), r3(r), r3(k), r3(v), r3(kk), r3(a), tri).reshape(b * s, w3)


def _merge_kernel(x_ref, hg_ref, da_ref, y_ref, v_ref, c3_ref, g_ref, g0_ref, g1_ref, g2_ref,
                  bg_ref, lng_ref, lnb_ref, hs_ref, wb_ref, wo_ref, o_ref):
    hs = hs_ref[...]
    y = y_ref[...]
    mu = _gdot(y, hs) * (1.0 / RW_HEAD)
    yc = y - mu
    var = _gdot(yc * yc, hs) * (1.0 / RW_HEAD)
    yn = yc * lax.rsqrt(var + RW_GN_EPS) * lng_ref[...] + lnb_ref[...]
    o_rw = (yn + c3_ref[...] * v_ref[...]) * g_ref[...]

    merged = None
    for n, (br, gp) in enumerate(((hg_ref[...], g0_ref), (da_ref[...], g1_ref), (o_rw, g2_ref))):
        gate = jax.nn.sigmoid(gp[...] + bg_ref[:, n * D_MODEL:(n + 1) * D_MODEL])
        pb = jnp.dot(br.astype(BF16), wb_ref[n], preferred_element_type=F32)
        merged = gate * pb if merged is None else merged + gate * pb
    o_ref[...] = x_ref[...] + jnp.dot(merged.astype(BF16), wo_ref[...],
                                      preferred_element_type=F32)


def _merge(x, o_hg, o_da, y, v, c3, g, proj, b_gate, ln_g, ln_b, w_branch, w_out):
    t = x.shape[0]
    tm = _pick(t, 512)
    bw = BRANCH_WIDTH
    row = pl.BlockSpec((tm, bw), lambda i: (i, 0))
    g0 = 0
    gate_spec = lambda n: pl.BlockSpec((tm, D_MODEL), lambda i, n=n: (i, g0 + n))
    const = lambda shape: pl.BlockSpec(shape, lambda i: (0,) * len(shape),
                                       pipeline_mode=pl.Buffered(1))
    return pl.pallas_call(
        _merge_kernel,
        grid=(t // tm,),
        in_specs=[pl.BlockSpec((tm, D_MODEL), lambda i: (i, 0))] + [row] * 6
        + [gate_spec(0), gate_spec(1), gate_spec(2),
           const((1, N_BRANCH * D_MODEL)), const((1, bw)), const((1, bw)), const((bw, bw)),
           const((N_BRANCH, bw, D_MODEL)), const((D_MODEL, D_MODEL))],
        out_specs=pl.BlockSpec((tm, D_MODEL), lambda i: (i, 0)),
        out_shape=jax.ShapeDtypeStruct((t, D_MODEL), F32),
        compiler_params=_cparams(("parallel",)),
    )(x, o_hg, o_da, y, v, c3, g, proj, proj, proj, b_gate.reshape(1, -1),
      ln_g.reshape(1, bw), ln_b.reshape(1, bw), _head_sum_matrix(bw, RW_HEAD),
      w_branch.astype(BF16), w_out.astype(BF16))


def _xattn_kernel(x_ref, g_ref, wq_ref, kv_ref, wo_ref, o_ref):
    x = x_ref[...]
    h = _rms(x, g_ref[...]).astype(BF16)
    q = jnp.dot(h, wq_ref[...], preferred_element_type=F32) * (XA_HEAD ** -0.5)
    outs = []
    for hd in range(XA_HEADS):
        qh = q[:, hd * XA_HEAD:(hd + 1) * XA_HEAD].astype(BF16)
        kh = kv_ref[0, :, hd * XA_HEAD:(hd + 1) * XA_HEAD].astype(BF16)
        vh = kv_ref[0, :, D_MODEL + hd * XA_HEAD:D_MODEL + (hd + 1) * XA_HEAD].astype(BF16)
        s = lax.dot_general(qh, kh, (((1,), (1,)), ((), ())), preferred_element_type=F32)
        m = jnp.max(s, axis=-1, keepdims=True)
        p = jnp.exp(s - m)
        p = p / jnp.sum(p, axis=-1, keepdims=True)
        outs.append(jnp.dot(p.astype(BF16), vh, preferred_element_type=F32))
    o = jnp.concatenate(outs, axis=-1).astype(BF16)
    o_ref[...] = x + jnp.dot(o, wo_ref[...], preferred_element_type=F32)


def _cross_attention(x, g, wq, kv, wo, b, s):
    tm = _pick(s, 512)
    nt = s // tm
    m = kv.shape[1]
    return pl.pallas_call(
        _xattn_kernel,
        grid=(b, nt),
        in_specs=[
            pl.BlockSpec((tm, D_MODEL), lambda bi, i: (bi * nt + i, 0)),
            pl.BlockSpec((1, D_MODEL), lambda bi, i: (0, 0)),
            pl.BlockSpec((D_MODEL, D_MODEL), lambda bi, i: (0, 0)),
            pl.BlockSpec((1, m, 2 * D_MODEL), lambda bi, i: (bi, 0, 0)),
            pl.BlockSpec((D_MODEL, D_MODEL), lambda bi, i: (0, 0)),
        ],
        out_specs=pl.BlockSpec((tm, D_MODEL), lambda bi, i: (bi * nt + i, 0)),
        out_shape=jax.ShapeDtypeStruct((b * s, D_MODEL), F32),
        compiler_params=_cparams(("parallel", "parallel")),
    )(x, g.reshape(1, D_MODEL), wq.astype(BF16), kv, wo.astype(BF16))


FFN_HALO = 16
FFN_CHUNK = 256


def _ffn_kernel(x_ref, xp_ref, g_ref, wu_ref, wv_ref, cw_ref, cb_ref, wd_ref, *rest,
                tiles_per_seq, out_norm):
    if out_norm:
        og_ref, o_ref, h_s, act_s = rest
    else:
        o_ref, h_s, act_s = rest
    tm = x_ref.shape[0]
    first = (pl.program_id(0) % tiles_per_seq) == 0
    hp = _rms(xp_ref[...], g_ref[...])
    h_s[0:FFN_HALO, :] = jnp.where(first, 0.0, hp).astype(BF16)
    h_s[FFN_HALO:, :] = _rms(x_ref[...], g_ref[...]).astype(BF16)
    row = lax.broadcasted_iota(jnp.int32, (tm, FFN_CHUNK), 0)
    for c0 in range(0, D_FF, FFN_CHUNK):
        cols = slice(c0, c0 + FFN_CHUNK)
        ue = jnp.dot(h_s[...], wu_ref[:, cols], preferred_element_type=F32)
        vv = jnp.dot(h_s[FFN_HALO:, :], wv_ref[:, cols], preferred_element_type=F32)
        u = ue[FFN_HALO:, :]
        p1 = ue[FFN_HALO - 1:FFN_HALO, :]
        p2 = ue[FFN_HALO - 2:FFN_HALO - 1, :]
        u1 = jnp.where(row == 0, p1, pltpu.roll(u, 1, axis=0))
        u2 = jnp.where(row == 0, p2, jnp.where(row == 1, p1, pltpu.roll(u, 2, axis=0)))
        uc = (cw_ref[0:1, cols] * u2 + cw_ref[1:2, cols] * u1 + cw_ref[2:3, cols] * u
              + cb_ref[:, cols])
        act_s[:, cols] = (_silu(uc) * vv).astype(BF16)
    out = x_ref[...] + jnp.dot(act_s[...], wd_ref[...], preferred_element_type=F32)
    o_ref[...] = _rms(out, og_ref[...]) if out_norm else out


def _ffn(x, g, w_up, conv_w, conv_b, w_down, b, s, out_gain=None):
    t = x.shape[0]
    tm = _pick(s, 512)
    tiles_per_seq = s // tm
    nbh = tm // FFN_HALO
    out_norm = out_gain is not None
    wu = w_up.astype(BF16)
    resident = lambda shape, idx: pl.BlockSpec(shape, lambda i: idx, pipeline_mode=pl.Buffered(1))
    return pl.pallas_call(
        functools.partial(_ffn_kernel, tiles_per_seq=tiles_per_seq, out_norm=out_norm),
        grid=(t // tm,),
        in_specs=[
            pl.BlockSpec((tm, D_MODEL), lambda i: (i, 0)),
            pl.BlockSpec((FFN_HALO, D_MODEL), lambda i: (jnp.maximum(i * nbh - 1, 0), 0)),
            resident((1, D_MODEL), (0, 0)),
            resident((D_MODEL, D_FF), (0, 0)),
            resident((D_MODEL, D_FF), (0, 1)),
            resident((CONV_W, D_FF), (0, 0)),
            resident((1, D_FF), (0, 0)),
            resident((D_FF, D_MODEL), (0, 0)),
        ] + ([resident((1, D_MODEL), (0, 0))] if out_norm else []),
        out_specs=pl.BlockSpec((tm, D_MODEL), lambda i: (i, 0)),
        out_shape=jax.ShapeDtypeStruct((t, D_MODEL), F32),
        scratch_shapes=[pltpu.VMEM((tm + FFN_HALO, D_MODEL), BF16), pltpu.VMEM((tm, D_FF), BF16)],
        compiler_params=_cparams(("parallel",)),
    )(x, x, g.reshape(1, D_MODEL), wu, wu, conv_w, conv_b.reshape(1, D_FF),
      w_down.astype(BF16), *([out_gain.reshape(1, D_MODEL)] if out_norm else []))


def _permute_w_in(w):
    d = w.shape[0]
    return jnp.concatenate([
        w[:, R_OFF_RW:R_OFF_GATE], jnp.zeros((d, RW_PAD - RW_COLS), w.dtype),
        w[:, 0:R_OFF_DA], w[:, R_OFF_GATE:], w[:, R_OFF_DA:R_OFF_RW]], axis=1).astype(BF16)


def kernel(x, mem, norm_mix_g, w_in, b_gate, hgrn_lb_param, hgrn_norm_g, diff_lambda, diff_subln_g, rwkv_mu, rwkv_w0, rwkv_w_up, rwkv_a0, rwkv_a_up, rwkv_g_up, rwkv_k_k, rwkv_k_a, rwkv_r_k, rwkv_ln_g, rwkv_ln_b, w_branch, w_out, norm_xa_g, norm_mem_g, xa_wq, xa_wkv, xa_wo, norm_ffn_g, ffn_w_up, ffn_conv_w, ffn_conv_b, ffn_w_down, final_norm_g):
    b, s, d = x.shape
    depth = w_in.shape[0]
    m = mem.shape[1]
    t = b * s
    lb_p = jax.nn.softmax(hgrn_lb_param.astype(F32), axis=0)
    lower_bounds = jnp.cumsum(lb_p, axis=0) - lb_p[0]

    xf = x.reshape(t, d)
    memf = mem.reshape(b * m, d)
    for l in range(depth):
        p_rw, p_hg, p_gate, p_da = _in_proj(xf, norm_mix_g[l], _permute_w_in(w_in[l]))
        o_hg = _hgrn2(p_hg, lower_bounds[l], hgrn_norm_g[l], b, s)
        lambda_init = 0.8 - 0.6 * math.exp(-0.3 * l)
        o_da = _diff_attention(p_da.reshape(b, s, 3 * DA_WIDTH), diff_lambda[l], diff_subln_g[l],
                               lambda_init, b, s).reshape(t, DA_WIDTH)
        rw_params = dict(mu=rwkv_mu[l], w0=rwkv_w0[l], w_up=rwkv_w_up[l], a0=rwkv_a0[l],
                         a_up=rwkv_a_up[l], g_up=rwkv_g_up[l], k_k=rwkv_k_k[l], k_a=rwkv_k_a[l],
                         r_k=rwkv_r_k[l].reshape(-1))
        lw, r, k, v, kk, a, c3, g = _rwkv_prep(p_rw, rw_params, b, s)
        y = _rwkv_chunked(lw, r, k, v, kk, a, b, s)
        xf = _merge(xf, o_hg, o_da, y, v, c3, g, p_gate, b_gate[l], rwkv_ln_g[l], rwkv_ln_b[l],
                    w_branch[l], w_out[l])
        kv = _norm_matmul(memf, norm_mem_g[l], xa_wkv[l].astype(BF16), 256, 512)
        xf = _cross_attention(xf, norm_xa_g[l], xa_wq[l], kv.reshape(b, m, 2 * d), xa_wo[l], b, s)
        xf = _ffn(xf, norm_ffn_g[l], ffn_w_up[l], ffn_conv_w[l], ffn_conv_b[l], ffn_w_down[l], b, s,
                  out_gain=final_norm_g if l == depth - 1 else None)
    return xf.reshape(b, s, d)
```

```python
import functools
import math

import jax
import jax.numpy as jnp
from jax import lax
from jax.experimental import pallas as pl
from jax.experimental.pallas import tpu as pltpu

F32 = jnp.float32
BF16 = jnp.bfloat16

D_MODEL = 1024
NORM_EPS = 1e-6
MASK_VALUE = -1e30
TINY = 1e-30

HG_HEADS = 4
HG_D = 128
HG_WIDTH = HG_HEADS * HG_D
HG_SUB = 16

DA_HEADS = 4
DA_DK = 64
DA_DV = 128
DA_WIDTH = DA_HEADS * DA_DV
ALIBI_MAX_BIAS = 8.0
LOG2E = 1.4426950408889634
DA_ROW_BLOCK = 128
DA_UNROLL = 4

RW_HEADS = 8
RW_HEAD = 64
RW_WIDTH = RW_HEADS * RW_HEAD
RW_DECAY_RANK = 64
RW_A_RANK = 64
RW_GATE_RANK = 128
RW_COLS = 3 * RW_WIDTH + RW_DECAY_RANK + RW_A_RANK + RW_GATE_RANK
RW_GN_EPS = 64e-5

N_BRANCH = 3
BRANCH_WIDTH = 512

XA_HEADS = 4
XA_HEAD = D_MODEL // XA_HEADS

D_FF = 2816
CONV_W = 3

RW_PAD = 2048
P_OFF_RW = 0
P_OFF_HG = RW_PAD
P_OFF_GATE = P_OFF_HG + 4 * HG_WIDTH
P_OFF_DA = P_OFF_GATE + N_BRANCH * D_MODEL
N_PROJ = P_OFF_DA + 3 * DA_WIDTH

R_OFF_DA = 4 * HG_WIDTH
R_OFF_RW = R_OFF_DA + 3 * DA_WIDTH
R_OFF_GATE = R_OFF_RW + RW_COLS

V7X_VMEM_LIMIT = 56 * 1024 * 1024
SUBLANES = 8


def _cparams(sem):
    return pltpu.CompilerParams(dimension_semantics=sem, vmem_limit_bytes=V7X_VMEM_LIMIT)


def _pick(n, pref):
    t = min(n, pref)
    while n % t:
        t //= 2
    return t


def _rms(x, g):
    ms = jnp.mean(x * x, axis=-1, keepdims=True)
    return x * lax.rsqrt(ms + NORM_EPS) * g


def _split3(x):
    hi = x.astype(BF16)
    r1 = x - hi.astype(F32)
    mid = r1.astype(BF16)
    lo = (r1 - mid.astype(F32)).astype(BF16)
    return hi, mid, lo


def _gdot(x, g):
    hi, lo, _ = _split3(x)
    return (jnp.dot(hi, g, preferred_element_type=F32)
            + jnp.dot(lo, g, preferred_element_type=F32))


def _silu(x):
    return x * jax.nn.sigmoid(x)


def _norm_mm_kernel(x_ref, g_ref, w_ref, o_ref, h_ref):
    @pl.when(pl.program_id(1) == 0)
    def _():
        h_ref[...] = _rms(x_ref[...], g_ref[...]).astype(BF16)

    o_ref[...] = jnp.dot(h_ref[...], w_ref[...], preferred_element_type=F32)


def _norm_matmul(x, g, w, tm, tn):
    t, d = x.shape
    n = w.shape[1]
    tm = _pick(t, tm)
    tn = _pick(n, tn)
    return pl.pallas_call(
        _norm_mm_kernel,
        grid=(t // tm, n // tn),
        in_specs=[
            pl.BlockSpec((tm, d), lambda i, j: (i, 0)),
            pl.BlockSpec((1, d), lambda i, j: (0, 0)),
            pl.BlockSpec((d, tn), lambda i, j: (0, j)),
        ],
        out_specs=pl.BlockSpec((tm, tn), lambda i, j: (i, j)),
        out_shape=jax.ShapeDtypeStruct((t, n), F32),
        scratch_shapes=[pltpu.VMEM((tm, d), BF16)],
        compiler_params=_cparams(("parallel", "arbitrary")),
    )(x, g.reshape(1, d), w)


PROJ_SEGMENTS = ((P_OFF_RW, RW_PAD, F32), (P_OFF_HG, 4 * HG_WIDTH, F32),
                 (P_OFF_GATE, N_BRANCH * D_MODEL, F32), (P_OFF_DA, 3 * DA_WIDTH, BF16))
MXU_N = 256


def _in_proj_kernel(x_ref, g_ref, w_ref, *outs):
    h = _rms(x_ref[...], g_ref[...]).astype(BF16)
    step = 2 * MXU_N
    for o_ref, (off, width, dt) in zip(outs, PROJ_SEGMENTS):
        for c0 in range(0, width, step):
            o_ref[:, c0:c0 + step] = jnp.dot(h, w_ref[:, off + c0:off + c0 + step],
                                             preferred_element_type=F32).astype(dt)


def _in_proj(x, g, w):
    t, d = x.shape
    tm = _pick(t, 256)
    return pl.pallas_call(
        _in_proj_kernel,
        grid=(t // tm,),
        in_specs=[
            pl.BlockSpec((tm, d), lambda i: (i, 0)),
            pl.BlockSpec((1, d), lambda i: (0, 0)),
            pl.BlockSpec((d, N_PROJ), lambda i: (0, 0), pipeline_mode=pl.Buffered(1)),
        ],
        out_specs=[pl.BlockSpec((tm, width), lambda i: (i, 0)) for _, width, _ in PROJ_SEGMENTS],
        out_shape=[jax.ShapeDtypeStruct((t, width), dt) for _, width, dt in PROJ_SEGMENTS],
        compiler_params=_cparams(("parallel",)),
    )(x, g.reshape(1, d), w)


HG_GROUP = 2


def _hgrn_kernel(q_ref, f_ref, i_ref, g_ref, lb_ref, ng_ref, o_ref,
                 st_ref, qs_s, b_s, k_s, v_s, o_s, *, ts):
    @pl.when(pl.program_id(2) == 0)
    def _():
        st_ref[...] = jnp.zeros_like(st_ref)

    hs = range(HG_GROUP)
    r16 = lax.broadcasted_iota(jnp.int32, (ts, HG_D), 0) % HG_SUB
    for hh in hs:
        cols = slice(hh * HG_D, (hh + 1) * HG_D)
        lb = lb_ref[:, cols]
        fp = f_ref[:, cols]
        f = lb + (1.0 - lb) * jax.nn.sigmoid(fp)
        k_s[hh] = (1.0 - lb) * jax.nn.sigmoid(-fp)
        qs_s[hh] = _silu(q_ref[:, cols])
        v_s[hh] = i_ref[:, cols]
        b = jnp.log(jnp.maximum(f, TINY))
        sh = 1
        while sh < HG_SUB:
            b = b + jnp.where(r16 >= sh, pltpu.roll(b, sh, axis=0), 0.0)
            sh *= 2
        b_s[hh] = b * LOG2E

    rowi = lax.broadcasted_iota(jnp.int32, (HG_SUB, HG_D), 0)

    def body(ci, carry):
        base = pl.multiple_of(ci * HG_SUB, HG_SUB)
        rows = pl.ds(base, HG_SUB)
        qc = [qs_s[hh, rows, :] for hh in hs]
        bc = [b_s[hh, rows, :] for hh in hs]
        st = [st_ref[hh] for hh in hs]
        zero = jnp.zeros((HG_D, HG_D), BF16)
        w_bd = jnp.concatenate(
            [jnp.concatenate([st[hh].astype(BF16) if c == hh else zero for c in hs], axis=1)
             for hh in hs], axis=0)
        qe = jnp.concatenate([(qc[hh] * jnp.exp2(bc[hh])).astype(BF16) for hh in hs], axis=1)
        o_cat = lax.dot_general(qe, w_bd, (((1,), (1,)), ((), ())), preferred_element_type=F32)
        o = [o_cat[:, hh * HG_D:(hh + 1) * HG_D] for hh in hs]
        for j in range(HG_SUB):
            for hh in hs:
                bj = b_s[hh, pl.ds(base + j, 1), :]
                kj = k_s[hh, pl.ds(base + j, 1), :]
                vj = v_s[hh, pl.ds(base + j, 1), :]
                dec = jnp.exp2(jnp.minimum(bc[hh] - bj, 0.0))
                sc = jnp.where(rowi >= j,
                               jnp.sum(qc[hh] * dec * kj, axis=-1, keepdims=True), 0.0)
                o[hh] = o[hh] + sc * vj
        bl = [b_s[hh, pl.ds(base + HG_SUB - 1, 1), :] for hh in hs]
        kd = jnp.concatenate([(k_s[hh, rows, :] * jnp.exp2(bl[hh] - bc[hh])).astype(BF16)
                              for hh in hs], axis=1)
        vc = jnp.concatenate([v_s[hh, rows, :].astype(BF16) for hh in hs], axis=1)
        upd = lax.dot_general(vc, kd, (((0,), (0,)), ((), ())), preferred_element_type=F32)
        for hh in hs:
            o_s[hh, rows, :] = o[hh]
            blk = slice(hh * HG_D, (hh + 1) * HG_D)
            st_ref[hh] = st[hh] * jnp.exp2(bl[hh]) + upd[blk, blk]
        return carry

    lax.fori_loop(0, ts // HG_SUB, body, 0, unroll=8)

    for hh in hs:
        cols = slice(hh * HG_D, (hh + 1) * HG_D)
        o = _rms(o_s[hh], ng_ref[:, cols])
        o_ref[:, cols] = (o * _silu(g_ref[:, cols])).astype(o_ref.dtype)


def _hgrn2(proj, lb, norm_g, b, s):
    ts = _pick(s, 256)
    nt = s // ts
    width = HG_GROUP * HG_D
    groups = HG_HEADS // HG_GROUP

    def col(section):
        return pl.BlockSpec((ts, width),
                            lambda bi, h, c, section=section: (bi * nt + c, section * groups + h))

    par = pl.BlockSpec((1, width), lambda bi, h, c: (0, h))
    return pl.pallas_call(
        functools.partial(_hgrn_kernel, ts=ts),
        grid=(b, groups, nt),
        in_specs=[col(0), col(1), col(2), col(3), par, par],
        out_specs=pl.BlockSpec((ts, width), lambda bi, h, c: (bi * nt + c, h)),
        out_shape=jax.ShapeDtypeStruct((b * s, HG_WIDTH), BF16),
        scratch_shapes=[pltpu.VMEM((HG_GROUP, HG_D, HG_D), F32)]
        + [pltpu.VMEM((HG_GROUP, ts, HG_D), F32)] * 5,
        compiler_params=_cparams(("parallel", "parallel", "arbitrary")),
    )(proj, proj, proj, proj, lb.reshape(1, HG_WIDTH), norm_g.reshape(1, HG_WIDTH))


def _da_kernel(q_ref, k_ref, v_ref, sl_ref, lam_ref, sg_ref, o_ref,
               vb_s, q2_s, m_s, acc_s, *, tq, lambda_init):
    i = pl.program_id(2)
    nq = pl.num_programs(2)
    lanes = DA_DV

    @pl.when(i == 0)
    def _():
        def fill(c, carry):
            rows = pl.ds(pl.multiple_of(c * tq, tq), tq)
            vb_s[rows, 0:lanes] = v_ref[rows, :]
            vb_s[rows, lanes:2 * lanes] = jnp.ones((tq, lanes), BF16)
            return carry
        lax.fori_loop(0, nq, fill, 0)

    q = q_ref[...].astype(F32) * (DA_DK ** -0.5 * LOG2E)
    lane = lax.broadcasted_iota(jnp.int32, (tq, DA_DV), 1)
    q2_s[pl.ds(0, tq), :] = jnp.where(lane < DA_DK, q, 0.0).astype(BF16)
    q2_s[pl.ds(tq, tq), :] = jnp.where(lane >= DA_DK, q, 0.0).astype(BF16)
    m_s[...] = jnp.full_like(m_s, MASK_VALUE)
    acc_s[...] = jnp.zeros_like(acc_s)

    slope = sl_ref[0] * LOG2E
    colf = lax.broadcasted_iota(jnp.int32, (1, tq), 1).astype(F32)
    rb = min(DA_ROW_BLOCK, tq)

    def step(j, masked):
        start = pl.multiple_of(j * tq, tq)
        kb = k_ref[pl.ds(start, tq), :]
        vb = vb_s[pl.ds(start, tq), :]
        bias = slope * (colf + ((j - i) * tq).astype(F32))
        for r0 in range(0, 2 * tq, rb):
            rows = pl.ds(r0, rb)
            s = lax.dot_general(q2_s[rows, :], kb, (((1,), (1,)), ((), ())),
                                preferred_element_type=F32) + bias
            if masked:
                r = lax.broadcasted_iota(jnp.int32, (rb, tq), 0) + (r0 % tq)
                c = lax.broadcasted_iota(jnp.int32, (rb, tq), 1)
                s = jnp.where(c <= r, s, MASK_VALUE)
            m_old = m_s[rows, :]
            m_new = jnp.maximum(m_old, jnp.max(s, axis=-1, keepdims=True))
            alpha = jnp.exp2(m_old - m_new)
            p = jnp.exp2(s - pltpu.repeat(m_new, tq // lanes, axis=1))
            acc_s[rows, :] = (pltpu.repeat(alpha, 2, axis=1) * acc_s[rows, :]
                              + jnp.dot(p.astype(BF16), vb, preferred_element_type=F32))
            m_s[rows, :] = m_new

    def body_many(jj, carry):
        for u in range(DA_UNROLL):
            step(DA_UNROLL * jj + u, False)
        return carry

    full = i // DA_UNROLL
    lax.fori_loop(0, full, body_many, 0)
    for rem in range(DA_UNROLL):
        @pl.when(i % DA_UNROLL == rem)
        def _(rem=rem):
            for u in range(rem):
                step(full * DA_UNROLL + u, False)
            step(i, True)

    lam = lam_ref[0]
    e1 = jnp.exp(jnp.sum(lam[0:1] * lam[1:2], axis=-1, keepdims=True))
    e2 = jnp.exp(jnp.sum(lam[2:3] * lam[3:4], axis=-1, keepdims=True))
    lam_full = e1 - e2 + lambda_init
    o1 = acc_s[pl.ds(0, tq), 0:lanes] / acc_s[pl.ds(0, tq), lanes:2 * lanes]
    o2 = acc_s[pl.ds(tq, tq), 0:lanes] / acc_s[pl.ds(tq, tq), lanes:2 * lanes]
    o = o1 - lam_full * o2
    o_ref[...] = (_rms(o, sg_ref[...]) * (1.0 - lambda_init)).astype(o_ref.dtype)


def _diff_attention(qkv, lam, subln_g, lambda_init, b, s):
    tq = _pick(s, 512)
    nq = s // tq
    c0 = 0
    slopes = 2.0 ** (-ALIBI_MAX_BIAS * jnp.arange(1, DA_HEADS + 1, dtype=F32) / DA_HEADS)
    slopes = jnp.broadcast_to(slopes[:, None, None], (DA_HEADS, 1, tq))
    return pl.pallas_call(
        functools.partial(_da_kernel, tq=tq, lambda_init=lambda_init),
        grid=(b, DA_HEADS, nq),
        in_specs=[
            pl.BlockSpec((None, tq, DA_DV), lambda bi, h, i: (bi, i, c0 + h)),
            pl.BlockSpec((None, s, DA_DV), lambda bi, h, i: (bi, 0, c0 + DA_HEADS + h)),
            pl.BlockSpec((None, s, DA_DV), lambda bi, h, i: (bi, 0, c0 + 2 * DA_HEADS + h)),
            pl.BlockSpec((1, 1, tq), lambda bi, h, i: (h, 0, 0)),
            pl.BlockSpec((1, 4, DA_DK), lambda bi, h, i: (0, 0, 0)),
            pl.BlockSpec((1, DA_DV), lambda bi, h, i: (0, 0)),
        ],
        out_specs=pl.BlockSpec((None, tq, DA_DV), lambda bi, h, i: (bi, i, h)),
        out_shape=jax.ShapeDtypeStruct((b, s, DA_WIDTH), BF16),
        scratch_shapes=[
            pltpu.VMEM((s, 2 * DA_DV), BF16),
            pltpu.VMEM((2 * tq, DA_DV), BF16),
            pltpu.VMEM((2 * tq, DA_DV), F32),
            pltpu.VMEM((2 * tq, 2 * DA_DV), F32),
        ],
        compiler_params=_cparams(("parallel", "parallel", "arbitrary")),
    )(qkv, qkv, qkv, slopes, lam.reshape(1, 4, DA_DK), subln_g.reshape(1, DA_DV))


def _rw_prep_kernel(z_ref, zp_ref, mu_ref, wa_ref, w0_ref, a0_ref, gup_ref, kk_ref, ka_ref,
                    rk_ref, hs_ref,
                    lw_o, r_o, k_o, v_o, kk_o, a_o, c3_o, g_o, *, tiles_per_seq):
    tm = z_ref.shape[0]
    z = z_ref[...]
    first = (pl.program_id(0) % tiles_per_seq) == 0
    prev = jnp.where(first, 0.0, zp_ref[SUBLANES - 1:SUBLANES, :])
    row = lax.broadcasted_iota(jnp.int32, z.shape, 0)
    zs = jnp.where(row == 0, prev, pltpu.roll(z, 1, axis=0))
    z = z + mu_ref[...] * (zs - z)

    w3 = RW_WIDTH
    r = z[:, 0:w3]
    k = z[:, w3:2 * w3]
    v = z[:, 2 * w3:3 * w3]
    wd_ad = z[:, 3 * w3:3 * w3 + 128]
    gd = z[:, 3 * w3 + 128:3 * w3 + 256]
    lane = lax.broadcasted_iota(jnp.int32, wd_ad.shape, 1)
    lowrank_in = jnp.where(lane < RW_DECAY_RANK, jnp.tanh(wd_ad), wd_ad)
    wa = jnp.dot(lowrank_in.astype(BF16), wa_ref[...], preferred_element_type=F32)
    w = w0_ref[...] + wa[:, 0:w3]
    nw = -w
    softplus = jnp.maximum(nw, 0.0) + jnp.log(1.0 + jnp.exp(-jnp.abs(nw)))
    log_decay = -jnp.exp(-softplus - 0.5)
    a = jax.nn.sigmoid(a0_ref[...] + wa[:, w3:2 * w3])
    g = jnp.dot(jax.nn.sigmoid(gd).astype(BF16), gup_ref[...], preferred_element_type=F32)

    hs = hs_ref[...]
    kk = k * kk_ref[...]
    kk = kk / jnp.maximum(jnp.sqrt(_gdot(kk * kk, hs)), 1e-12)
    kmod = k * (1.0 + (a - 1.0) * ka_ref[...])

    lw_o[...] = log_decay
    r_o[...] = r
    k_o[...] = kmod
    v_o[...] = v
    kk_o[...] = kk
    a_o[...] = a
    c3_o[...] = _gdot(r * kmod * rk_ref[...], hs)
    g_o[...] = g


def _head_sum_matrix(n, head):
    idx = jnp.arange(n) // head
    return (idx[:, None] == idx[None, :]).astype(BF16)


def _rwkv_prep(proj, p, b, s):
    t = b * s
    tm = _pick(s, 256)
    tiles_per_seq = s // tm
    w3 = RW_WIDTH
    wa = jnp.zeros((128, 2 * w3), F32)
    wa = wa.at[0:RW_DECAY_RANK, 0:w3].set(p["w_up"]).at[RW_DECAY_RANK:128, w3:].set(p["a_up"])
    mu = jnp.pad(p["mu"], (0, RW_PAD - RW_COLS)).reshape(1, RW_PAD)
    row = lambda a: a.reshape(1, w3)
    const = lambda shape: pl.BlockSpec(shape, lambda i: (0, 0))
    out_spec = pl.BlockSpec((tm, w3), lambda i: (i, 0))
    nb8 = tm // SUBLANES
    return pl.pallas_call(
        functools.partial(_rw_prep_kernel, tiles_per_seq=tiles_per_seq),
        grid=(t // tm,),
        in_specs=[
            pl.BlockSpec((tm, RW_PAD), lambda i: (i, 0)),
            pl.BlockSpec((SUBLANES, RW_PAD), lambda i: (jnp.maximum(i * nb8 - 1, 0), 0)),
            const((1, RW_PAD)), const((128, 2 * w3)), const((1, w3)), const((1, w3)),
            const((RW_GATE_RANK, w3)), const((1, w3)), const((1, w3)), const((1, w3)),
            const((w3, w3)),
        ],
        out_specs=[out_spec] * 8,
        out_shape=[jax.ShapeDtypeStruct((t, w3), F32)] * 8,
        compiler_params=_cparams(("parallel",)),
    )(proj, proj, mu, wa.astype(BF16), row(p["w0"]), row(p["a0"]), p["g_up"].astype(BF16),
      row(p["k_k"]), row(p["k_a"]), row(p["r_k"]), _head_sum_matrix(w3, RW_HEAD))


_NT = (((1,), (1,)), ((), ()))
_TN = (((0,), (0,)), ((), ()))
_NN = (((1,), (0,)), ((), ()))


def _mm(a, b, dims=_NN):
    return lax.dot_general(a.astype(BF16), b.astype(BF16), dims, preferred_element_type=F32)


def _rw_chunk_kernel(lw_ref, r_ref, k_ref, v_ref, kk_ref, a_ref, tri_ref, y_ref, st_ref, *, c):
    @pl.when(pl.program_id(1) == 0)
    def _():
        st_ref[...] = jnp.zeros_like(st_ref)

    lanes = 2 * RW_HEAD
    tri = tri_ref[...]
    row = lax.broadcasted_iota(jnp.int32, (c, c), 0)
    col = lax.broadcasted_iota(jnp.int32, (c, c), 1)
    strict = row > col
    incl = row >= col
    eye = (row == col).astype(F32)
    blk = 16
    diag_blocks = (row // blk) == (col // blk)
    lane = lax.broadcasted_iota(jnp.int32, (c, lanes), 1)
    head_mask = (lane < RW_HEAD, lane >= RW_HEAD)
    srow = lax.broadcasted_iota(jnp.int32, (lanes, lanes), 0)
    scol = lax.broadcasted_iota(jnp.int32, (lanes, lanes), 1)
    same_head = (srow // RW_HEAD) == (scol // RW_HEAD)

    def by_head(x):
        return jnp.concatenate([jnp.where(hm, x, 0.0) for hm in head_mask], axis=0)

    npair = RW_HEADS // 2
    pairs = range(npair)
    heads = range(RW_HEADS)
    cols = [pl.ds(p * lanes, lanes) for p in pairs]
    lw = [lw_ref[0, :, cols[p]] for p in pairs]
    parts = [_split3(lw[p]) for p in pairs]
    lc = [sum(jnp.dot(tri, part, preferred_element_type=F32) for part in parts[p])
          for p in pairs]
    kk = [kk_ref[0, :, cols[p]] for p in pairs]
    k = [k_ref[0, :, cols[p]] for p in pairs]
    v = [v_ref[0, :, cols[p]] for p in pairs]
    beta = [kk[p] * a_ref[0, :, cols[p]] for p in pairs]
    a_t = [-kk[p] * jnp.exp(lc[p] - lw[p]) for p in pairs]
    r_t = [r_ref[0, :, cols[p]] * jnp.exp(lc[p]) for p in pairs]
    e_neg = [jnp.exp(-lc[p]) for p in pairs]
    rhs = [jnp.concatenate([beta[p] * e_neg[p], k[p] * e_neg[p]], axis=0) for p in pairs]

    m = [_mm(jnp.concatenate([jnp.where(head_mask[h % 2], a_t[h // 2], 0.0),
                              jnp.where(head_mask[h % 2], r_t[h // 2], 0.0)], axis=0),
             rhs[h // 2], _NT) for h in heads]
    l_mat = [jnp.where(strict, m[h][0:c, 0:c], 0.0) for h in heads]
    a_ak = [jnp.where(strict, m[h][0:c, c:2 * c], 0.0) for h in heads]
    a_rb = [jnp.where(incl, m[h][c:2 * c, 0:c], 0.0) for h in heads]
    a_rk = [jnp.where(incl, m[h][c:2 * c, c:2 * c], 0.0) for h in heads]

    t_inv = [eye + jnp.where(((row // 2) == (col // 2)) & (row % 2 == 1) & (col % 2 == 0),
                             l_mat[h], 0.0) for h in heads]
    size = 2
    while size < c:
        sel = (((row // (2 * size)) == (col // (2 * size)))
               & ((row // size) % 2 == 1) & ((col // size) % 2 == 0))
        po = [_mm(t_inv[h], jnp.where(sel, l_mat[h], 0.0)) for h in heads]
        t_inv = [t_inv[h] + _mm(po[h], t_inv[h]) for h in heads]
        size *= 2

    s0 = [st_ref[p] for p in pairs]
    xr = [_mm(jnp.concatenate([a_t[p], r_t[p]], axis=0), s0[p], _NT) for p in pairs]
    v2 = [by_head(v[p]) for p in pairs]
    x = [xr[p][0:c] + _mm(jnp.concatenate(a_ak[2 * p:2 * p + 2], axis=1), v2[p]) for p in pairs]
    u = [_mm(jnp.concatenate(t_inv[2 * p:2 * p + 2], axis=1), by_head(x[p]))
         for p in pairs]
    for p in pairs:
        y_ref[0, :, cols[p]] = xr[p][c:2 * c] + _mm(
            jnp.concatenate(a_rb[2 * p:2 * p + 2] + a_rk[2 * p:2 * p + 2], axis=1),
            jnp.concatenate([by_head(u[p]), v2[p]], axis=0))
    for p in pairs:
        lc_end = lc[p][c - 1:c, :]
        e_end = jnp.exp(lc_end - lc[p])
        upd = _mm(jnp.concatenate([u[p], v[p]], axis=0),
                  jnp.concatenate([beta[p] * e_end, k[p] * e_end], axis=0), _TN)
        st_ref[p] = jnp.where(same_head, s0[p] * jnp.exp(lc_end) + upd, 0.0)


def _rwkv_chunked(lw, r, k, v, kk, a, b, s):
    c = _pick(s, 128)
    w3 = RW_WIDTH
    r3 = lambda x: x.reshape(b, s, w3)
    spec = pl.BlockSpec((1, c, w3), lambda bi, i: (bi, i, 0))
    tri = (jnp.arange(c)[:, None] >= jnp.arange(c)[None, :]).astype(BF16)
    return pl.pallas_call(
        functools.partial(_rw_chunk_kernel, c=c),
        grid=(b, s // c),
        in_specs=[spec] * 6 + [pl.BlockSpec((c, c), lambda bi, i: (0, 0))],
        out_specs=spec,
        out_shape=jax.ShapeDtypeStruct((b, s, w3), F32),
        scratch_shapes=[pltpu.VMEM((RW_HEADS // 2, 2 * RW_HEAD, 2 * RW_HEAD), F32)],
        compiler_params=_cparams(("parallel", "arbitrary")),
    )(r3(lw), r3(r), r3(k), r3(v), r3(kk), r3(a), tri).reshape(b * s, w3)


def _merge_kernel(x_ref, hg_ref, da_ref, y_ref, v_ref, c3_ref, g_ref, g0_ref, g1_ref, g2_ref,
                  bg_ref, lng_ref, lnb_ref, hs_ref, wb_ref, wo_ref, o_ref):
    hs = hs_ref[...]
    y = y_ref[...]
    mu = _gdot(y, hs) * (1.0 / RW_HEAD)
    yc = y - mu
    var = _gdot(yc * yc, hs) * (1.0 / RW_HEAD)
    yn = yc * lax.rsqrt(var + RW_GN_EPS) * lng_ref[...] + lnb_ref[...]
    o_rw = (yn + c3_ref[...] * v_ref[...]) * g_ref[...]

    merged = None
    for n, (br, gp) in enumerate(((hg_ref[...], g0_ref), (da_ref[...], g1_ref), (o_rw, g2_ref))):
        gate = jax.nn.sigmoid(gp[...] + bg_ref[:, n * D_MODEL:(n + 1) * D_MODEL])
        pb = jnp.dot(br.astype(BF16), wb_ref[n], preferred_element_type=F32)
        merged = gate * pb if merged is None else merged + gate * pb
    o_ref[...] = x_ref[...] + jnp.dot(merged.astype(BF16), wo_ref[...],
                                      preferred_element_type=F32)


def _merge(x, o_hg, o_da, y, v, c3, g, proj, b_gate, ln_g, ln_b, w_branch, w_out):
    t = x.shape[0]
    tm = _pick(t, 512)
    bw = BRANCH_WIDTH
    row = pl.BlockSpec((tm, bw), lambda i: (i, 0))
    g0 = 0
    gate_spec = lambda n: pl.BlockSpec((tm, D_MODEL), lambda i, n=n: (i, g0 + n))
    const = lambda shape: pl.BlockSpec(shape, lambda i: (0,) * len(shape),
                                       pipeline_mode=pl.Buffered(1))
    return pl.pallas_call(
        _merge_kernel,
        grid=(t // tm,),
        in_specs=[pl.BlockSpec((tm, D_MODEL), lambda i: (i, 0))] + [row] * 6
        + [gate_spec(0), gate_spec(1), gate_spec(2),
           const((1, N_BRANCH * D_MODEL)), const((1, bw)), const((1, bw)), const((bw, bw)),
           const((N_BRANCH, bw, D_MODEL)), const((D_MODEL, D_MODEL))],
        out_specs=pl.BlockSpec((tm, D_MODEL), lambda i: (i, 0)),
        out_shape=jax.ShapeDtypeStruct((t, D_MODEL), F32),
        compiler_params=_cparams(("parallel",)),
    )(x, o_hg, o_da, y, v, c3, g, proj, proj, proj, b_gate.reshape(1, -1),
      ln_g.reshape(1, bw), ln_b.reshape(1, bw), _head_sum_matrix(bw, RW_HEAD),
      w_branch.astype(BF16), w_out.astype(BF16))


def _xattn_kernel(x_ref, g_ref, wq_ref, kv_ref, wo_ref, o_ref):
    x = x_ref[...]
    h = _rms(x, g_ref[...]).astype(BF16)
    q = jnp.dot(h, wq_ref[...], preferred_element_type=F32) * (XA_HEAD ** -0.5)
    outs = []
    for hd in range(XA_HEADS):
        qh = q[:, hd * XA_HEAD:(hd + 1) * XA_HEAD].astype(BF16)
        kh = kv_ref[0, :, hd * XA_HEAD:(hd + 1) * XA_HEAD].astype(BF16)
        vh = kv_ref[0, :, D_MODEL + hd * XA_HEAD:D_MODEL + (hd + 1) * XA_HEAD].astype(BF16)
        s = lax.dot_general(qh, kh, (((1,), (1,)), ((), ())), preferred_element_type=F32)
        m = jnp.max(s, axis=-1, keepdims=True)
        p = jnp.exp(s - m)
        p = p / jnp.sum(p, axis=-1, keepdims=True)
        outs.append(jnp.dot(p.astype(BF16), vh, preferred_element_type=F32))
    o = jnp.concatenate(outs, axis=-1).astype(BF16)
    o_ref[...] = x + jnp.dot(o, wo_ref[...], preferred_element_type=F32)


def _cross_attention(x, g, wq, kv, wo, b, s):
    tm = _pick(s, 512)
    nt = s // tm
    m = kv.shape[1]
    return pl.pallas_call(
        _xattn_kernel,
        grid=(b, nt),
        in_specs=[
            pl.BlockSpec((tm, D_MODEL), lambda bi, i: (bi * nt + i, 0)),
            pl.BlockSpec((1, D_MODEL), lambda bi, i: (0, 0)),
            pl.BlockSpec((D_MODEL, D_MODEL), lambda bi, i: (0, 0)),
            pl.BlockSpec((1, m, 2 * D_MODEL), lambda bi, i: (bi, 0, 0)),
            pl.BlockSpec((D_MODEL, D_MODEL), lambda bi, i: (0, 0)),
        ],
        out_specs=pl.BlockSpec((tm, D_MODEL), lambda bi, i: (bi * nt + i, 0)),
        out_shape=jax.ShapeDtypeStruct((b * s, D_MODEL), F32),
        compiler_params=_cparams(("parallel", "parallel")),
    )(x, g.reshape(1, D_MODEL), wq.astype(BF16), kv, wo.astype(BF16))


FFN_HALO = 16
FFN_CHUNK = 256


def _ffn_kernel(x_ref, xp_ref, g_ref, wu_ref, wv_ref, cw_ref, cb_ref, wd_ref, *rest,
                tiles_per_seq, out_norm):
    if out_norm:
        og_ref, o_ref, h_s, act_s = rest
    else:
        o_ref, h_s, act_s = rest
    tm = x_ref.shape[0]
    first = (pl.program_id(0) % tiles_per_seq) == 0
    hp = _rms(xp_ref[...], g_ref[...])
    h_s[0:FFN_HALO, :] = jnp.where(first, 0.0, hp).astype(BF16)
    h_s[FFN_HALO:, :] = _rms(x_ref[...], g_ref[...]).astype(BF16)
    row = lax.broadcasted_iota(jnp.int32, (tm, FFN_CHUNK), 0)
    for c0 in range(0, D_FF, FFN_CHUNK):
        cols = slice(c0, c0 + FFN_CHUNK)
        ue = jnp.dot(h_s[...], wu_ref[:, cols], preferred_element_type=F32)
        vv = jnp.dot(h_s[FFN_HALO:, :], wv_ref[:, cols], preferred_element_type=F32)
        u = ue[FFN_HALO:, :]
        p1 = ue[FFN_HALO - 1:FFN_HALO, :]
        p2 = ue[FFN_HALO - 2:FFN_HALO - 1, :]
        u1 = jnp.where(row == 0, p1, pltpu.roll(u, 1, axis=0))
        u2 = jnp.where(row == 0, p2, jnp.where(row == 1, p1, pltpu.roll(u, 2, axis=0)))
        uc = (cw_ref[0:1, cols] * u2 + cw_ref[1:2, cols] * u1 + cw_ref[2:3, cols] * u
              + cb_ref[:, cols])
        act_s[:, cols] = (_silu(uc) * vv).astype(BF16)
    out = x_ref[...] + jnp.dot(act_s[...], wd_ref[...], preferred_element_type=F32)
    o_ref[...] = _rms(out, og_ref[...]) if out_norm else out


def _ffn(x, g, w_up, conv_w, conv_b, w_down, b, s, out_gain=None):
    t = x.shape[0]
    tm = _pick(s, 512)
    tiles_per_seq = s // tm
    nbh = tm // FFN_HALO
    out_norm = out_gain is not None
    wu = w_up.astype(BF16)
    resident = lambda shape, idx: pl.BlockSpec(shape, lambda i: idx, pipeline_mode=pl.Buffered(1))
    return pl.pallas_call(
        functools.partial(_ffn_kernel, tiles_per_seq=tiles_per_seq, out_norm=out_norm),
        grid=(t // tm,),
        in_specs=[
            pl.BlockSpec((tm, D_MODEL), lambda i: (i, 0)),
            pl.BlockSpec((FFN_HALO, D_MODEL), lambda i: (jnp.maximum(i * nbh - 1, 0), 0)),
            resident((1, D_MODEL), (0, 0)),
            resident((D_MODEL, D_FF), (0, 0)),
            resident((D_MODEL, D_FF), (0, 1)),
            resident((CONV_W, D_FF), (0, 0)),
            resident((1, D_FF), (0, 0)),
            resident((D_FF, D_MODEL), (0, 0)),
        ] + ([resident((1, D_MODEL), (0, 0))] if out_norm else []),
        out_specs=pl.BlockSpec((tm, D_MODEL), lambda i: (i, 0)),
        out_shape=jax.ShapeDtypeStruct((t, D_MODEL), F32),
        scratch_shapes=[pltpu.VMEM((tm + FFN_HALO, D_MODEL), BF16), pltpu.VMEM((tm, D_FF), BF16)],
        compiler_params=_cparams(("parallel",)),
    )(x, x, g.reshape(1, D_MODEL), wu, wu, conv_w, conv_b.reshape(1, D_FF),
      w_down.astype(BF16), *([out_gain.reshape(1, D_MODEL)] if out_norm else []))


def _permute_w_in(w):
    d = w.shape[0]
    return jnp.concatenate([
        w[:, R_OFF_RW:R_OFF_GATE], jnp.zeros((d, RW_PAD - RW_COLS), w.dtype),
        w[:, 0:R_OFF_DA], w[:, R_OFF_GATE:], w[:, R_OFF_DA:R_OFF_RW]], axis=1).astype(BF16)


def kernel(x, mem, norm_mix_g, w_in, b_gate, hgrn_lb_param, hgrn_norm_g, diff_lambda, diff_subln_g, rwkv_mu, rwkv_w0, rwkv_w_up, rwkv_a0, rwkv_a_up, rwkv_g_up, rwkv_k_k, rwkv_k_a, rwkv_r_k, rwkv_ln_g, rwkv_ln_b, w_branch, w_out, norm_xa_g, norm_mem_g, xa_wq, xa_wkv, xa_wo, norm_ffn_g, ffn_w_up, ffn_conv_w, ffn_conv_b, ffn_w_down, final_norm_g):
    b, s, d = x.shape
    depth = w_in.shape[0]
    m = mem.shape[1]
    t = b * s
    lb_p = jax.nn.softmax(hgrn_lb_param.astype(F32), axis=0)
    lower_bounds = jnp.cumsum(lb_p, axis=0) - lb_p[0]

    xf = x.reshape(t, d)
    memf = mem.reshape(b * m, d)
    for l in range(depth):
        p_rw, p_hg, p_gate, p_da = _in_proj(xf, norm_mix_g[l], _permute_w_in(w_in[l]))
        o_hg = _hgrn2(p_hg, lower_bounds[l], hgrn_norm_g[l], b, s)
        lambda_init = 0.8 - 0.6 * math.exp(-0.3 * l)
        o_da = _diff_attention(p_da.reshape(b, s, 3 * DA_WIDTH), diff_lambda[l], diff_subln_g[l],
                               lambda_init, b, s).reshape(t, DA_WIDTH)
        rw_params = dict(mu=rwkv_mu[l], w0=rwkv_w0[l], w_up=rwkv_w_up[l], a0=rwkv_a0[l],
                         a_up=rwkv_a_up[l], g_up=rwkv_g_up[l], k_k=rwkv_k_k[l], k_a=rwkv_k_a[l],
                         r_k=rwkv_r_k[l].reshape(-1))
        lw, r, k, v, kk, a, c3, g = _rwkv_prep(p_rw, rw_params, b, s)
        y = _rwkv_chunked(lw, r, k, v, kk, a, b, s)
        xf = _merge(xf, o_hg, o_da, y, v, c3, g, p_gate, b_gate[l], rwkv_ln_g[l], rwkv_ln_b[l],
                    w_branch[l], w_out[l])
        kv = _norm_matmul(memf, norm_mem_g[l], xa_wkv[l].astype(BF16), 256, 512)
        xf = _cross_attention(xf, norm_xa_g[l], xa_wq[l], kv.reshape(b, m, 2 * d), xa_wo[l], b, s)
        xf = _ffn(xf, norm_ffn_g[l], ffn_w_up[l], ffn_conv_w[l], ffn_conv_b[l], ffn_w_down[l], b, s,
                  out_gain=final_norm_g if l == depth - 1 else None)
    return xf.reshape(b, s, d)
```

```python
import functools
import math

import jax
import jax.numpy as jnp
from jax import lax
from jax.experimental import pallas as pl
from jax.experimental.pallas import tpu as pltpu

F32 = jnp.float32
BF16 = jnp.bfloat16

D_MODEL = 1024
NORM_EPS = 1e-6
MASK_VALUE = -1e30
TINY = 1e-30

HG_HEADS = 4
HG_D = 128
HG_WIDTH = HG_HEADS * HG_D
HG_SUB = 16

DA_HEADS = 4
DA_DK = 64
DA_DV = 128
DA_WIDTH = DA_HEADS * DA_DV
ALIBI_MAX_BIAS = 8.0
LOG2E = 1.4426950408889634
DA_ROW_BLOCK = 128
DA_UNROLL = 4

RW_HEADS = 8
RW_HEAD = 64
RW_WIDTH = RW_HEADS * RW_HEAD
RW_DECAY_RANK = 64
RW_A_RANK = 64
RW_GATE_RANK = 128
RW_COLS = 3 * RW_WIDTH + RW_DECAY_RANK + RW_A_RANK + RW_GATE_RANK
RW_GN_EPS = 64e-5
RW_CHUNKS_PER_STEP = 2

N_BRANCH = 3
BRANCH_WIDTH = 512

XA_HEADS = 4
XA_HEAD = D_MODEL // XA_HEADS

D_FF = 2816
CONV_W = 3

RW_PAD = 2048
P_OFF_RW = 0
P_OFF_HG = RW_PAD
P_OFF_GATE = P_OFF_HG + 4 * HG_WIDTH
P_OFF_DA = P_OFF_GATE + N_BRANCH * D_MODEL
N_PROJ = P_OFF_DA + 3 * DA_WIDTH

R_OFF_DA = 4 * HG_WIDTH
R_OFF_RW = R_OFF_DA + 3 * DA_WIDTH
R_OFF_GATE = R_OFF_RW + RW_COLS

V7X_VMEM_LIMIT = 56 * 1024 * 1024
SUBLANES = 8


def _cparams(sem):
    return pltpu.CompilerParams(dimension_semantics=sem, vmem_limit_bytes=V7X_VMEM_LIMIT)


def _pick(n, pref):
    t = min(n, pref)
    while n % t:
        t //= 2
    return t


def _rms(x, g):
    ms = jnp.mean(x * x, axis=-1, keepdims=True)
    return x * lax.rsqrt(ms + NORM_EPS) * g


def _split3(x):
    hi = x.astype(BF16)
    r1 = x - hi.astype(F32)
    mid = r1.astype(BF16)
    lo = (r1 - mid.astype(F32)).astype(BF16)
    return hi, mid, lo


def _gdot(x, g):
    hi, lo, _ = _split3(x)
    return (jnp.dot(hi, g, preferred_element_type=F32)
            + jnp.dot(lo, g, preferred_element_type=F32))


def _silu(x):
    return x * jax.nn.sigmoid(x)


def _norm_mm_kernel(x_ref, g_ref, w_ref, o_ref, h_ref):
    @pl.when(pl.program_id(1) == 0)
    def _():
        h_ref[...] = _rms(x_ref[...], g_ref[...]).astype(BF16)

    o_ref[...] = jnp.dot(h_ref[...], w_ref[...], preferred_element_type=F32)


def _norm_matmul(x, g, w, tm, tn):
    t, d = x.shape
    n = w.shape[1]
    tm = _pick(t, tm)
    tn = _pick(n, tn)
    return pl.pallas_call(
        _norm_mm_kernel,
        grid=(t // tm, n // tn),
        in_specs=[
            pl.BlockSpec((tm, d), lambda i, j: (i, 0)),
            pl.BlockSpec((1, d), lambda i, j: (0, 0)),
            pl.BlockSpec((d, tn), lambda i, j: (0, j)),
        ],
        out_specs=pl.BlockSpec((tm, tn), lambda i, j: (i, j)),
        out_shape=jax.ShapeDtypeStruct((t, n), F32),
        scratch_shapes=[pltpu.VMEM((tm, d), BF16)],
        compiler_params=_cparams(("parallel", "arbitrary")),
    )(x, g.reshape(1, d), w)


PROJ_SEGMENTS = ((P_OFF_RW, RW_PAD, F32), (P_OFF_HG, 4 * HG_WIDTH, F32),
                 (P_OFF_GATE, N_BRANCH * D_MODEL, F32), (P_OFF_DA, 3 * DA_WIDTH, BF16))
MXU_N = 256


def _in_proj_kernel(x_ref, g_ref, w_ref, *outs):
    h = _rms(x_ref[...], g_ref[...]).astype(BF16)
    step = 2 * MXU_N
    for o_ref, (off, width, dt) in zip(outs, PROJ_SEGMENTS):
        for c0 in range(0, width, step):
            o_ref[:, c0:c0 + step] = jnp.dot(h, w_ref[:, off + c0:off + c0 + step],
                                             preferred_element_type=F32).astype(dt)


def _in_proj(x, g, w):
    t, d = x.shape
    tm = _pick(t, 256)
    return pl.pallas_call(
        _in_proj_kernel,
        grid=(t // tm,),
        in_specs=[
            pl.BlockSpec((tm, d), lambda i: (i, 0)),
            pl.BlockSpec((1, d), lambda i: (0, 0)),
            pl.BlockSpec((d, N_PROJ), lambda i: (0, 0), pipeline_mode=pl.Buffered(1)),
        ],
        out_specs=[pl.BlockSpec((tm, width), lambda i: (i, 0)) for _, width, _ in PROJ_SEGMENTS],
        out_shape=[jax.ShapeDtypeStruct((t, width), dt) for _, width, dt in PROJ_SEGMENTS],
        compiler_params=_cparams(("parallel",)),
    )(x, g.reshape(1, d), w)


HG_GROUP = 2


def _hgrn_kernel(q_ref, f_ref, i_ref, g_ref, lb_ref, ng_ref, o_ref,
                 st_ref, qs_s, b_s, k_s, v_s, o_s, *, ts):
    @pl.when(pl.program_id(2) == 0)
    def _():
        st_ref[...] = jnp.zeros_like(st_ref)

    hs = range(HG_GROUP)
    r16 = lax.broadcasted_iota(jnp.int32, (ts, HG_D), 0) % HG_SUB
    for hh in hs:
        cols = slice(hh * HG_D, (hh + 1) * HG_D)
        lb = lb_ref[:, cols]
        fp = f_ref[:, cols]
        f = lb + (1.0 - lb) * jax.nn.sigmoid(fp)
        k_s[hh] = (1.0 - lb) * jax.nn.sigmoid(-fp)
        qs_s[hh] = _silu(q_ref[:, cols])
        v_s[hh] = i_ref[:, cols]
        b = jnp.log(jnp.maximum(f, TINY))
        sh = 1
        while sh < HG_SUB:
            b = b + jnp.where(r16 >= sh, pltpu.roll(b, sh, axis=0), 0.0)
            sh *= 2
        b_s[hh] = b * LOG2E

    rowi = lax.broadcasted_iota(jnp.int32, (HG_SUB, HG_D), 0)

    def body(ci, carry):
        base = pl.multiple_of(ci * HG_SUB, HG_SUB)
        rows = pl.ds(base, HG_SUB)
        qc = [qs_s[hh, rows, :] for hh in hs]
        bc = [b_s[hh, rows, :] for hh in hs]
        st = [st_ref[hh] for hh in hs]
        zero = jnp.zeros((HG_D, HG_D), BF16)
        w_bd = jnp.concatenate(
            [jnp.concatenate([st[hh].astype(BF16) if c == hh else zero for c in hs], axis=1)
             for hh in hs], axis=0)
        qe = jnp.concatenate([(qc[hh] * jnp.exp2(bc[hh])).astype(BF16) for hh in hs], axis=1)
        o_cat = lax.dot_general(qe, w_bd, (((1,), (1,)), ((), ())), preferred_element_type=F32)
        o = [o_cat[:, hh * HG_D:(hh + 1) * HG_D] for hh in hs]
        for j in range(HG_SUB):
            for hh in hs:
                bj = b_s[hh, pl.ds(base + j, 1), :]
                kj = k_s[hh, pl.ds(base + j, 1), :]
                vj = v_s[hh, pl.ds(base + j, 1), :]
                dec = jnp.exp2(jnp.minimum(bc[hh] - bj, 0.0))
                sc = jnp.where(rowi >= j,
                               jnp.sum(qc[hh] * dec * kj, axis=-1, keepdims=True), 0.0)
                o[hh] = o[hh] + sc * vj
        bl = [b_s[hh, pl.ds(base + HG_SUB - 1, 1), :] for hh in hs]
        kd = jnp.concatenate([(k_s[hh, rows, :] * jnp.exp2(bl[hh] - bc[hh])).astype(BF16)
                              for hh in hs], axis=1)
        vc = jnp.concatenate([v_s[hh, rows, :].astype(BF16) for hh in hs], axis=1)
        upd = lax.dot_general(vc, kd, (((0,), (0,)), ((), ())), preferred_element_type=F32)
        for hh in hs:
            o_s[hh, rows, :] = o[hh]
            blk = slice(hh * HG_D, (hh + 1) * HG_D)
            st_ref[hh] = st[hh] * jnp.exp2(bl[hh]) + upd[blk, blk]
        return carry

    lax.fori_loop(0, ts // HG_SUB, body, 0, unroll=8)

    for hh in hs:
        cols = slice(hh * HG_D, (hh + 1) * HG_D)
        o = _rms(o_s[hh], ng_ref[:, cols])
        o_ref[:, cols] = (o * _silu(g_ref[:, cols])).astype(o_ref.dtype)


def _hgrn2(proj, lb, norm_g, b, s):
    ts = _pick(s, 256)
    nt = s // ts
    width = HG_GROUP * HG_D
    groups = HG_HEADS // HG_GROUP

    def col(section):
        return pl.BlockSpec((ts, width),
                            lambda bi, h, c, section=section: (bi * nt + c, section * groups + h))

    par = pl.BlockSpec((1, width), lambda bi, h, c: (0, h))
    return pl.pallas_call(
        functools.partial(_hgrn_kernel, ts=ts),
        grid=(b, groups, nt),
        in_specs=[col(0), col(1), col(2), col(3), par, par],
        out_specs=pl.BlockSpec((ts, width), lambda bi, h, c: (bi * nt + c, h)),
        out_shape=jax.ShapeDtypeStruct((b * s, HG_WIDTH), BF16),
        scratch_shapes=[pltpu.VMEM((HG_GROUP, HG_D, HG_D), F32)]
        + [pltpu.VMEM((HG_GROUP, ts, HG_D), F32)] * 5,
        compiler_params=_cparams(("parallel", "parallel", "arbitrary")),
    )(proj, proj, proj, proj, lb.reshape(1, HG_WIDTH), norm_g.reshape(1, HG_WIDTH))


def _da_kernel(q_ref, k_ref, v_ref, sl_ref, lam_ref, sg_ref, o_ref,
               vb_s, q2_s, m_s, acc_s, *, tq, lambda_init):
    i = pl.program_id(2)
    nq = pl.num_programs(2)
    lanes = DA_DV

    @pl.when(i == 0)
    def _():
        def fill(c, carry):
            rows = pl.ds(pl.multiple_of(c * tq, tq), tq)
            vb_s[rows, 0:lanes] = v_ref[rows, :]
            vb_s[rows, lanes:2 * lanes] = jnp.ones((tq, lanes), BF16)
            return carry
        lax.fori_loop(0, nq, fill, 0)

    q = q_ref[...].astype(F32) * (DA_DK ** -0.5 * LOG2E)
    lane = lax.broadcasted_iota(jnp.int32, (tq, DA_DV), 1)
    q2_s[pl.ds(0, tq), :] = jnp.where(lane < DA_DK, q, 0.0).astype(BF16)
    q2_s[pl.ds(tq, tq), :] = jnp.where(lane >= DA_DK, q, 0.0).astype(BF16)
    m_s[...] = jnp.full_like(m_s, MASK_VALUE)
    acc_s[...] = jnp.zeros_like(acc_s)

    slope = sl_ref[0] * LOG2E
    colf = lax.broadcasted_iota(jnp.int32, (1, tq), 1).astype(F32)
    rb = min(DA_ROW_BLOCK, tq)

    def step(j, masked):
        start = pl.multiple_of(j * tq, tq)
        kb = k_ref[pl.ds(start, tq), :]
        vb = vb_s[pl.ds(start, tq), :]
        bias = slope * (colf + ((j - i) * tq).astype(F32))
        for r0 in range(0, 2 * tq, rb):
            rows = pl.ds(r0, rb)
            s = lax.dot_general(q2_s[rows, :], kb, (((1,), (1,)), ((), ())),
                                preferred_element_type=F32) + bias
            if masked:
                r = lax.broadcasted_iota(jnp.int32, (rb, tq), 0) + (r0 % tq)
                c = lax.broadcasted_iota(jnp.int32, (rb, tq), 1)
                s = jnp.where(c <= r, s, MASK_VALUE)
            m_old = m_s[rows, :]
            m_new = jnp.maximum(m_old, jnp.max(s, axis=-1, keepdims=True))
            alpha = jnp.exp2(m_old - m_new)
            p = jnp.exp2(s - pltpu.repeat(m_new, tq // lanes, axis=1))
            acc_s[rows, :] = (pltpu.repeat(alpha, 2, axis=1) * acc_s[rows, :]
                              + jnp.dot(p.astype(BF16), vb, preferred_element_type=F32))
            m_s[rows, :] = m_new

    def body_many(jj, carry):
        for u in range(DA_UNROLL):
            step(DA_UNROLL * jj + u, False)
        return carry

    full = i // DA_UNROLL
    lax.fori_loop(0, full, body_many, 0)
    for rem in range(DA_UNROLL):
        @pl.when(i % DA_UNROLL == rem)
        def _(rem=rem):
            for u in range(rem):
                step(full * DA_UNROLL + u, False)
            step(i, True)

    lam = lam_ref[0]
    e1 = jnp.exp(jnp.sum(lam[0:1] * lam[1:2], axis=-1, keepdims=True))
    e2 = jnp.exp(jnp.sum(lam[2:3] * lam[3:4], axis=-1, keepdims=True))
    lam_full = e1 - e2 + lambda_init
    o1 = acc_s[pl.ds(0, tq), 0:lanes] / acc_s[pl.ds(0, tq), lanes:2 * lanes]
    o2 = acc_s[pl.ds(tq, tq), 0:lanes] / acc_s[pl.ds(tq, tq), lanes:2 * lanes]
    o = o1 - lam_full * o2
    o_ref[...] = (_rms(o, sg_ref[...]) * (1.0 - lambda_init)).astype(o_ref.dtype)


def _diff_attention(qkv, lam, subln_g, lambda_init, b, s):
    tq = _pick(s, 512)
    nq = s // tq
    c0 = 0
    slopes = 2.0 ** (-ALIBI_MAX_BIAS * jnp.arange(1, DA_HEADS + 1, dtype=F32) / DA_HEADS)
    slopes = jnp.broadcast_to(slopes[:, None, None], (DA_HEADS, 1, tq))
    return pl.pallas_call(
        functools.partial(_da_kernel, tq=tq, lambda_init=lambda_init),
        grid=(b, DA_HEADS, nq),
        in_specs=[
            pl.BlockSpec((None, tq, DA_DV), lambda bi, h, i: (bi, i, c0 + h)),
            pl.BlockSpec((None, s, DA_DV), lambda bi, h, i: (bi, 0, c0 + DA_HEADS + h)),
            pl.BlockSpec((None, s, DA_DV), lambda bi, h, i: (bi, 0, c0 + 2 * DA_HEADS + h)),
            pl.BlockSpec((1, 1, tq), lambda bi, h, i: (h, 0, 0)),
            pl.BlockSpec((1, 4, DA_DK), lambda bi, h, i: (0, 0, 0)),
            pl.BlockSpec((1, DA_DV), lambda bi, h, i: (0, 0)),
        ],
        out_specs=pl.BlockSpec((None, tq, DA_DV), lambda bi, h, i: (bi, i, h)),
        out_shape=jax.ShapeDtypeStruct((b, s, DA_WIDTH), BF16),
        scratch_shapes=[
            pltpu.VMEM((s, 2 * DA_DV), BF16),
            pltpu.VMEM((2 * tq, DA_DV), BF16),
            pltpu.VMEM((2 * tq, DA_DV), F32),
            pltpu.VMEM((2 * tq, 2 * DA_DV), F32),
        ],
        compiler_params=_cparams(("parallel", "parallel", "arbitrary")),
    )(qkv, qkv, qkv, slopes, lam.reshape(1, 4, DA_DK), subln_g.reshape(1, DA_DV))


def _rw_prep_kernel(z_ref, zp_ref, mu_ref, wa_ref, w0_ref, a0_ref, gup_ref, kk_ref, ka_ref,
                    rk_ref, hs_ref,
                    lw_o, r_o, k_o, v_o, kk_o, a_o, c3_o, g_o, *, tiles_per_seq):
    tm = z_ref.shape[0]
    z = z_ref[...]
    first = (pl.program_id(0) % tiles_per_seq) == 0
    prev = jnp.where(first, 0.0, zp_ref[SUBLANES - 1:SUBLANES, :])
    row = lax.broadcasted_iota(jnp.int32, z.shape, 0)
    zs = jnp.where(row == 0, prev, pltpu.roll(z, 1, axis=0))
    z = z + mu_ref[...] * (zs - z)

    w3 = RW_WIDTH
    r = z[:, 0:w3]
    k = z[:, w3:2 * w3]
    v = z[:, 2 * w3:3 * w3]
    wd_ad = z[:, 3 * w3:3 * w3 + 128]
    gd = z[:, 3 * w3 + 128:3 * w3 + 256]
    lane = lax.broadcasted_iota(jnp.int32, wd_ad.shape, 1)
    lowrank_in = jnp.where(lane < RW_DECAY_RANK, jnp.tanh(wd_ad), wd_ad)
    wa = jnp.dot(lowrank_in.astype(BF16), wa_ref[...], preferred_element_type=F32)
    w = w0_ref[...] + wa[:, 0:w3]
    nw = -w
    softplus = jnp.maximum(nw, 0.0) + jnp.log(1.0 + jnp.exp(-jnp.abs(nw)))
    log_decay = -jnp.exp(-softplus - 0.5)
    a = jax.nn.sigmoid(a0_ref[...] + wa[:, w3:2 * w3])
    g = jnp.dot(jax.nn.sigmoid(gd).astype(BF16), gup_ref[...], preferred_element_type=F32)

    hs = hs_ref[...]
    kk = k * kk_ref[...]
    kk = kk / jnp.maximum(jnp.sqrt(_gdot(kk * kk, hs)), 1e-12)
    kmod = k * (1.0 + (a - 1.0) * ka_ref[...])

    lw_o[...] = log_decay
    r_o[...] = r
    k_o[...] = kmod
    v_o[...] = v
    kk_o[...] = kk
    a_o[...] = a
    c3_o[...] = _gdot(r * kmod * rk_ref[...], hs)
    g_o[...] = g


def _head_sum_matrix(n, head):
    idx = jnp.arange(n) // head
    return (idx[:, None] == idx[None, :]).astype(BF16)


def _rwkv_prep(proj, p, b, s):
    t = b * s
    tm = _pick(s, 256)
    tiles_per_seq = s // tm
    w3 = RW_WIDTH
    wa = jnp.zeros((128, 2 * w3), F32)
    wa = wa.at[0:RW_DECAY_RANK, 0:w3].set(p["w_up"]).at[RW_DECAY_RANK:128, w3:].set(p["a_up"])
    mu = jnp.pad(p["mu"], (0, RW_PAD - RW_COLS)).reshape(1, RW_PAD)
    row = lambda a: a.reshape(1, w3)
    const = lambda shape: pl.BlockSpec(shape, lambda i: (0, 0))
    out_spec = pl.BlockSpec((tm, w3), lambda i: (i, 0))
    nb8 = tm // SUBLANES
    return pl.pallas_call(
        functools.partial(_rw_prep_kernel, tiles_per_seq=tiles_per_seq),
        grid=(t // tm,),
        in_specs=[
            pl.BlockSpec((tm, RW_PAD), lambda i: (i, 0)),
            pl.BlockSpec((SUBLANES, RW_PAD), lambda i: (jnp.maximum(i * nb8 - 1, 0), 0)),
            const((1, RW_PAD)), const((128, 2 * w3)), const((1, w3)), const((1, w3)),
            const((RW_GATE_RANK, w3)), const((1, w3)), const((1, w3)), const((1, w3)),
            const((w3, w3)),
        ],
        out_specs=[out_spec] * 8,
        out_shape=[jax.ShapeDtypeStruct((t, w3), F32)] * 8,
        compiler_params=_cparams(("parallel",)),
    )(proj, proj, mu, wa.astype(BF16), row(p["w0"]), row(p["a0"]), p["g_up"].astype(BF16),
      row(p["k_k"]), row(p["k_a"]), row(p["r_k"]), _head_sum_matrix(w3, RW_HEAD))


_NT = (((1,), (1,)), ((), ()))
_TN = (((0,), (0,)), ((), ()))
_NN = (((1,), (0,)), ((), ()))


def _mm(a, b, dims=_NN):
    return lax.dot_general(a.astype(BF16), b.astype(BF16), dims, preferred_element_type=F32)


def _rw_chunk_kernel(lw_ref, r_ref, k_ref, v_ref, kk_ref, a_ref, tri_ref, y_ref, st_ref, *,
                     c, nc):
    @pl.when(pl.program_id(1) == 0)
    def _():
        st_ref[...] = jnp.zeros_like(st_ref)

    lanes = 2 * RW_HEAD
    tri = tri_ref[...]
    row = lax.broadcasted_iota(jnp.int32, (c, c), 0)
    col = lax.broadcasted_iota(jnp.int32, (c, c), 1)
    strict = row > col
    incl = row >= col
    eye = (row == col).astype(F32)
    blk = 16
    diag_blocks = (row // blk) == (col // blk)
    lane = lax.broadcasted_iota(jnp.int32, (c, lanes), 1)
    head_mask = (lane < RW_HEAD, lane >= RW_HEAD)
    srow = lax.broadcasted_iota(jnp.int32, (lanes, lanes), 0)
    scol = lax.broadcasted_iota(jnp.int32, (lanes, lanes), 1)
    same_head = (srow // RW_HEAD) == (scol // RW_HEAD)

    def by_head(x):
        return jnp.concatenate([jnp.where(hm, x, 0.0) for hm in head_mask], axis=0)

    npair = RW_HEADS // 2
    units = range(nc * npair)
    heads = range(nc * RW_HEADS)
    rows = [pl.ds((q // npair) * c, c) for q in units]
    cols = [pl.ds((q % npair) * lanes, lanes) for q in units]
    lw = [lw_ref[0, rows[q], cols[q]] for q in units]
    parts = [_split3(lw[q]) for q in units]
    lc = [sum(jnp.dot(tri, part, preferred_element_type=F32) for part in parts[q])
          for q in units]
    kk = [kk_ref[0, rows[q], cols[q]] for q in units]
    k = [k_ref[0, rows[q], cols[q]] for q in units]
    v = [v_ref[0, rows[q], cols[q]] for q in units]
    beta = [kk[q] * a_ref[0, rows[q], cols[q]] for q in units]
    a_t = [-kk[q] * jnp.exp(lc[q] - lw[q]) for q in units]
    r_t = [r_ref[0, rows[q], cols[q]] * jnp.exp(lc[q]) for q in units]
    e_neg = [jnp.exp(-lc[q]) for q in units]
    rhs = [jnp.concatenate([beta[q] * e_neg[q], k[q] * e_neg[q]], axis=0) for q in units]

    m = [_mm(jnp.concatenate([jnp.where(head_mask[g % 2], a_t[g // 2], 0.0),
                              jnp.where(head_mask[g % 2], r_t[g // 2], 0.0)], axis=0),
             rhs[g // 2], _NT) for g in heads]
    l_mat = [jnp.where(strict, m[g][0:c, 0:c], 0.0) for g in heads]
    a_ak = [jnp.where(strict, m[g][0:c, c:2 * c], 0.0) for g in heads]
    a_rb = [jnp.where(incl, m[g][c:2 * c, 0:c], 0.0) for g in heads]
    a_rk = [jnp.where(incl, m[g][c:2 * c, c:2 * c], 0.0) for g in heads]

    t_inv = [eye + jnp.where(((row // 2) == (col // 2)) & (row % 2 == 1) & (col % 2 == 0),
                             l_mat[g], 0.0) for g in heads]
    size = 2
    while size < c:
        sel = (((row // (2 * size)) == (col // (2 * size)))
               & ((row // size) % 2 == 1) & ((col // size) % 2 == 0))
        po = [_mm(t_inv[g], jnp.where(sel, l_mat[g], 0.0)) for g in heads]
        t_inv = [t_inv[g] + _mm(po[g], t_inv[g]) for g in heads]
        size *= 2

    state = [st_ref[p] for p in range(npair)]
    for ci in range(nc):
        qs = range(ci * npair, (ci + 1) * npair)
        s0 = {q: state[q % npair] for q in qs}
        xr = {q: _mm(jnp.concatenate([a_t[q], r_t[q]], axis=0), s0[q], _NT) for q in qs}
        v2 = {q: by_head(v[q]) for q in qs}
        x = {q: xr[q][0:c] + _mm(jnp.concatenate(a_ak[2 * q:2 * q + 2], axis=1), v2[q])
             for q in qs}
        u = {q: _mm(jnp.concatenate(t_inv[2 * q:2 * q + 2], axis=1), by_head(x[q])) for q in qs}
        for q in qs:
            y_ref[0, rows[q], cols[q]] = xr[q][c:2 * c] + _mm(
                jnp.concatenate(a_rb[2 * q:2 * q + 2] + a_rk[2 * q:2 * q + 2], axis=1),
                jnp.concatenate([by_head(u[q]), v2[q]], axis=0))
        for q in qs:
            lc_end = lc[q][c - 1:c, :]
            e_end = jnp.exp(lc_end - lc[q])
            upd = _mm(jnp.concatenate([u[q], v[q]], axis=0),
                      jnp.concatenate([beta[q] * e_end, k[q] * e_end], axis=0), _TN)
            state[q % npair] = jnp.where(same_head, s0[q] * jnp.exp(lc_end) + upd, 0.0)
    for p in range(npair):
        st_ref[p] = state[p]


def _rwkv_chunked(lw, r, k, v, kk, a, b, s):
    c = _pick(s, 128)
    nc = RW_CHUNKS_PER_STEP if s % (RW_CHUNKS_PER_STEP * c) == 0 else 1
    w3 = RW_WIDTH
    r3 = lambda x: x.reshape(b, s, w3)
    spec = pl.BlockSpec((1, nc * c, w3), lambda bi, i: (bi, i, 0))
    tri = (jnp.arange(c)[:, None] >= jnp.arange(c)[None, :]).astype(BF16)
    return pl.pallas_call(
        functools.partial(_rw_chunk_kernel, c=c, nc=nc),
        grid=(b, s // (nc * c)),
        in_specs=[spec] * 6 + [pl.BlockSpec((c, c), lambda bi, i: (0, 0))],
        out_specs=spec,
        out_shape=jax.ShapeDtypeStruct((b, s, w3), F32),
        scratch_shapes=[pltpu.VMEM((RW_HEADS // 2, 2 * RW_HEAD, 2 * RW_HEAD), F32)],
        compiler_params=_cparams(("parallel", "arbitrary")),
    )(r3(lw), r3(r), r3(k), r3(v), r3(kk), r3(a), tri).reshape(b * s, w3)


def _merge_kernel(x_ref, hg_ref, da_ref, y_ref, v_ref, c3_ref, g_ref, g0_ref, g1_ref, g2_ref,
                  bg_ref, lng_ref, lnb_ref, hs_ref, wb_ref, wo_ref, o_ref):
    hs = hs_ref[...]
    y = y_ref[...]
    mu = _gdot(y, hs) * (1.0 / RW_HEAD)
    yc = y - mu
    var = _gdot(yc * yc, hs) * (1.0 / RW_HEAD)
    yn = yc * lax.rsqrt(var + RW_GN_EPS) * lng_ref[...] + lnb_ref[...]
    o_rw = (yn + c3_ref[...] * v_ref[...]) * g_ref[...]

    merged = None
    for n, (br, gp) in enumerate(((hg_ref[...], g0_ref), (da_ref[...], g1_ref), (o_rw, g2_ref))):
        gate = jax.nn.sigmoid(gp[...] + bg_ref[:, n * D_MODEL:(n + 1) * D_MODEL])
        pb = jnp.dot(br.astype(BF16), wb_ref[n], preferred_element_type=F32)
        merged = gate * pb if merged is None else merged + gate * pb
    o_ref[...] = x_ref[...] + jnp.dot(merged.astype(BF16), wo_ref[...],
                                      preferred_element_type=F32)


def _merge(x, o_hg, o_da, y, v, c3, g, proj, b_gate, ln_g, ln_b, w_branch, w_out):
    t = x.shape[0]
    tm = _pick(t, 512)
    bw = BRANCH_WIDTH
    row = pl.BlockSpec((tm, bw), lambda i: (i, 0))
    g0 = 0
    gate_spec = lambda n: pl.BlockSpec((tm, D_MODEL), lambda i, n=n: (i, g0 + n))
    const = lambda shape: pl.BlockSpec(shape, lambda i: (0,) * len(shape),
                                       pipeline_mode=pl.Buffered(1))
    return pl.pallas_call(
        _merge_kernel,
        grid=(t // tm,),
        in_specs=[pl.BlockSpec((tm, D_MODEL), lambda i: (i, 0))] + [row] * 6
        + [gate_spec(0), gate_spec(1), gate_spec(2),
           const((1, N_BRANCH * D_MODEL)), const((1, bw)), const((1, bw)), const((bw, bw)),
           const((N_BRANCH, bw, D_MODEL)), const((D_MODEL, D_MODEL))],
        out_specs=pl.BlockSpec((tm, D_MODEL), lambda i: (i, 0)),
        out_shape=jax.ShapeDtypeStruct((t, D_MODEL), F32),
        compiler_params=_cparams(("parallel",)),
    )(x, o_hg, o_da, y, v, c3, g, proj, proj, proj, b_gate.reshape(1, -1),
      ln_g.reshape(1, bw), ln_b.reshape(1, bw), _head_sum_matrix(bw, RW_HEAD),
      w_branch.astype(BF16), w_out.astype(BF16))


def _xattn_kernel(x_ref, g_ref, wq_ref, kv_ref, wo_ref, o_ref):
    x = x_ref[...]
    h = _rms(x, g_ref[...]).astype(BF16)
    q = jnp.dot(h, wq_ref[...], preferred_element_type=F32) * (XA_HEAD ** -0.5)
    outs = []
    for hd in range(XA_HEADS):
        qh = q[:, hd * XA_HEAD:(hd + 1) * XA_HEAD].astype(BF16)
        kh = kv_ref[0, :, hd * XA_HEAD:(hd + 1) * XA_HEAD].astype(BF16)
        vh = kv_ref[0, :, D_MODEL + hd * XA_HEAD:D_MODEL + (hd + 1) * XA_HEAD].astype(BF16)
        s = lax.dot_general(qh, kh, (((1,), (1,)), ((), ())), preferred_element_type=F32)
        m = jnp.max(s, axis=-1, keepdims=True)
        p = jnp.exp(s - m)
        p = p / jnp.sum(p, axis=-1, keepdims=True)
        outs.append(jnp.dot(p.astype(BF16), vh, preferred_element_type=F32))
    o = jnp.concatenate(outs, axis=-1).astype(BF16)
    o_ref[...] = x + jnp.dot(o, wo_ref[...], preferred_element_type=F32)


def _cross_attention(x, g, wq, kv, wo, b, s):
    tm = _pick(s, 512)
    nt = s // tm
    m = kv.shape[1]
    return pl.pallas_call(
        _xattn_kernel,
        grid=(b, nt),
        in_specs=[
            pl.BlockSpec((tm, D_MODEL), lambda bi, i: (bi * nt + i, 0)),
            pl.BlockSpec((1, D_MODEL), lambda bi, i: (0, 0)),
            pl.BlockSpec((D_MODEL, D_MODEL), lambda bi, i: (0, 0)),
            pl.BlockSpec((1, m, 2 * D_MODEL), lambda bi, i: (bi, 0, 0)),
            pl.BlockSpec((D_MODEL, D_MODEL), lambda bi, i: (0, 0)),
        ],
        out_specs=pl.BlockSpec((tm, D_MODEL), lambda bi, i: (bi * nt + i, 0)),
        out_shape=jax.ShapeDtypeStruct((b * s, D_MODEL), F32),
        compiler_params=_cparams(("parallel", "parallel")),
    )(x, g.reshape(1, D_MODEL), wq.astype(BF16), kv, wo.astype(BF16))


FFN_HALO = 16
FFN_CHUNK = 256


def _ffn_kernel(x_ref, xp_ref, g_ref, wu_ref, wv_ref, cw_ref, cb_ref, wd_ref, *rest,
                tiles_per_seq, out_norm):
    if out_norm:
        og_ref, o_ref, h_s, act_s = rest
    else:
        o_ref, h_s, act_s = rest
    tm = x_ref.shape[0]
    first = (pl.program_id(0) % tiles_per_seq) == 0
    hp = _rms(xp_ref[...], g_ref[...])
    h_s[0:FFN_HALO, :] = jnp.where(first, 0.0, hp).astype(BF16)
    h_s[FFN_HALO:, :] = _rms(x_ref[...], g_ref[...]).astype(BF16)
    row = lax.broadcasted_iota(jnp.int32, (tm, FFN_CHUNK), 0)
    for c0 in range(0, D_FF, FFN_CHUNK):
        cols = slice(c0, c0 + FFN_CHUNK)
        ue = jnp.dot(h_s[...], wu_ref[:, cols], preferred_element_type=F32)
        vv = jnp.dot(h_s[FFN_HALO:, :], wv_ref[:, cols], preferred_element_type=F32)
        u = ue[FFN_HALO:, :]
        p1 = ue[FFN_HALO - 1:FFN_HALO, :]
        p2 = ue[FFN_HALO - 2:FFN_HALO - 1, :]
        u1 = jnp.where(row == 0, p1, pltpu.roll(u, 1, axis=0))
        u2 = jnp.where(row == 0, p2, jnp.where(row == 1, p1, pltpu.roll(u, 2, axis=0)))
        uc = (cw_ref[0:1, cols] * u2 + cw_ref[1:2, cols] * u1 + cw_ref[2:3, cols] * u
              + cb_ref[:, cols])
        act_s[:, cols] = (_silu(uc) * vv).astype(BF16)
    out = x_ref[...] + jnp.dot(act_s[...], wd_ref[...], preferred_element_type=F32)
    o_ref[...] = _rms(out, og_ref[...]) if out_norm else out


def _ffn(x, g, w_up, conv_w, conv_b, w_down, b, s, out_gain=None):
    t = x.shape[0]
    tm = _pick(s, 512)
    tiles_per_seq = s // tm
    nbh = tm // FFN_HALO
    out_norm = out_gain is not None
    wu = w_up.astype(BF16)
    resident = lambda shape, idx: pl.BlockSpec(shape, lambda i: idx, pipeline_mode=pl.Buffered(1))
    return pl.pallas_call(
        functools.partial(_ffn_kernel, tiles_per_seq=tiles_per_seq, out_norm=out_norm),
        grid=(t // tm,),
        in_specs=[
            pl.BlockSpec((tm, D_MODEL), lambda i: (i, 0)),
            pl.BlockSpec((FFN_HALO, D_MODEL), lambda i: (jnp.maximum(i * nbh - 1, 0), 0)),
            resident((1, D_MODEL), (0, 0)),
            resident((D_MODEL, D_FF), (0, 0)),
            resident((D_MODEL, D_FF), (0, 1)),
            resident((CONV_W, D_FF), (0, 0)),
            resident((1, D_FF), (0, 0)),
            resident((D_FF, D_MODEL), (0, 0)),
        ] + ([resident((1, D_MODEL), (0, 0))] if out_norm else []),
        out_specs=pl.BlockSpec((tm, D_MODEL), lambda i: (i, 0)),
        out_shape=jax.ShapeDtypeStruct((t, D_MODEL), F32),
        scratch_shapes=[pltpu.VMEM((tm + FFN_HALO, D_MODEL), BF16), pltpu.VMEM((tm, D_FF), BF16)],
        compiler_params=_cparams(("parallel",)),
    )(x, x, g.reshape(1, D_MODEL), wu, wu, conv_w, conv_b.reshape(1, D_FF),
      w_down.astype(BF16), *([out_gain.reshape(1, D_MODEL)] if out_norm else []))


def _permute_w_in(w):
    d = w.shape[0]
    return jnp.concatenate([
        w[:, R_OFF_RW:R_OFF_GATE], jnp.zeros((d, RW_PAD - RW_COLS), w.dtype),
        w[:, 0:R_OFF_DA], w[:, R_OFF_GATE:], w[:, R_OFF_DA:R_OFF_RW]], axis=1).astype(BF16)


def kernel(x, mem, norm_mix_g, w_in, b_gate, hgrn_lb_param, hgrn_norm_g, diff_lambda, diff_subln_g, rwkv_mu, rwkv_w0, rwkv_w_up, rwkv_a0, rwkv_a_up, rwkv_g_up, rwkv_k_k, rwkv_k_a, rwkv_r_k, rwkv_ln_g, rwkv_ln_b, w_branch, w_out, norm_xa_g, norm_mem_g, xa_wq, xa_wkv, xa_wo, norm_ffn_g, ffn_w_up, ffn_conv_w, ffn_conv_b, ffn_w_down, final_norm_g):
    b, s, d = x.shape
    depth = w_in.shape[0]
    m = mem.shape[1]
    t = b * s
    lb_p = jax.nn.softmax(hgrn_lb_param.astype(F32), axis=0)
    lower_bounds = jnp.cumsum(lb_p, axis=0) - lb_p[0]

    xf = x.reshape(t, d)
    memf = mem.reshape(b * m, d)
    for l in range(depth):
        p_rw, p_hg, p_gate, p_da = _in_proj(xf, norm_mix_g[l], _permute_w_in(w_in[l]))
        o_hg = _hgrn2(p_hg, lower_bounds[l], hgrn_norm_g[l], b, s)
        lambda_init = 0.8 - 0.6 * math.exp(-0.3 * l)
        o_da = _diff_attention(p_da.reshape(b, s, 3 * DA_WIDTH), diff_lambda[l], diff_subln_g[l],
                               lambda_init, b, s).reshape(t, DA_WIDTH)
        rw_params = dict(mu=rwkv_mu[l], w0=rwkv_w0[l], w_up=rwkv_w_up[l], a0=rwkv_a0[l],
                         a_up=rwkv_a_up[l], g_up=rwkv_g_up[l], k_k=rwkv_k_k[l], k_a=rwkv_k_a[l],
                         r_k=rwkv_r_k[l].reshape(-1))
        lw, r, k, v, kk, a, c3, g = _rwkv_prep(p_rw, rw_params, b, s)
        y = _rwkv_chunked(lw, r, k, v, kk, a, b, s)
        xf = _merge(xf, o_hg, o_da, y, v, c3, g, p_gate, b_gate[l], rwkv_ln_g[l], rwkv_ln_b[l],
                    w_branch[l], w_out[l])
        kv = _norm_matmul(memf, norm_mem_g[l], xa_wkv[l].astype(BF16), 256, 512)
        xf = _cross_attention(xf, norm_xa_g[l], xa_wq[l], kv.reshape(b, m, 2 * d), xa_wo[l], b, s)
        xf = _ffn(xf, norm_ffn_g[l], ffn_w_up[l], ffn_conv_w[l], ffn_conv_b[l], ffn_w_down[l], b, s,
                  out_gain=final_norm_g if l == depth - 1 else None)
    return xf.reshape(b, s, d)
```

```python
import functools
import math

import jax
import jax.numpy as jnp
from jax import lax
from jax.experimental import pallas as pl
from jax.experimental.pallas import tpu as pltpu

F32 = jnp.float32
BF16 = jnp.bfloat16

D_MODEL = 1024
NORM_EPS = 1e-6
MASK_VALUE = -1e30
TINY = 1e-30

HG_HEADS = 4
HG_D = 128
HG_WIDTH = HG_HEADS * HG_D
HG_SUB = 16

DA_HEADS = 4
DA_DK = 64
DA_DV = 128
DA_WIDTH = DA_HEADS * DA_DV
ALIBI_MAX_BIAS = 8.0
LOG2E = 1.4426950408889634
DA_ROW_BLOCK = 128
DA_UNROLL = 4

RW_HEADS = 8
RW_HEAD = 64
RW_WIDTH = RW_HEADS * RW_HEAD
RW_DECAY_RANK = 64
RW_A_RANK = 64
RW_GATE_RANK = 128
RW_COLS = 3 * RW_WIDTH + RW_DECAY_RANK + RW_A_RANK + RW_GATE_RANK
RW_GN_EPS = 64e-5
RW_CHUNKS_PER_STEP = 2

N_BRANCH = 3
BRANCH_WIDTH = 512

XA_HEADS = 4
XA_HEAD = D_MODEL // XA_HEADS

D_FF = 2816
CONV_W = 3

RW_PAD = 2048
P_OFF_RW = 0
P_OFF_HG = RW_PAD
P_OFF_GATE = P_OFF_HG + 4 * HG_WIDTH
P_OFF_DA = P_OFF_GATE + N_BRANCH * D_MODEL
N_PROJ = P_OFF_DA + 3 * DA_WIDTH

R_OFF_DA = 4 * HG_WIDTH
R_OFF_RW = R_OFF_DA + 3 * DA_WIDTH
R_OFF_GATE = R_OFF_RW + RW_COLS

V7X_VMEM_LIMIT = 56 * 1024 * 1024
SUBLANES = 8


def _cparams(sem):
    return pltpu.CompilerParams(dimension_semantics=sem, vmem_limit_bytes=V7X_VMEM_LIMIT)


def _pick(n, pref):
    t = min(n, pref)
    while n % t:
        t //= 2
    return t


def _rms(x, g):
    ms = jnp.mean(x * x, axis=-1, keepdims=True)
    return x * lax.rsqrt(ms + NORM_EPS) * g


def _split3(x):
    hi = x.astype(BF16)
    r1 = x - hi.astype(F32)
    mid = r1.astype(BF16)
    lo = (r1 - mid.astype(F32)).astype(BF16)
    return hi, mid, lo


def _gdot(x, g):
    hi, lo, _ = _split3(x)
    return (jnp.dot(hi, g, preferred_element_type=F32)
            + jnp.dot(lo, g, preferred_element_type=F32))


def _silu(x):
    return x * jax.nn.sigmoid(x)


def _norm_mm_kernel(x_ref, g_ref, w_ref, o_ref, h_ref):
    @pl.when(pl.program_id(1) == 0)
    def _():
        h_ref[...] = _rms(x_ref[...], g_ref[...]).astype(BF16)

    o_ref[...] = jnp.dot(h_ref[...], w_ref[...], preferred_element_type=F32)


def _norm_matmul(x, g, w, tm, tn):
    t, d = x.shape
    n = w.shape[1]
    tm = _pick(t, tm)
    tn = _pick(n, tn)
    return pl.pallas_call(
        _norm_mm_kernel,
        grid=(t // tm, n // tn),
        in_specs=[
            pl.BlockSpec((tm, d), lambda i, j: (i, 0)),
            pl.BlockSpec((1, d), lambda i, j: (0, 0)),
            pl.BlockSpec((d, tn), lambda i, j: (0, j)),
        ],
        out_specs=pl.BlockSpec((tm, tn), lambda i, j: (i, j)),
        out_shape=jax.ShapeDtypeStruct((t, n), F32),
        scratch_shapes=[pltpu.VMEM((tm, d), BF16)],
        compiler_params=_cparams(("parallel", "arbitrary")),
    )(x, g.reshape(1, d), w)


PROJ_SEGMENTS = ((P_OFF_RW, RW_PAD, F32), (P_OFF_HG, 4 * HG_WIDTH, F32),
                 (P_OFF_GATE, N_BRANCH * D_MODEL, F32), (P_OFF_DA, 3 * DA_WIDTH, BF16))
MXU_N = 256


def _in_proj_kernel(x_ref, g_ref, w_ref, *outs):
    h = _rms(x_ref[...], g_ref[...]).astype(BF16)
    step = 2 * MXU_N
    for o_ref, (off, width, dt) in zip(outs, PROJ_SEGMENTS):
        for c0 in range(0, width, step):
            o_ref[:, c0:c0 + step] = jnp.dot(h, w_ref[:, off + c0:off + c0 + step],
                                             preferred_element_type=F32).astype(dt)


def _in_proj(x, g, w):
    t, d = x.shape
    tm = _pick(t, 256)
    return pl.pallas_call(
        _in_proj_kernel,
        grid=(t // tm,),
        in_specs=[
            pl.BlockSpec((tm, d), lambda i: (i, 0)),
            pl.BlockSpec((1, d), lambda i: (0, 0)),
            pl.BlockSpec((d, N_PROJ), lambda i: (0, 0), pipeline_mode=pl.Buffered(1)),
        ],
        out_specs=[pl.BlockSpec((tm, width), lambda i: (i, 0)) for _, width, _ in PROJ_SEGMENTS],
        out_shape=[jax.ShapeDtypeStruct((t, width), dt) for _, width, dt in PROJ_SEGMENTS],
        compiler_params=_cparams(("parallel",)),
    )(x, g.reshape(1, d), w)


HG_GROUP = 2


def _hgrn_kernel(q_ref, f_ref, i_ref, g_ref, lb_ref, ng_ref, o_ref,
                 st_ref, qs_s, b_s, k_s, v_s, o_s, *, ts):
    @pl.when(pl.program_id(2) == 0)
    def _():
        st_ref[...] = jnp.zeros_like(st_ref)

    hs = range(HG_GROUP)
    r16 = lax.broadcasted_iota(jnp.int32, (ts, HG_D), 0) % HG_SUB
    for hh in hs:
        cols = slice(hh * HG_D, (hh + 1) * HG_D)
        lb = lb_ref[:, cols]
        fp = f_ref[:, cols]
        f = lb + (1.0 - lb) * jax.nn.sigmoid(fp)
        k_s[hh] = (1.0 - lb) * jax.nn.sigmoid(-fp)
        qs_s[hh] = _silu(q_ref[:, cols])
        v_s[hh] = i_ref[:, cols]
        b = jnp.log(jnp.maximum(f, TINY))
        sh = 1
        while sh < HG_SUB:
            b = b + jnp.where(r16 >= sh, pltpu.roll(b, sh, axis=0), 0.0)
            sh *= 2
        b_s[hh] = b * LOG2E

    rowi = lax.broadcasted_iota(jnp.int32, (HG_SUB, HG_D), 0)

    def body(ci, carry):
        base = pl.multiple_of(ci * HG_SUB, HG_SUB)
        rows = pl.ds(base, HG_SUB)
        qc = [qs_s[hh, rows, :] for hh in hs]
        bc = [b_s[hh, rows, :] for hh in hs]
        st = [st_ref[hh] for hh in hs]
        zero = jnp.zeros((HG_D, HG_D), BF16)
        w_bd = jnp.concatenate(
            [jnp.concatenate([st[hh].astype(BF16) if c == hh else zero for c in hs], axis=1)
             for hh in hs], axis=0)
        qe = jnp.concatenate([(qc[hh] * jnp.exp2(bc[hh])).astype(BF16) for hh in hs], axis=1)
        o_cat = lax.dot_general(qe, w_bd, (((1,), (1,)), ((), ())), preferred_element_type=F32)
        o = [o_cat[:, hh * HG_D:(hh + 1) * HG_D] for hh in hs]
        for j in range(HG_SUB):
            for hh in hs:
                bj = b_s[hh, pl.ds(base + j, 1), :]
                kj = k_s[hh, pl.ds(base + j, 1), :]
                vj = v_s[hh, pl.ds(base + j, 1), :]
                dec = jnp.exp2(jnp.minimum(bc[hh] - bj, 0.0))
                sc = jnp.where(rowi >= j,
                               jnp.sum(qc[hh] * dec * kj, axis=-1, keepdims=True), 0.0)
                o[hh] = o[hh] + sc * vj
        bl = [b_s[hh, pl.ds(base + HG_SUB - 1, 1), :] for hh in hs]
        kd = jnp.concatenate([(k_s[hh, rows, :] * jnp.exp2(bl[hh] - bc[hh])).astype(BF16)
                              for hh in hs], axis=1)
        vc = jnp.concatenate([v_s[hh, rows, :].astype(BF16) for hh in hs], axis=1)
        upd = lax.dot_general(vc, kd, (((0,), (0,)), ((), ())), preferred_element_type=F32)
        for hh in hs:
            o_s[hh, rows, :] = o[hh]
            blk = slice(hh * HG_D, (hh + 1) * HG_D)
            st_ref[hh] = st[hh] * jnp.exp2(bl[hh]) + upd[blk, blk]
        return carry

    lax.fori_loop(0, ts // HG_SUB, body, 0, unroll=16)

    for hh in hs:
        cols = slice(hh * HG_D, (hh + 1) * HG_D)
        o = _rms(o_s[hh], ng_ref[:, cols])
        o_ref[:, cols] = (o * _silu(g_ref[:, cols])).astype(o_ref.dtype)


def _hgrn2(proj, lb, norm_g, b, s):
    ts = _pick(s, 256)
    nt = s // ts
    width = HG_GROUP * HG_D
    groups = HG_HEADS // HG_GROUP

    def col(section):
        return pl.BlockSpec((ts, width),
                            lambda bi, h, c, section=section: (bi * nt + c, section * groups + h))

    par = pl.BlockSpec((1, width), lambda bi, h, c: (0, h))
    return pl.pallas_call(
        functools.partial(_hgrn_kernel, ts=ts),
        grid=(b, groups, nt),
        in_specs=[col(0), col(1), col(2), col(3), par, par],
        out_specs=pl.BlockSpec((ts, width), lambda bi, h, c: (bi * nt + c, h)),
        out_shape=jax.ShapeDtypeStruct((b * s, HG_WIDTH), BF16),
        scratch_shapes=[pltpu.VMEM((HG_GROUP, HG_D, HG_D), F32)]
        + [pltpu.VMEM((HG_GROUP, ts, HG_D), F32)] * 5,
        compiler_params=_cparams(("parallel", "parallel", "arbitrary")),
    )(proj, proj, proj, proj, lb.reshape(1, HG_WIDTH), norm_g.reshape(1, HG_WIDTH))


def _da_kernel(q_ref, k_ref, v_ref, sl_ref, lam_ref, sg_ref, o_ref,
               vb_s, q2_s, m_s, acc_s, *, tq, lambda_init):
    i = pl.program_id(2)
    nq = pl.num_programs(2)
    lanes = DA_DV

    @pl.when(i == 0)
    def _():
        def fill(c, carry):
            rows = pl.ds(pl.multiple_of(c * tq, tq), tq)
            vb_s[rows, 0:lanes] = v_ref[rows, :]
            vb_s[rows, lanes:2 * lanes] = jnp.ones((tq, lanes), BF16)
            return carry
        lax.fori_loop(0, nq, fill, 0)

    q = q_ref[...].astype(F32) * (DA_DK ** -0.5 * LOG2E)
    lane = lax.broadcasted_iota(jnp.int32, (tq, DA_DV), 1)
    q2_s[pl.ds(0, tq), :] = jnp.where(lane < DA_DK, q, 0.0).astype(BF16)
    q2_s[pl.ds(tq, tq), :] = jnp.where(lane >= DA_DK, q, 0.0).astype(BF16)
    m_s[...] = jnp.full_like(m_s, MASK_VALUE)
    acc_s[...] = jnp.zeros_like(acc_s)

    slope = sl_ref[0] * LOG2E
    colf = lax.broadcasted_iota(jnp.int32, (1, tq), 1).astype(F32)
    rb = min(DA_ROW_BLOCK, tq)

    def step(j, masked):
        start = pl.multiple_of(j * tq, tq)
        kb = k_ref[pl.ds(start, tq), :]
        vb = vb_s[pl.ds(start, tq), :]
        bias = slope * (colf + ((j - i) * tq).astype(F32))
        for r0 in range(0, 2 * tq, rb):
            rows = pl.ds(r0, rb)
            s = lax.dot_general(q2_s[rows, :], kb, (((1,), (1,)), ((), ())),
                                preferred_element_type=F32) + bias
            if masked:
                r = lax.broadcasted_iota(jnp.int32, (rb, tq), 0) + (r0 % tq)
                c = lax.broadcasted_iota(jnp.int32, (rb, tq), 1)
                s = jnp.where(c <= r, s, MASK_VALUE)
            m_old = m_s[rows, :]
            m_new = jnp.maximum(m_old, jnp.max(s, axis=-1, keepdims=True))
            alpha = jnp.exp2(m_old - m_new)
            p = jnp.exp2(s - pltpu.repeat(m_new, tq // lanes, axis=1))
            acc_s[rows, :] = (pltpu.repeat(alpha, 2, axis=1) * acc_s[rows, :]
                              + jnp.dot(p.astype(BF16), vb, preferred_element_type=F32))
            m_s[rows, :] = m_new

    def body_many(jj, carry):
        for u in range(DA_UNROLL):
            step(DA_UNROLL * jj + u, False)
        return carry

    full = i // DA_UNROLL
    lax.fori_loop(0, full, body_many, 0)
    for rem in range(DA_UNROLL):
        @pl.when(i % DA_UNROLL == rem)
        def _(rem=rem):
            for u in range(rem):
                step(full * DA_UNROLL + u, False)
            step(i, True)

    lam = lam_ref[0]
    e1 = jnp.exp(jnp.sum(lam[0:1] * lam[1:2], axis=-1, keepdims=True))
    e2 = jnp.exp(jnp.sum(lam[2:3] * lam[3:4], axis=-1, keepdims=True))
    lam_full = e1 - e2 + lambda_init
    o1 = acc_s[pl.ds(0, tq), 0:lanes] / acc_s[pl.ds(0, tq), lanes:2 * lanes]
    o2 = acc_s[pl.ds(tq, tq), 0:lanes] / acc_s[pl.ds(tq, tq), lanes:2 * lanes]
    o = o1 - lam_full * o2
    o_ref[...] = (_rms(o, sg_ref[...]) * (1.0 - lambda_init)).astype(o_ref.dtype)


def _diff_attention(qkv, lam, subln_g, lambda_init, b, s):
    tq = _pick(s, 512)
    nq = s // tq
    slopes = 2.0 ** (-ALIBI_MAX_BIAS * jnp.arange(1, DA_HEADS + 1, dtype=F32) / DA_HEADS)
    slopes = jnp.broadcast_to(slopes[:, None, None], (DA_HEADS, 1, tq))
    return pl.pallas_call(
        functools.partial(_da_kernel, tq=tq, lambda_init=lambda_init),
        grid=(b, DA_HEADS, nq),
        in_specs=[
            pl.BlockSpec((None, tq, DA_DV), lambda bi, h, i: (bi, i, h)),
            pl.BlockSpec((None, s, DA_DV), lambda bi, h, i: (bi, 0, DA_HEADS + h)),
            pl.BlockSpec((None, s, DA_DV), lambda bi, h, i: (bi, 0, 2 * DA_HEADS + h)),
            pl.BlockSpec((1, 1, tq), lambda bi, h, i: (h, 0, 0)),
            pl.BlockSpec((1, 4, DA_DK), lambda bi, h, i: (0, 0, 0)),
            pl.BlockSpec((1, DA_DV), lambda bi, h, i: (0, 0)),
        ],
        out_specs=pl.BlockSpec((None, tq, DA_DV), lambda bi, h, i: (bi, i, h)),
        out_shape=jax.ShapeDtypeStruct((b, s, DA_WIDTH), BF16),
        scratch_shapes=[
            pltpu.VMEM((s, 2 * DA_DV), BF16),
            pltpu.VMEM((2 * tq, DA_DV), BF16),
            pltpu.VMEM((2 * tq, DA_DV), F32),
            pltpu.VMEM((2 * tq, 2 * DA_DV), F32),
        ],
        compiler_params=_cparams(("parallel", "parallel", "arbitrary")),
    )(qkv, qkv, qkv, slopes, lam.reshape(1, 4, DA_DK), subln_g.reshape(1, DA_DV))


def _rw_prep_kernel(z_ref, zp_ref, mu_ref, wa_ref, w0_ref, a0_ref, gup_ref, kk_ref, ka_ref,
                    rk_ref, hs_ref,
                    lw_o, r_o, k_o, v_o, kk_o, a_o, c3_o, g_o, *, tiles_per_seq):
    tm = z_ref.shape[0]
    z = z_ref[...]
    first = (pl.program_id(0) % tiles_per_seq) == 0
    prev = jnp.where(first, 0.0, zp_ref[SUBLANES - 1:SUBLANES, :])
    row = lax.broadcasted_iota(jnp.int32, z.shape, 0)
    zs = jnp.where(row == 0, prev, pltpu.roll(z, 1, axis=0))
    z = z + mu_ref[...] * (zs - z)

    w3 = RW_WIDTH
    r = z[:, 0:w3]
    k = z[:, w3:2 * w3]
    v = z[:, 2 * w3:3 * w3]
    wd_ad = z[:, 3 * w3:3 * w3 + 128]
    gd = z[:, 3 * w3 + 128:3 * w3 + 256]
    lane = lax.broadcasted_iota(jnp.int32, wd_ad.shape, 1)
    lowrank_in = jnp.where(lane < RW_DECAY_RANK, jnp.tanh(wd_ad), wd_ad)
    wa = jnp.dot(lowrank_in.astype(BF16), wa_ref[...], preferred_element_type=F32)
    w = w0_ref[...] + wa[:, 0:w3]
    nw = -w
    softplus = jnp.maximum(nw, 0.0) + jnp.log(1.0 + jnp.exp(-jnp.abs(nw)))
    log_decay = -jnp.exp(-softplus - 0.5)
    a = jax.nn.sigmoid(a0_ref[...] + wa[:, w3:2 * w3])
    g = jnp.dot(jax.nn.sigmoid(gd).astype(BF16), gup_ref[...], preferred_element_type=F32)

    hs = hs_ref[...]
    kk = k * kk_ref[...]
    kk = kk / jnp.maximum(jnp.sqrt(_gdot(kk * kk, hs)), 1e-12)
    kmod = k * (1.0 + (a - 1.0) * ka_ref[...])

    lw_o[...] = log_decay
    r_o[...] = r
    k_o[...] = kmod
    v_o[...] = v
    kk_o[...] = kk
    a_o[...] = a
    c3_o[...] = _gdot(r * kmod * rk_ref[...], hs)
    g_o[...] = g


def _head_sum_matrix(n, head):
    idx = jnp.arange(n) // head
    return (idx[:, None] == idx[None, :]).astype(BF16)


def _rwkv_prep(proj, p, b, s):
    t = b * s
    tm = _pick(s, 256)
    tiles_per_seq = s // tm
    w3 = RW_WIDTH
    wa = jnp.zeros((128, 2 * w3), F32)
    wa = wa.at[0:RW_DECAY_RANK, 0:w3].set(p["w_up"]).at[RW_DECAY_RANK:128, w3:].set(p["a_up"])
    mu = jnp.pad(p["mu"], (0, RW_PAD - RW_COLS)).reshape(1, RW_PAD)
    row = lambda a: a.reshape(1, w3)
    const = lambda shape: pl.BlockSpec(shape, lambda i: (0, 0))
    out_spec = pl.BlockSpec((tm, w3), lambda i: (i, 0))
    nb8 = tm // SUBLANES
    return pl.pallas_call(
        functools.partial(_rw_prep_kernel, tiles_per_seq=tiles_per_seq),
        grid=(t // tm,),
        in_specs=[
            pl.BlockSpec((tm, RW_PAD), lambda i: (i, 0)),
            pl.BlockSpec((SUBLANES, RW_PAD), lambda i: (jnp.maximum(i * nb8 - 1, 0), 0)),
            const((1, RW_PAD)), const((128, 2 * w3)), const((1, w3)), const((1, w3)),
            const((RW_GATE_RANK, w3)), const((1, w3)), const((1, w3)), const((1, w3)),
            const((w3, w3)),
        ],
        out_specs=[out_spec] * 8,
        out_shape=[jax.ShapeDtypeStruct((t, w3), F32)] * 8,
        compiler_params=_cparams(("parallel",)),
    )(proj, proj, mu, wa.astype(BF16), row(p["w0"]), row(p["a0"]), p["g_up"].astype(BF16),
      row(p["k_k"]), row(p["k_a"]), row(p["r_k"]), _head_sum_matrix(w3, RW_HEAD))


_NT = (((1,), (1,)), ((), ()))
_TN = (((0,), (0,)), ((), ()))
_NN = (((1,), (0,)), ((), ()))


def _mm(a, b, dims=_NN):
    return lax.dot_general(a.astype(BF16), b.astype(BF16), dims, preferred_element_type=F32)


def _rw_chunk_kernel(lw_ref, r_ref, k_ref, v_ref, kk_ref, a_ref, tri_ref, y_ref, st_ref, *,
                     c, nc):
    @pl.when(pl.program_id(1) == 0)
    def _():
        st_ref[...] = jnp.zeros_like(st_ref)

    lanes = 2 * RW_HEAD
    tri = tri_ref[...]
    row = lax.broadcasted_iota(jnp.int32, (c, c), 0)
    col = lax.broadcasted_iota(jnp.int32, (c, c), 1)
    strict = row > col
    incl = row >= col
    eye = (row == col).astype(F32)
    lane = lax.broadcasted_iota(jnp.int32, (c, lanes), 1)
    head_mask = (lane < RW_HEAD, lane >= RW_HEAD)
    srow = lax.broadcasted_iota(jnp.int32, (lanes, lanes), 0)
    scol = lax.broadcasted_iota(jnp.int32, (lanes, lanes), 1)
    same_head = (srow // RW_HEAD) == (scol // RW_HEAD)

    def by_head(x):
        return jnp.concatenate([jnp.where(hm, x, 0.0) for hm in head_mask], axis=0)

    npair = RW_HEADS // 2
    units = range(nc * npair)
    heads = range(nc * RW_HEADS)
    rows = [pl.ds((q // npair) * c, c) for q in units]
    cols = [pl.ds((q % npair) * lanes, lanes) for q in units]
    lw = [lw_ref[0, rows[q], cols[q]] for q in units]
    parts = [_split3(lw[q]) for q in units]
    lc = [sum(jnp.dot(tri, part, preferred_element_type=F32) for part in parts[q])
          for q in units]
    kk = [kk_ref[0, rows[q], cols[q]] for q in units]
    k = [k_ref[0, rows[q], cols[q]] for q in units]
    v = [v_ref[0, rows[q], cols[q]] for q in units]
    beta = [kk[q] * a_ref[0, rows[q], cols[q]] for q in units]
    a_t = [-kk[q] * jnp.exp(lc[q] - lw[q]) for q in units]
    r_t = [r_ref[0, rows[q], cols[q]] * jnp.exp(lc[q]) for q in units]
    e_neg = [jnp.exp(-lc[q]) for q in units]
    rhs = [jnp.concatenate([beta[q] * e_neg[q], k[q] * e_neg[q]], axis=0) for q in units]

    m = [_mm(jnp.concatenate([jnp.where(head_mask[g % 2], a_t[g // 2], 0.0),
                              jnp.where(head_mask[g % 2], r_t[g // 2], 0.0)], axis=0),
             rhs[g // 2], _NT) for g in heads]
    l_mat = [jnp.where(strict, m[g][0:c, 0:c], 0.0) for g in heads]
    a_ak = [jnp.where(strict, m[g][0:c, c:2 * c], 0.0) for g in heads]
    a_rb = [jnp.where(incl, m[g][c:2 * c, 0:c], 0.0) for g in heads]
    a_rk = [jnp.where(incl, m[g][c:2 * c, c:2 * c], 0.0) for g in heads]

    t_inv = [eye + jnp.where(((row // 2) == (col // 2)) & (row % 2 == 1) & (col % 2 == 0),
                             l_mat[g], 0.0) for g in heads]
    size = 2
    while size < c:
        sel = (((row // (2 * size)) == (col // (2 * size)))
               & ((row // size) % 2 == 1) & ((col // size) % 2 == 0))
        po = [_mm(t_inv[g], jnp.where(sel, l_mat[g], 0.0)) for g in heads]
        t_inv = [t_inv[g] + _mm(po[g], t_inv[g]) for g in heads]
        size *= 2

    state = [st_ref[p] for p in range(npair)]
    for ci in range(nc):
        qs = range(ci * npair, (ci + 1) * npair)
        s0 = {q: state[q % npair] for q in qs}
        xr = {q: _mm(jnp.concatenate([a_t[q], r_t[q]], axis=0), s0[q], _NT) for q in qs}
        v2 = {q: by_head(v[q]) for q in qs}
        x = {q: xr[q][0:c] + _mm(jnp.concatenate(a_ak[2 * q:2 * q + 2], axis=1), v2[q])
             for q in qs}
        u = {q: _mm(jnp.concatenate(t_inv[2 * q:2 * q + 2], axis=1), by_head(x[q])) for q in qs}
        for q in qs:
            y_ref[0, rows[q], cols[q]] = xr[q][c:2 * c] + _mm(
                jnp.concatenate(a_rb[2 * q:2 * q + 2] + a_rk[2 * q:2 * q + 2], axis=1),
                jnp.concatenate([by_head(u[q]), v2[q]], axis=0))
        for q in qs:
            lc_end = lc[q][c - 1:c, :]
            e_end = jnp.exp(lc_end - lc[q])
            upd = _mm(jnp.concatenate([u[q], v[q]], axis=0),
                      jnp.concatenate([beta[q] * e_end, k[q] * e_end], axis=0), _TN)
            state[q % npair] = jnp.where(same_head, s0[q] * jnp.exp(lc_end) + upd, 0.0)
    for p in range(npair):
        st_ref[p] = state[p]


def _rwkv_chunked(lw, r, k, v, kk, a, b, s):
    c = _pick(s, 128)
    nc = RW_CHUNKS_PER_STEP if s % (RW_CHUNKS_PER_STEP * c) == 0 else 1
    w3 = RW_WIDTH
    r3 = lambda x: x.reshape(b, s, w3)
    spec = pl.BlockSpec((1, nc * c, w3), lambda bi, i: (bi, i, 0))
    tri = (jnp.arange(c)[:, None] >= jnp.arange(c)[None, :]).astype(BF16)
    return pl.pallas_call(
        functools.partial(_rw_chunk_kernel, c=c, nc=nc),
        grid=(b, s // (nc * c)),
        in_specs=[spec] * 6 + [pl.BlockSpec((c, c), lambda bi, i: (0, 0))],
        out_specs=spec,
        out_shape=jax.ShapeDtypeStruct((b, s, w3), F32),
        scratch_shapes=[pltpu.VMEM((RW_HEADS // 2, 2 * RW_HEAD, 2 * RW_HEAD), F32)],
        compiler_params=_cparams(("parallel", "arbitrary")),
    )(r3(lw), r3(r), r3(k), r3(v), r3(kk), r3(a), tri).reshape(b * s, w3)


def _merge_kernel(x_ref, hg_ref, da_ref, y_ref, v_ref, c3_ref, g_ref, g0_ref, g1_ref, g2_ref,
                  bg_ref, lng_ref, lnb_ref, hs_ref, wb_ref, wo_ref, o_ref):
    hs = hs_ref[...]
    y = y_ref[...]
    mu = _gdot(y, hs) * (1.0 / RW_HEAD)
    yc = y - mu
    var = _gdot(yc * yc, hs) * (1.0 / RW_HEAD)
    yn = yc * lax.rsqrt(var + RW_GN_EPS) * lng_ref[...] + lnb_ref[...]
    o_rw = (yn + c3_ref[...] * v_ref[...]) * g_ref[...]

    merged = None
    for n, (br, gp) in enumerate(((hg_ref[...], g0_ref), (da_ref[...], g1_ref), (o_rw, g2_ref))):
        gate = jax.nn.sigmoid(gp[...] + bg_ref[:, n * D_MODEL:(n + 1) * D_MODEL])
        pb = jnp.dot(br.astype(BF16), wb_ref[n], preferred_element_type=F32)
        merged = gate * pb if merged is None else merged + gate * pb
    o_ref[...] = x_ref[...] + jnp.dot(merged.astype(BF16), wo_ref[...],
                                      preferred_element_type=F32)


def _merge(x, o_hg, o_da, y, v, c3, g, proj, b_gate, ln_g, ln_b, w_branch, w_out):
    t = x.shape[0]
    tm = _pick(t, 512)
    bw = BRANCH_WIDTH
    row = pl.BlockSpec((tm, bw), lambda i: (i, 0))
    gate_spec = lambda n: pl.BlockSpec((tm, D_MODEL), lambda i, n=n: (i, n))
    const = lambda shape: pl.BlockSpec(shape, lambda i: (0,) * len(shape),
                                       pipeline_mode=pl.Buffered(1))
    return pl.pallas_call(
        _merge_kernel,
        grid=(t // tm,),
        in_specs=[pl.BlockSpec((tm, D_MODEL), lambda i: (i, 0))] + [row] * 6
        + [gate_spec(0), gate_spec(1), gate_spec(2),
           const((1, N_BRANCH * D_MODEL)), const((1, bw)), const((1, bw)), const((bw, bw)),
           const((N_BRANCH, bw, D_MODEL)), const((D_MODEL, D_MODEL))],
        out_specs=pl.BlockSpec((tm, D_MODEL), lambda i: (i, 0)),
        out_shape=jax.ShapeDtypeStruct((t, D_MODEL), F32),
        compiler_params=_cparams(("parallel",)),
    )(x, o_hg, o_da, y, v, c3, g, proj, proj, proj, b_gate.reshape(1, -1),
      ln_g.reshape(1, bw), ln_b.reshape(1, bw), _head_sum_matrix(bw, RW_HEAD),
      w_branch.astype(BF16), w_out.astype(BF16))


def _xattn_kernel(x_ref, g_ref, wq_ref, kv_ref, wo_ref, o_ref):
    x = x_ref[...]
    h = _rms(x, g_ref[...]).astype(BF16)
    q = jnp.dot(h, wq_ref[...], preferred_element_type=F32) * (XA_HEAD ** -0.5)
    outs = []
    for hd in range(XA_HEADS):
        qh = q[:, hd * XA_HEAD:(hd + 1) * XA_HEAD].astype(BF16)
        kh = kv_ref[0, :, hd * XA_HEAD:(hd + 1) * XA_HEAD].astype(BF16)
        vh = kv_ref[0, :, D_MODEL + hd * XA_HEAD:D_MODEL + (hd + 1) * XA_HEAD].astype(BF16)
        s = lax.dot_general(qh, kh, (((1,), (1,)), ((), ())), preferred_element_type=F32)
        m = jnp.max(s, axis=-1, keepdims=True)
        p = jnp.exp(s - m)
        p = p / jnp.sum(p, axis=-1, keepdims=True)
        outs.append(jnp.dot(p.astype(BF16), vh, preferred_element_type=F32))
    o = jnp.concatenate(outs, axis=-1).astype(BF16)
    o_ref[...] = x + jnp.dot(o, wo_ref[...], preferred_element_type=F32)


def _cross_attention(x, g, wq, kv, wo, b, s):
    tm = _pick(s, 512)
    nt = s // tm
    m = kv.shape[1]
    return pl.pallas_call(
        _xattn_kernel,
        grid=(b, nt),
        in_specs=[
            pl.BlockSpec((tm, D_MODEL), lambda bi, i: (bi * nt + i, 0)),
            pl.BlockSpec((1, D_MODEL), lambda bi, i: (0, 0)),
            pl.BlockSpec((D_MODEL, D_MODEL), lambda bi, i: (0, 0)),
            pl.BlockSpec((1, m, 2 * D_MODEL), lambda bi, i: (bi, 0, 0)),
            pl.BlockSpec((D_MODEL, D_MODEL), lambda bi, i: (0, 0)),
        ],
        out_specs=pl.BlockSpec((tm, D_MODEL), lambda bi, i: (bi * nt + i, 0)),
        out_shape=jax.ShapeDtypeStruct((b * s, D_MODEL), F32),
        compiler_params=_cparams(("parallel", "parallel")),
    )(x, g.reshape(1, D_MODEL), wq.astype(BF16), kv, wo.astype(BF16))


FFN_HALO = 16
FFN_CHUNK = 256


def _ffn_kernel(x_ref, xp_ref, g_ref, wu_ref, wv_ref, cw_ref, cb_ref, wd_ref, *rest,
                tiles_per_seq, out_norm):
    if out_norm:
        og_ref, o_ref, h_s, act_s = rest
    else:
        o_ref, h_s, act_s = rest
    tm = x_ref.shape[0]
    first = (pl.program_id(0) % tiles_per_seq) == 0
    hp = _rms(xp_ref[...], g_ref[...])
    h_s[0:FFN_HALO, :] = jnp.where(first, 0.0, hp).astype(BF16)
    h_s[FFN_HALO:, :] = _rms(x_ref[...], g_ref[...]).astype(BF16)
    row = lax.broadcasted_iota(jnp.int32, (tm, FFN_CHUNK), 0)
    for c0 in range(0, D_FF, FFN_CHUNK):
        cols = slice(c0, c0 + FFN_CHUNK)
        ue = jnp.dot(h_s[...], wu_ref[:, cols], preferred_element_type=F32)
        vv = jnp.dot(h_s[FFN_HALO:, :], wv_ref[:, cols], preferred_element_type=F32)
        u = ue[FFN_HALO:, :]
        p1 = ue[FFN_HALO - 1:FFN_HALO, :]
        p2 = ue[FFN_HALO - 2:FFN_HALO - 1, :]
        u1 = jnp.where(row == 0, p1, pltpu.roll(u, 1, axis=0))
        u2 = jnp.where(row == 0, p2, jnp.where(row == 1, p1, pltpu.roll(u, 2, axis=0)))
        uc = (cw_ref[0:1, cols] * u2 + cw_ref[1:2, cols] * u1 + cw_ref[2:3, cols] * u
              + cb_ref[:, cols])
        act_s[:, cols] = (_silu(uc) * vv).astype(BF16)
    out = x_ref[...] + jnp.dot(act_s[...], wd_ref[...], preferred_element_type=F32)
    o_ref[...] = _rms(out, og_ref[...]) if out_norm else out


def _ffn(x, g, w_up, conv_w, conv_b, w_down, b, s, out_gain=None):
    t = x.shape[0]
    tm = _pick(s, 512)
    tiles_per_seq = s // tm
    nbh = tm // FFN_HALO
    out_norm = out_gain is not None
    wu = w_up.astype(BF16)
    resident = lambda shape, idx: pl.BlockSpec(shape, lambda i: idx, pipeline_mode=pl.Buffered(1))
    return pl.pallas_call(
        functools.partial(_ffn_kernel, tiles_per_seq=tiles_per_seq, out_norm=out_norm),
        grid=(t // tm,),
        in_specs=[
            pl.BlockSpec((tm, D_MODEL), lambda i: (i, 0)),
            pl.BlockSpec((FFN_HALO, D_MODEL), lambda i: (jnp.maximum(i * nbh - 1, 0), 0)),
            resident((1, D_MODEL), (0, 0)),
            resident((D_MODEL, D_FF), (0, 0)),
            resident((D_MODEL, D_FF), (0, 1)),
            resident((CONV_W, D_FF), (0, 0)),
            resident((1, D_FF), (0, 0)),
            resident((D_FF, D_MODEL), (0, 0)),
        ] + ([resident((1, D_MODEL), (0, 0))] if out_norm else []),
        out_specs=pl.BlockSpec((tm, D_MODEL), lambda i: (i, 0)),
        out_shape=jax.ShapeDtypeStruct((t, D_MODEL), F32),
        scratch_shapes=[pltpu.VMEM((tm + FFN_HALO, D_MODEL), BF16), pltpu.VMEM((tm, D_FF), BF16)],
        compiler_params=_cparams(("parallel",)),
    )(x, x, g.reshape(1, D_MODEL), wu, wu, conv_w, conv_b.reshape(1, D_FF),
      w_down.astype(BF16), *([out_gain.reshape(1, D_MODEL)] if out_norm else []))


def _permute_w_in(w):
    d = w.shape[0]
    return jnp.concatenate([
        w[:, R_OFF_RW:R_OFF_GATE], jnp.zeros((d, RW_PAD - RW_COLS), w.dtype),
        w[:, 0:R_OFF_DA], w[:, R_OFF_GATE:], w[:, R_OFF_DA:R_OFF_RW]], axis=1).astype(BF16)


def kernel(x, mem, norm_mix_g, w_in, b_gate, hgrn_lb_param, hgrn_norm_g, diff_lambda, diff_subln_g, rwkv_mu, rwkv_w0, rwkv_w_up, rwkv_a0, rwkv_a_up, rwkv_g_up, rwkv_k_k, rwkv_k_a, rwkv_r_k, rwkv_ln_g, rwkv_ln_b, w_branch, w_out, norm_xa_g, norm_mem_g, xa_wq, xa_wkv, xa_wo, norm_ffn_g, ffn_w_up, ffn_conv_w, ffn_conv_b, ffn_w_down, final_norm_g):
    b, s, d = x.shape
    depth = w_in.shape[0]
    m = mem.shape[1]
    t = b * s
    lb_p = jax.nn.softmax(hgrn_lb_param.astype(F32), axis=0)
    lower_bounds = jnp.cumsum(lb_p, axis=0) - lb_p[0]

    xf = x.reshape(t, d)
    memf = mem.reshape(b * m, d)
    for l in range(depth):
        p_rw, p_hg, p_gate, p_da = _in_proj(xf, norm_mix_g[l], _permute_w_in(w_in[l]))
        o_hg = _hgrn2(p_hg, lower_bounds[l], hgrn_norm_g[l], b, s)
        lambda_init = 0.8 - 0.6 * math.exp(-0.3 * l)
        o_da = _diff_attention(p_da.reshape(b, s, 3 * DA_WIDTH), diff_lambda[l], diff_subln_g[l],
                               lambda_init, b, s).reshape(t, DA_WIDTH)
        rw_params = dict(mu=rwkv_mu[l], w0=rwkv_w0[l], w_up=rwkv_w_up[l], a0=rwkv_a0[l],
                         a_up=rwkv_a_up[l], g_up=rwkv_g_up[l], k_k=rwkv_k_k[l], k_a=rwkv_k_a[l],
                         r_k=rwkv_r_k[l].reshape(-1))
        lw, r, k, v, kk, a, c3, g = _rwkv_prep(p_rw, rw_params, b, s)
        y = _rwkv_chunked(lw, r, k, v, kk, a, b, s)
        xf = _merge(xf, o_hg, o_da, y, v, c3, g, p_gate, b_gate[l], rwkv_ln_g[l], rwkv_ln_b[l],
                    w_branch[l], w_out[l])
        kv = _norm_matmul(memf, norm_mem_g[l], xa_wkv[l].astype(BF16), 256, 512)
        xf = _cross_attention(xf, norm_xa_g[l], xa_wq[l], kv.reshape(b, m, 2 * d), xa_wo[l], b, s)
        xf = _ffn(xf, norm_ffn_g[l], ffn_w_up[l], ffn_conv_w[l], ffn_conv_b[l], ffn_w_down[l], b, s,
                  out_gain=final_norm_g if l == depth - 1 else None)
    return xf.reshape(b, s, d)
```
